```python
import math
import jax, jax.numpy as jnp
from jax import lax
import numpy as np

D_MODEL = 2048
BATCH = 4
SEQ = 2048
DEPTH = 1

GRID_W = 64
CTX_LEN = 256
MIX_WIDTH = D_MODEL
POOL_WIDTH = MIX_WIDTH // 2
POOL_WINDOWS = (2, 4, 8, 16)
POOL_GC = POOL_WIDTH // len(POOL_WINDOWS)
DN_HEAD_DIM = 128
DN_WIDTH = MIX_WIDTH - POOL_WIDTH
DN_HEADS = DN_WIDTH // DN_HEAD_DIM
CONV_WIDTH = 5
CHUNK = 64
N_GROUPS = 4
EXPERTS_PER_GROUP = 8
N_EXPERTS = N_GROUPS * EXPERTS_PER_GROUP
TOP_K = 2
D_EXPERT = D_MODEL // 2
MOE_BLOCK = 128
EPS = 1e-6
Q0 = POOL_WIDTH
Z0 = Q0 + 3 * DN_WIDTH
AB0 = Z0 + DN_WIDTH
IN_COLS = AB0 + 4 * DN_HEADS

kernel_name = 'hybrid_pool_gdn_hmoe_dit_layer'


def _rmsnorm(x, g):
    xf = x.astype(jnp.float32)
    y = xf * lax.rsqrt(jnp.mean(xf * xf, axis=-1, keepdims=True) + EPS)
    return y.astype(x.dtype) * g


def _modulate(h, shift, scale):
    return h * (1 + scale) + shift


def _box_mean(u, window, rows, cols):
    b, t, ch = u.shape
    uf = u.astype(jnp.float32).reshape(b, rows, cols, ch)
    sat = jnp.pad(jnp.cumsum(jnp.cumsum(uf, axis=1), axis=2), ((0, 0), (1, 0), (1, 0), (0, 0)))
    lo = window // 2
    hi = window - lo
    def bounds(n):
        i = jnp.arange(n)
        return jnp.clip(i - lo, 0, n), jnp.clip(i + hi, 0, n)
    r0, r1 = bounds(rows)
    c0, c1 = bounds(cols)
    corner = lambda ri, ci: jnp.take(jnp.take(sat, ri, axis=1), ci, axis=2)
    total = corner(r1, c1) - corner(r0, c1) - corner(r1, c0) + corner(r0, c0)
    count = ((r1 - r0)[:, None] * (c1 - c0)[None, :]).astype(jnp.float32)
    return (total / count[None, :, :, None]).reshape(b, t, ch).astype(u.dtype)


def _pool_mixer(u, pool_w, pool_scale, rows, cols):
    outs = []
    for gi, win in enumerate(POOL_WINDOWS):
        ug = u[..., gi * POOL_GC:(gi + 1) * POOL_GC]
        outs.append((_box_mean(ug, win, rows, cols) - ug) @ pool_w[gi])
    return jnp.concatenate(outs, axis=-1) * pool_scale


def _short_conv(x, w):
    y = lax.conv_general_dilated(x, w[:, None, :], window_strides=(1,),
                                 padding=((CONV_WIDTH // 2, CONV_WIDTH // 2),),
                                 dimension_numbers=('NWC', 'WIO', 'NWC'),
                                 feature_group_count=x.shape[-1])
    return jax.nn.silu(y)


def _heads(a):
    b, t, _ = a.shape
    return a.reshape(b, t, DN_HEADS, DN_HEAD_DIM).transpose(0, 2, 1, 3)


def _l2norm(a):
    return a * lax.rsqrt(jnp.sum(a * a, axis=-1, keepdims=True) + EPS)


def _delta_inputs(proj, conv_w, a_log_f, dt_bias_f, a_log_b, dt_bias_b):
    qkv = _short_conv(proj[..., Q0:Z0], conv_w).astype(jnp.float32)
    q = _l2norm(_heads(qkv[..., :DN_WIDTH])) * (DN_HEAD_DIM ** -0.5)
    k = _l2norm(_heads(qkv[..., DN_WIDTH:2 * DN_WIDTH]))
    v = _heads(qkv[..., 2 * DN_WIDTH:])
    ab = proj[..., AB0:].astype(jnp.float32).transpose(0, 2, 1)
    h = DN_HEADS
    beta_f = jax.nn.sigmoid(ab[:, :h])
    beta_b = jax.nn.sigmoid(ab[:, h:2 * h])
    g_f = -jnp.exp(a_log_f.astype(jnp.float32))[:, None] * jax.nn.softplus(ab[:, 2 * h:3 * h] + dt_bias_f.astype(jnp.float32)[:, None])
    g_b = -jnp.exp(a_log_b.astype(jnp.float32))[:, None] * jax.nn.softplus(ab[:, 3 * h:] + dt_bias_b.astype(jnp.float32)[:, None])
    return q, k, v, (g_f, beta_f), (g_b, beta_b)


def _gated_delta_chunked(q, k, v, g, beta, s0):
    b, h, t, dk = q.shape
    dv = v.shape[-1]
    n = t // CHUNK
    split = lambda a: a.reshape(b, h, n, CHUNK, *a.shape[3:])
    q, k, v, g, beta = (split(a) for a in (q, k, v, g, beta))
    gc = jnp.cumsum(g, axis=-1)
    tri = jnp.tril(jnp.ones((CHUNK, CHUNK), dtype=bool))
    strict = tri & ~jnp.eye(CHUNK, dtype=bool)
    diff = gc[..., :, None] - gc[..., None, :]
    decay = jnp.where(tri, jnp.exp(jnp.where(tri, diff, 0.0)), 0.0)
    kb = k * beta[..., None]
    a_mat = jnp.where(strict, jnp.einsum('bhncd,bhnsd->bhncs', kb, k) * decay, 0.0)
    eye = jnp.eye(CHUNK, dtype=jnp.float32)
    t_inv = lax.linalg.triangular_solve(a_mat + eye, jnp.broadcast_to(eye, a_mat.shape),
                                        left_side=True, lower=True, unit_diagonal=True)
    u = t_inv @ (v * beta[..., None])
    w = t_inv @ (kb * jnp.exp(gc)[..., None])
    qk = jnp.einsum('bhncd,bhnsd->bhncs', q, k) * decay
    q_dec = q * jnp.exp(gc)[..., None]
    k_dec = k * jnp.exp(gc[..., -1:] - gc)[..., None]
    g_tot = jnp.exp(gc[..., -1])

    def step(s, xs):
        u_c, w_c, qk_c, qd_c, kd_c, gt_c = xs
        v_new = u_c - w_c @ s
        o_c = qd_c @ s + qk_c @ v_new
        s = s * gt_c[..., None, None] + jnp.einsum('bhcd,bhce->bhde', kd_c, v_new)
        return s, o_c

    xs = tuple(jnp.moveaxis(a, 2, 0) for a in (u, w, qk, q_dec, k_dec, g_tot))
    s, o = lax.scan(step, s0, xs)
    return jnp.moveaxis(o, 0, 2).reshape(b, h, t, dv), s


def _bidir_delta(lat, ctx):
    ql, kl, vl, fl, bl = lat
    qc, kc, vc, fc, bc = ctx
    b = ql.shape[0]
    s0 = jnp.zeros((b, DN_HEADS, DN_HEAD_DIM, DN_HEAD_DIM), jnp.float32)
    oc_f, s_f = _gated_delta_chunked(qc, kc, vc, fc[0], fc[1], s0)
    ol_f, _ = _gated_delta_chunked(ql, kl, vl, fl[0], fl[1], s_f)
    flip = lambda a: jnp.flip(a, axis=2)
    oc_b, s_b = _gated_delta_chunked(flip(qc), flip(kc), flip(vc), flip(bc[0]), flip(bc[1]), s0)
    ol_b, _ = _gated_delta_chunked(flip(ql), flip(kl), flip(vl), flip(bl[0]), flip(bl[1]), s_b)
    return ol_f + flip(ol_b), oc_f + flip(oc_b)


def _gated_out(o, z, out_norm_g):
    b, h, t, d = o.shape
    o = o.transpose(0, 2, 1, 3)
    o = o * lax.rsqrt(jnp.mean(o * o, axis=-1, keepdims=True) + EPS) * out_norm_g.astype(jnp.float32)
    return (o.reshape(b, t, h * d) * jax.nn.silu(z.astype(jnp.float32))).astype(z.dtype)


def _mix_stream(proj, o, pool_w, pool_scale, out_norm_g, w_out, rows, cols):
    pool = _pool_mixer(proj[..., :POOL_WIDTH], pool_w, pool_scale, rows, cols)
    dn = _gated_out(o, proj[..., Z0:AB0], out_norm_g)
    return jnp.concatenate([pool, dn], axis=-1) @ w_out


def _hier_moe(h, w_grp, b_grp, w_rt, b_rt, w1, w3, w2):
    n, d = h.shape
    hf = h.astype(jnp.float32)
    grp_logits = hf @ w_grp.astype(jnp.float32) + b_grp.astype(jnp.float32)
    g_sel = jnp.argmax(grp_logits, axis=-1)
    p_grp = jnp.take_along_axis(jax.nn.softmax(grp_logits, axis=-1), g_sel[:, None], axis=-1)
    e_logits = (hf @ w_rt.astype(jnp.float32) + b_rt.astype(jnp.float32)).reshape(n, N_GROUPS, EXPERTS_PER_GROUP)
    in_grp = jnp.take_along_axis(e_logits, g_sel[:, None, None], axis=1)[:, 0]
    top_val, top_loc = lax.top_k(in_grp, TOP_K)
    gate = jax.nn.softmax(top_val, axis=-1) * p_grp
    expert = (g_sel[:, None] * EXPERTS_PER_GROUP + top_loc).reshape(-1)
    a_cnt = n * TOP_K
    onehot = jax.nn.one_hot(expert, N_EXPERTS, dtype=jnp.int32)
    rank = jnp.sum((jnp.cumsum(onehot, axis=0) - onehot) * onehot, axis=-1)
    counts = jnp.sum(onehot, axis=0)
    padded = ((counts + MOE_BLOCK - 1) // MOE_BLOCK) * MOE_BLOCK
    ends = jnp.cumsum(padded)
    dest = (ends - padded)[expert] + rank
    n_blocks = (a_cnt + N_EXPERTS * (MOE_BLOCK - 1) + MOE_BLOCK - 1) // MOE_BLOCK
    tok = jnp.arange(a_cnt) // TOP_K
    x_pad = jnp.zeros((n_blocks * MOE_BLOCK, d), h.dtype).at[dest].set(h[tok])
    block_expert = jnp.minimum(jnp.searchsorted(ends, jnp.arange(n_blocks) * MOE_BLOCK, side='right'), N_EXPERTS - 1)

    def run_block(args):
        xb, e = args
        return (jax.nn.silu(xb @ w1[e]) * (xb @ w3[e])) @ w2[e]

    y_pad = lax.map(run_block, (x_pad.reshape(n_blocks, MOE_BLOCK, d), block_expert))
    y = y_pad.reshape(n_blocks * MOE_BLOCK, d)[dest].reshape(n, TOP_K, d)
    return jnp.einsum('nk,nkd->nd', gate.astype(h.dtype), y)


def setup_inputs(seed: int = 0) -> dict:
    key = jax.random.key(seed)
    ks = iter(jax.random.split(key, 40))
    nrm = lambda shape, scale: jax.random.normal(next(ks), shape, jnp.float32) * scale
    L, D, H = DEPTH, D_MODEL, DN_HEADS

    def dt_bias():
        dt = jnp.exp(jax.random.uniform(next(ks), (L, H), jnp.float32, math.log(1e-3), math.log(1e-1)))
        return dt + jnp.log(-jnp.expm1(-dt))

    def a_log():
        return jnp.log(jax.random.uniform(next(ks), (L, H), jnp.float32, 1.0, 16.0))

    return {
        'x': nrm((BATCH, SEQ, D), 1.0),
        'c': nrm((BATCH, D), 1.0),
        'ctx': nrm((BATCH, CTX_LEN, D), 1.0),
        'c_ctx': nrm((D,), 1.0),
        'w_mod': nrm((L, D, 6 * D), 0.5 * D ** -0.5),
        'b_mod': nrm((L, 6 * D), 0.02),
        'norm1_g': 1.0 + nrm((L, D), 0.05),
        'w_in': nrm((L, D, IN_COLS), D ** -0.5),
        'pool_w': nrm((L, len(POOL_WINDOWS), POOL_GC, POOL_GC), POOL_GC ** -0.5),
        'pool_scale': 1.0 + nrm((L, POOL_WIDTH), 0.1),
        'conv_w': nrm((L, CONV_WIDTH, 3 * DN_WIDTH), CONV_WIDTH ** -0.5),
        'a_log_f': a_log(),
        'dt_bias_f': dt_bias(),
        'a_log_b': a_log(),
        'dt_bias_b': dt_bias(),
        'out_norm_g': 1.0 + nrm((L, DN_HEAD_DIM), 0.05),
        'w_out': nrm((L, MIX_WIDTH, D), MIX_WIDTH ** -0.5),
        'norm2_g': 1.0 + nrm((L, D), 0.05),
        'w_grp': nrm((L, D, N_GROUPS), D ** -0.5),
        'b_grp': nrm((L, N_GROUPS), 0.01),
        'w_rt': nrm((L, D, N_EXPERTS), D ** -0.5),
        'b_rt': nrm((L, N_EXPERTS), 0.01),
        'w1': nrm((L, N_EXPERTS, D, D_EXPERT), D ** -0.5),
        'w3': nrm((L, N_EXPERTS, D, D_EXPERT), D ** -0.5),
        'w2': nrm((L, N_EXPERTS, D_EXPERT, D), D_EXPERT ** -0.5),
        'final_g': 1.0 + nrm((D,), 0.05),
    }


def reference(x, c, ctx, c_ctx, w_mod, b_mod, norm1_g, w_in, pool_w, pool_scale, conv_w,
              a_log_f, dt_bias_f, a_log_b, dt_bias_b, out_norm_g, w_out, norm2_g,
              w_grp, b_grp, w_rt, b_rt, w1, w3, w2, final_g):
    rows = x.shape[1] // GRID_W
    d = x.shape[-1]
    for l in range(DEPTH):
        last = l == DEPTH - 1
        mod = jax.nn.silu(c) @ w_mod[l] + b_mod[l]
        mod_c = jax.nn.silu(c_ctx) @ w_mod[l] + b_mod[l]
        sh1, sc1, gt1, sh2, sc2, gt2 = jnp.split(mod[:, None, :], 6, axis=-1)
        sh1c, sc1c, gt1c, sh2c, sc2c, gt2c = jnp.split(mod_c[None, None, :], 6, axis=-1)
        proj = _modulate(_rmsnorm(x, norm1_g[l]), sh1, sc1) @ w_in[l]
        proj_c = _modulate(_rmsnorm(ctx, norm1_g[l]), sh1c, sc1c) @ w_in[l]
        lat_in = _delta_inputs(proj, conv_w[l], a_log_f[l], dt_bias_f[l], a_log_b[l], dt_bias_b[l])
        ctx_in = _delta_inputs(proj_c, conv_w[l], a_log_f[l], dt_bias_f[l], a_log_b[l], dt_bias_b[l])
        o_lat, o_ctx = _bidir_delta(lat_in, ctx_in)
        x = x + gt1 * _mix_stream(proj, o_lat, pool_w[l], pool_scale[l], out_norm_g[l], w_out[l], rows, GRID_W)
        h2 = _modulate(_rmsnorm(x, norm2_g[l]), sh2, sc2)
        x = x + gt2 * _hier_moe(h2.reshape(-1, d), w_grp[l], b_grp[l], w_rt[l], b_rt[l], w1[l], w3[l], w2[l]).reshape(x.shape)
        if not last:
            ctx = ctx + gt1c * _mix_stream(proj_c, o_ctx, pool_w[l], pool_scale[l], out_norm_g[l], w_out[l], 1, ctx.shape[1])
            h2c = _modulate(_rmsnorm(ctx, norm2_g[l]), sh2c, sc2c)
            ctx = ctx + gt2c * _hier_moe(h2c.reshape(-1, d), w_grp[l], b_grp[l], w_rt[l], b_rt[l], w1[l], w3[l], w2[l]).reshape(ctx.shape)
    return _rmsnorm(x, final_g)
```

```python
import functools

import jax
import jax.numpy as jnp
from jax import lax
from jax.experimental import pallas as pl
from jax.experimental.pallas import tpu as pltpu

F32 = jnp.float32
BF16 = jnp.bfloat16

GRID_W = 64
POOL_WINDOWS = (2, 4, 8, 16)
HEAD_DIM = 128
CONV_WIDTH = 5
CHUNK = 64
N_GROUPS = 4
EXPERTS_PER_GROUP = 8
N_EXPERTS = N_GROUPS * EXPERTS_PER_GROUP
MOE_BLOCK = 128
EPS = 1e-6
LANES = 128
EXP_LANE0 = N_GROUPS

VMEM_LIMIT = 56 * 1024 * 1024


def _cparams(sem):
    return pltpu.CompilerParams(dimension_semantics=sem, vmem_limit_bytes=VMEM_LIMIT)


def _dot(a, b):
    return jnp.dot(a.astype(BF16), b.astype(BF16), preferred_element_type=F32)


def _dot_hi(a, b):
    return jnp.dot(a, b, preferred_element_type=F32, precision=lax.Precision.HIGHEST)


def _silu(x):
    return x * jax.nn.sigmoid(x)


def _mod_kernel(c_ref, w_ref, b_ref, o_ref):
    o_ref[...] = _dot(_silu(c_ref[...]), w_ref[...]) + b_ref[...]


def _mod_call(c8, w_mod, b_mod):
    d, n = w_mod.shape
    tn = 512
    return pl.pallas_call(
        _mod_kernel,
        grid=(n // tn,),
        in_specs=[pl.BlockSpec((8, d), lambda j: (0, 0)),
                  pl.BlockSpec((d, tn), lambda j: (0, j)),
                  pl.BlockSpec((1, tn), lambda j: (0, j))],
        out_specs=pl.BlockSpec((8, tn), lambda j: (0, j)),
        out_shape=jax.ShapeDtypeStruct((8, n), F32),
        compiler_params=_cparams(("parallel",)),
        name="mod",
    )(c8, w_mod, b_mod.reshape(1, n))


def _inproj_kernel(x_ref, mods_ref, g_ref, w_ref, wab_ref, o_ref, ab_ref, hn_ref, *, mod_row):
    b = pl.program_id(0)
    j = pl.program_id(2)

    @pl.when(j == 0)
    def _():
        x = x_ref[0]
        y = x * lax.rsqrt(jnp.mean(x * x, axis=-1, keepdims=True) + EPS) * g_ref[...]
        row = b if mod_row is None else mod_row
        sh = mods_ref[0, pl.ds(row, 1), :]
        sc = mods_ref[1, pl.ds(row, 1), :]
        hb = (y * (1 + sc) + sh).astype(BF16)
        hn_ref[...] = hb
        ab_ref[0] = jnp.dot(hb, wab_ref[...], preferred_element_type=F32)

    o_ref[0] = jnp.dot(hn_ref[...], w_ref[...], preferred_element_type=F32)


def _inproj_call(x, mods, g, w_bf, wab_bf, n_main, mod_row):
    bsz, t, d = x.shape
    tm = min(t, 1024)
    tn = 512
    kern = functools.partial(_inproj_kernel, mod_row=mod_row)
    return pl.pallas_call(
        kern,
        grid=(bsz, t // tm, n_main // tn),
        in_specs=[pl.BlockSpec((1, tm, d), lambda b, i, j: (b, i, 0)),
                  pl.BlockSpec((2, 8, d), lambda b, i, j: (0, 0, 0)),
                  pl.BlockSpec((1, d), lambda b, i, j: (0, 0)),
                  pl.BlockSpec((d, tn), lambda b, i, j: (0, j)),
                  pl.BlockSpec((d, LANES), lambda b, i, j: (0, 0))],
        out_specs=[pl.BlockSpec((1, tm, tn), lambda b, i, j: (b, i, j)),
                   pl.BlockSpec((1, tm, LANES), lambda b, i, j: (b, i, 0))],
        out_shape=[jax.ShapeDtypeStruct((bsz, t, n_main), F32),
                   jax.ShapeDtypeStruct((bsz, t, LANES), F32)],
        scratch_shapes=[pltpu.VMEM((tm, d), BF16)],
        compiler_params=_cparams(("parallel", "parallel", "arbitrary")),
        name="inproj",
    )(x, mods, g, w_bf, wab_bf)


def _gates_kernel(ab_ref, prm_ref, g_ref, gt_ref, *, heads):
    ab = ab_ref[0]
    t = ab.shape[0]
    h2, h3, h4, h6 = 2 * heads, 3 * heads, 4 * heads, 6 * heads
    beta = jax.nn.sigmoid(ab)
    xx = ab + prm_ref[1:2, :]
    softplus = jnp.maximum(xx, 0.0) + jnp.log1p(jnp.exp(-jnp.abs(xx)))
    g = -jnp.exp(prm_ref[0:1, :]) * softplus
    pos = lax.broadcasted_iota(jnp.int32, ab.shape, 0) & (CHUNK - 1)
    cs = g
    ss = g
    s = 1
    while s < CHUNK:
        cs = cs + jnp.where(pos >= s, pltpu.roll(cs, s, 0), 0.0)
        ss = ss + jnp.where(pos < CHUNK - s, pltpu.roll(ss, t - s, 0), 0.0)
        s *= 2
    tot = pltpu.roll(cs + ss - g, h2, 1)
    lane = lax.broadcasted_iota(jnp.int32, ab.shape, 1)
    out = jnp.where(lane < h2, beta,
                    jnp.where(lane < h3, cs,
                              jnp.where(lane < h4, ss,
                                        jnp.where(lane < h6, tot, 0.0))))
    g_ref[0] = out
    gt_ref[0] = out.T


def _gates_call(ab, prm, heads):
    bsz, t, _ = ab.shape
    return pl.pallas_call(
        functools.partial(_gates_kernel, heads=heads),
        grid=(bsz,),
        in_specs=[pl.BlockSpec((1, t, LANES), lambda b: (b, 0, 0)),
                  pl.BlockSpec((8, LANES), lambda b: (0, 0))],
        out_specs=[pl.BlockSpec((1, t, LANES), lambda b: (b, 0, 0)),
                   pl.BlockSpec((1, LANES, t), lambda b: (b, 0, 0))],
        out_shape=[jax.ShapeDtypeStruct((bsz, t, LANES), F32),
                   jax.ShapeDtypeStruct((bsz, LANES, t), F32)],
        compiler_params=_cparams(("parallel",)),
        name="gates",
    )(ab, prm)


def _conv_silu(x, w):
    n = x.shape[0]
    row = lax.broadcasted_iota(jnp.int32, x.shape, 0)
    acc = x * w[CONV_WIDTH // 2:CONV_WIDTH // 2 + 1, :]
    for j in range(CONV_WIDTH):
        d = j - CONV_WIDTH // 2
        if d == 0:
            continue
        xs = pltpu.roll(x, (-d) % n, 0)
        valid = (row + d >= 0) & (row + d < n)
        acc = acc + jnp.where(valid, xs, 0.0) * w[j:j + 1, :]
    return _silu(acc)


def _l2norm(a):
    return a * lax.rsqrt(jnp.sum(a * a, axis=-1, keepdims=True) + EPS)


def _lane_col(g, lane_idx):
    lane = lax.broadcasted_iota(jnp.int32, g.shape, 1)
    return jnp.sum(jnp.where(lane == lane_idx, g, 0.0), axis=-1, keepdims=True)


def _chunk_terms(kn_c, kb_c, qn_c, rhs_c, kd_c, qd_c, gcc, gcr, upper):
    c = kn_c.shape[0]
    ri = lax.broadcasted_iota(jnp.int32, (c, c), 0)
    ci = lax.broadcasted_iota(jnp.int32, (c, c), 1)
    incl = (ri <= ci) if upper else (ri >= ci)
    strict = (ri < ci) if upper else (ri > ci)
    dec = jnp.where(incl, jnp.exp(jnp.where(incl, gcc - gcr, 0.0)), 0.0)
    nt = (((1,), (1,)), ((), ()))
    kn_b = kn_c.astype(BF16)
    kk = lax.dot_general(kb_c.astype(BF16), kn_b, nt, preferred_element_type=F32)
    bm = -jnp.where(strict, kk * dec, 0.0)
    p = jnp.where(ri == ci, 1.0, 0.0) + bm
    n = 2
    while n < c:
        bm = _dot_hi(bm, bm)
        p = p + _dot_hi(p, bm)
        n *= 2
    uw = _dot(p, rhs_c)
    uw_b = uw.astype(BF16)
    tn = (((0,), (0,)), ((), ()))
    wn = lax.dot_general(kd_c.astype(BF16), uw_b, tn, preferred_element_type=F32)
    hd = kn_c.shape[1]
    nc, w2 = wn[:, :hd], wn[:, hd:]
    if qn_c is None:
        return w2, nc, None, None
    qk = lax.dot_general(qn_c.astype(BF16), kn_b, nt, preferred_element_type=F32) * dec
    qw = jnp.dot(qk.astype(BF16), uw_b, preferred_element_type=F32)
    return w2, nc, qd_c - qw[:, hd:], qw[:, :hd]


def _delta_kernel(q_ref, k_ref, v_ref, z_ref, kc_ref, vc_ref, cwq_ref, cwk_ref, cwv_ref,
                  g_ref, gc_ref, grf_ref, grb_ref, grcf_ref, grcb_ref, ong_ref,
                  out_ref,
                  qn_s, kn_s, kb_s, rhs_s, kd_s, qd_s, col_s,
                  knc_s, kbc_s, rhsc_s, kdc_s, colc_s,
                  w2_s, nc_s, qp_s, o0_s, w2c_s, ncc_s, o_s, *, heads):
    h = pl.program_id(1)
    t = q_ref.shape[1]
    tc = kc_ref.shape[1]
    hd = HEAD_DIM
    n_lat = t // CHUNK
    n_ctx = tc // CHUNK

    qn = _l2norm(_conv_silu(q_ref[0], cwq_ref[...])) * (hd ** -0.5)
    kn = _l2norm(_conv_silu(k_ref[0], cwk_ref[...]))
    vv = _conv_silu(v_ref[0], cwv_ref[...])
    knc = _l2norm(_conv_silu(kc_ref[0], cwk_ref[...]))
    vvc = _conv_silu(vc_ref[0], cwv_ref[...])
    qn_s[...] = qn
    kn_s[...] = kn
    knc_s[...] = knc
    g_lat = g_ref[0]
    g_ctx = gc_ref[0]

    for d in range(2):
        for (gt_, kn_, vv_, kb_r, rhs_r, kd_r, col_r, qn_, qd_r) in (
                (g_lat, kn, vv, kb_s, rhs_s, kd_s, col_s, qn, qd_s),
                (g_ctx, knc, vvc, kbc_s, rhsc_s, kdc_s, colc_s, None, None)):
            beta = _lane_col(gt_, d * heads + h)
            gcum = _lane_col(gt_, (2 + d) * heads + h)
            gtot = _lane_col(gt_, (4 + d) * heads + h)
            e = jnp.exp(gcum)
            kb = kn_ * beta
            kb_r[d] = kb
            rhs_r[d, :, 0:hd] = vv_ * beta
            rhs_r[d, :, hd:2 * hd] = kb * e
            kd_r[d] = kn_ * jnp.exp(gtot - gcum)
            lane = lax.broadcasted_iota(jnp.int32, (gcum.shape[0], LANES), 1)
            col_r[d] = jnp.where(lane == 0, gcum, jnp.where(lane == 1, jnp.exp(gtot), 0.0))
            if qn_ is not None:
                qd_r[d] = qn_ * e

    gr_lat = (grf_ref, grb_ref)
    gr_ctx = (grcf_ref, grcb_ref)

    def pre_ctx(i, carry):
        for d in range(2):
            c = i if d == 0 else n_ctx - 1 - i
            r0 = pl.multiple_of(c * CHUNK, CHUNK)
            rows = pl.ds(r0, CHUNK)
            w2, nc, _, _ = _chunk_terms(
                knc_s[rows, :], kbc_s[d, rows, :], None, rhsc_s[d, rows, :], kdc_s[d, rows, :], None,
                colc_s[d, rows, 0:1], gr_ctx[d][0, 0, pl.ds(c, 1), :], upper=(d == 1))
            m0 = pl.multiple_of(c * hd, hd)
            w2c_s[d, pl.ds(m0, hd), :] = w2
            ncc_s[d, pl.ds(m0, hd), :] = nc
        return carry

    lax.fori_loop(0, n_ctx, pre_ctx, 0)

    def pre_lat(i, carry):
        for d in range(2):
            c = i if d == 0 else n_lat - 1 - i
            r0 = pl.multiple_of(c * CHUNK, CHUNK)
            rows = pl.ds(r0, CHUNK)
            w2, nc, qp, o0 = _chunk_terms(
                kn_s[rows, :], kb_s[d, rows, :], qn_s[rows, :], rhs_s[d, rows, :], kd_s[d, rows, :],
                qd_s[d, rows, :], col_s[d, rows, 0:1], gr_lat[d][0, 0, pl.ds(c, 1), :], upper=(d == 1))
            m0 = pl.multiple_of(c * hd, hd)
            w2_s[d, pl.ds(m0, hd), :] = w2
            nc_s[d, pl.ds(m0, hd), :] = nc
            qp_s[d, rows, :] = qp
            o0_s[d, rows, :] = o0
        return carry

    lax.fori_loop(0, n_lat, pre_lat, 0)

    def scan_ctx(i, states):
        new = []
        for d in range(2):
            s = states[d]
            c = i if d == 0 else n_ctx - 1 - i
            m0 = pl.multiple_of(c * hd, hd)
            gt = colc_s[d, pl.ds(pl.multiple_of(c * CHUNK, CHUNK), 1), 1:2]
            s = gt * s + ncc_s[d, pl.ds(m0, hd), :] - _dot(w2c_s[d, pl.ds(m0, hd), :], s)
            new.append(s)
        return tuple(new)

    zero = jnp.zeros((hd, hd), F32)
    states = lax.fori_loop(0, n_ctx, scan_ctx, (zero, zero))

    def scan_lat(i, states):
        new = []
        for d in range(2):
            s = states[d]
            c = i if d == 0 else n_lat - 1 - i
            r0 = pl.multiple_of(c * CHUNK, CHUNK)
            rows = pl.ds(r0, CHUNK)
            m0 = pl.multiple_of(c * hd, hd)
            s_b = s.astype(BF16)
            o_s[d, rows, :] = jnp.dot(qp_s[d, rows, :].astype(BF16), s_b,
                                      preferred_element_type=F32) + o0_s[d, rows, :]
            gt = col_s[d, pl.ds(r0, 1), 1:2]
            s = gt * s + nc_s[d, pl.ds(m0, hd), :] - jnp.dot(
                w2_s[d, pl.ds(m0, hd), :].astype(BF16), s_b, preferred_element_type=F32)
            new.append(s)
        return tuple(new)

    lax.fori_loop(0, n_lat, scan_lat, states)

    o = o_s[0] + o_s[1]
    o = o * lax.rsqrt(jnp.mean(o * o, axis=-1, keepdims=True) + EPS) * ong_ref[...]
    out_ref[0] = (o * _silu(z_ref[0])).astype(out_ref.dtype)


def _delta_call(proj, proj_c, conv_w, g, gc, gr, grc, ong, heads, q_blk0, z_blk0):
    bsz, t, _ = proj.shape
    tc = proj_c.shape[1]
    hd = HEAD_DIM
    n_lat, n_ctx = t // CHUNK, tc // CHUNK

    def col(off):
        return lambda b, h: (b, 0, off + h)

    def cw(off):
        return lambda b, h: (0, off + h)

    def grow(off):
        return lambda b, h: (b, off + h, 0, 0)

    in_specs = [
        pl.BlockSpec((1, t, hd), col(q_blk0)),
        pl.BlockSpec((1, t, hd), col(q_blk0 + heads)),
        pl.BlockSpec((1, t, hd), col(q_blk0 + 2 * heads)),
        pl.BlockSpec((1, t, hd), col(z_blk0)),
        pl.BlockSpec((1, tc, hd), col(q_blk0 + heads)),
        pl.BlockSpec((1, tc, hd), col(q_blk0 + 2 * heads)),
        pl.BlockSpec((CONV_WIDTH, hd), cw(0)),
        pl.BlockSpec((CONV_WIDTH, hd), cw(heads)),
        pl.BlockSpec((CONV_WIDTH, hd), cw(2 * heads)),
        pl.BlockSpec((1, t, LANES), lambda b, h: (b, 0, 0)),
        pl.BlockSpec((1, tc, LANES), lambda b, h: (b, 0, 0)),
        pl.BlockSpec((1, 1, n_lat, CHUNK), grow(2 * heads)),
        pl.BlockSpec((1, 1, n_lat, CHUNK), grow(3 * heads)),
        pl.BlockSpec((1, 1, n_ctx, CHUNK), grow(2 * heads)),
        pl.BlockSpec((1, 1, n_ctx, CHUNK), grow(3 * heads)),
        pl.BlockSpec((1, hd), lambda b, h: (0, 0)),
    ]
    scratch = [
        pltpu.VMEM((t, hd), F32), pltpu.VMEM((t, hd), F32),
        pltpu.VMEM((2, t, hd), F32), pltpu.VMEM((2, t, 2 * hd), F32),
        pltpu.VMEM((2, t, hd), F32), pltpu.VMEM((2, t, hd), F32),
        pltpu.VMEM((2, t, LANES), F32),
        pltpu.VMEM((tc, hd), F32), pltpu.VMEM((2, tc, hd), F32),
        pltpu.VMEM((2, tc, 2 * hd), F32), pltpu.VMEM((2, tc, hd), F32),
        pltpu.VMEM((2, tc, LANES), F32),
        pltpu.VMEM((2, n_lat * hd, hd), F32), pltpu.VMEM((2, n_lat * hd, hd), F32),
        pltpu.VMEM((2, t, hd), F32), pltpu.VMEM((2, t, hd), F32),
        pltpu.VMEM((2, n_ctx * hd, hd), F32), pltpu.VMEM((2, n_ctx * hd, hd), F32),
        pltpu.VMEM((2, t, hd), F32),
    ]
    return pl.pallas_call(
        functools.partial(_delta_kernel, heads=heads),
        grid=(bsz, heads),
        in_specs=in_specs,
        out_specs=pl.BlockSpec((1, t, hd), lambda b, h: (b, 0, h)),
        out_shape=jax.ShapeDtypeStruct((bsz, t, heads * hd), BF16),
        scratch_shapes=scratch,
        compiler_params=_cparams(("parallel", "parallel")),
        name="delta",
    )(proj, proj, proj, proj, proj_c, proj_c, conv_w, conv_w, conv_w, g, gc, gr, gr, grc, grc, ong)


def _shift_rows(x, d, idx, size, stride):
    n = x.shape[0]
    xs = pltpu.roll(x, (-d * stride) % n, 0)
    return jnp.where((idx + d >= 0) & (idx + d < size), xs, 0.0)


def _box_sum_1d(x, win, idx, size, stride):
    m = win // 2
    lead = x
    trail = x
    k = 1
    while k < m:
        lead = lead + _shift_rows(lead, k, idx, size, stride)
        trail = trail + _shift_rows(trail, -k, idx, size, stride)
        k *= 2
    return lead + _shift_rows(trail, -1, idx, size, stride)


def _pool_kernel(u_ref, pw_ref, ps_ref, o_ref, *, rows, cols):
    t = u_ref.shape[1]
    gc = pw_ref.shape[1]
    tok = lax.broadcasted_iota(jnp.int32, (t, gc), 0)
    ci = tok % cols
    ri = tok // cols
    for gi, win in enumerate(POOL_WINDOWS):
        lo = win // 2
        hi = win - lo
        u = u_ref[0, :, gi * gc:(gi + 1) * gc]
        s = _box_sum_1d(u, win, ci, cols, 1)
        s = _box_sum_1d(s, win, ri, rows, cols)
        cnt_c = jnp.minimum(ci + hi, cols) - jnp.maximum(ci - lo, 0)
        cnt_r = jnp.minimum(ri + hi, rows) - jnp.maximum(ri - lo, 0)
        mean = s / (cnt_c * cnt_r).astype(F32)
        y = _dot(mean - u, pw_ref[gi]) * ps_ref[:, gi * gc:(gi + 1) * gc]
        o_ref[0, :, gi * gc:(gi + 1) * gc] = y.astype(o_ref.dtype)


def _pool_call(proj, pool_w, pool_scale, rows, cols):
    bsz, t, _ = proj.shape
    ng, gc, _ = pool_w.shape
    pwid = ng * gc
    return pl.pallas_call(
        functools.partial(_pool_kernel, rows=rows, cols=cols),
        grid=(bsz,),
        in_specs=[pl.BlockSpec((1, t, pwid), lambda b: (b, 0, 0)),
                  pl.BlockSpec((ng, gc, gc), lambda b: (0, 0, 0)),
                  pl.BlockSpec((1, pwid), lambda b: (0, 0))],
        out_specs=pl.BlockSpec((1, t, pwid), lambda b: (b, 0, 0)),
        out_shape=jax.ShapeDtypeStruct((bsz, t, pwid), BF16),
        compiler_params=_cparams(("parallel",)),
        name="pool",
    )(proj, pool_w, pool_scale)


def _outproj_kernel(pool_ref, dn_ref, wa_ref, wb_ref, x_ref, mods_ref, g2_ref, wr_ref, br_ref,
                    x1_ref, h2_ref, lg_ref):
    b = pl.program_id(0)
    mix = (jnp.dot(pool_ref[0], wa_ref[...], preferred_element_type=F32)
           + jnp.dot(dn_ref[0], wb_ref[...], preferred_element_type=F32))
    gt1 = mods_ref[2, pl.ds(b, 1), :]
    sh2 = mods_ref[3, pl.ds(b, 1), :]
    sc2 = mods_ref[4, pl.ds(b, 1), :]
    x1 = x_ref[0] + gt1 * mix
    x1_ref[0] = x1
    y = x1 * lax.rsqrt(jnp.mean(x1 * x1, axis=-1, keepdims=True) + EPS) * g2_ref[...]
    h2 = y * (1 + sc2) + sh2
    h2_ref[0] = h2
    lg_ref[0] = _dot_hi(h2, wr_ref[...]) + br_ref[...]


def _outproj_call(pool, dn, w_out_bf, x, mods, g2, wr, br):
    bsz, t, d = x.shape
    half = pool.shape[-1]
    tm = min(t, 256)
    return pl.pallas_call(
        _outproj_kernel,
        grid=(bsz, t // tm),
        in_specs=[pl.BlockSpec((1, tm, half), lambda b, i: (b, i, 0)),
                  pl.BlockSpec((1, tm, half), lambda b, i: (b, i, 0)),
                  pl.BlockSpec((half, d), lambda b, i: (0, 0)),
                  pl.BlockSpec((half, d), lambda b, i: (1, 0)),
                  pl.BlockSpec((1, tm, d), lambda b, i: (b, i, 0)),
                  pl.BlockSpec((6, 8, d), lambda b, i: (0, 0, 0)),
                  pl.BlockSpec((1, d), lambda b, i: (0, 0)),
                  pl.BlockSpec((d, LANES), lambda b, i: (0, 0)),
                  pl.BlockSpec((1, LANES), lambda b, i: (0, 0))],
        out_specs=[pl.BlockSpec((1, tm, d), lambda b, i: (b, i, 0)),
                   pl.BlockSpec((1, tm, d), lambda b, i: (b, i, 0)),
                   pl.BlockSpec((1, tm, LANES), lambda b, i: (b, i, 0))],
        out_shape=[jax.ShapeDtypeStruct((bsz, t, d), F32),
                   jax.ShapeDtypeStruct((bsz, t, d), F32),
                   jax.ShapeDtypeStruct((bsz, t, LANES), F32)],
        compiler_params=_cparams(("parallel", "parallel")),
        name="outproj",
    )(pool, dn, w_out_bf, w_out_bf, x, mods, g2, wr, br)


def _router_kernel(lg_ref, dest_ref, gate_ref, be_ref, cnt_s, run_s, off_s):
    p = pl.program_id(0)
    i = pl.program_id(1)
    lg = lg_ref[...]
    tm = lg.shape[0]
    lane = lax.broadcasted_iota(jnp.int32, lg.shape, 1)
    lane_f = lane.astype(F32)
    neg = -jnp.inf
    big = float(LANES)

    grp = jnp.where(lane < N_GROUPS, lg, neg)
    gmax = jnp.max(grp, axis=-1, keepdims=True)
    gidx = jnp.min(jnp.where(grp == gmax, lane_f, big), axis=-1, keepdims=True)
    p_grp = 1.0 / jnp.sum(jnp.where(lane < N_GROUPS, jnp.exp(lg - gmax), 0.0), axis=-1, keepdims=True)
    lo = EXP_LANE0 + EXPERTS_PER_GROUP * gidx
    ev = jnp.where((lane_f >= lo) & (lane_f < lo + EXPERTS_PER_GROUP), lg, neg)
    t1 = jnp.max(ev, axis=-1, keepdims=True)
    i1 = jnp.min(jnp.where(ev == t1, lane_f, big), axis=-1, keepdims=True)
    ev2 = jnp.where(lane_f == i1, neg, ev)
    t2 = jnp.max(ev2, axis=-1, keepdims=True)
    i2 = jnp.min(jnp.where(ev2 == t2, lane_f, big), axis=-1, keepdims=True)
    oh1 = lane_f == i1
    oh2 = lane_f == i2
    cnt = oh1.astype(F32) + oh2.astype(F32)
    colsum = jnp.sum(cnt, axis=0, keepdims=True)

    @pl.when(p == 0)
    def _():
        @pl.when(i == 0)
        def _():
            cnt_s[...] = jnp.zeros_like(cnt_s)

        cnt_s[...] += colsum

    @pl.when(p == 1)
    def _():
        @pl.when(i == 0)
        def _():
            nblk = jnp.floor((cnt_s[...] + (MOE_BLOCK - 1)) * (1.0 / MOE_BLOCK))
            r = lax.broadcasted_iota(jnp.int32, (LANES, LANES), 0)
            c = lax.broadcasted_iota(jnp.int32, (LANES, LANES), 1)
            tri = (r < c).astype(BF16)
            nb8 = jnp.broadcast_to(nblk, (8, LANES))
            start_blk = jnp.dot(nb8.astype(BF16), tri, preferred_element_type=F32)[0:1, :]
            off_s[...] = start_blk * MOE_BLOCK
            run_s[...] = jnp.zeros_like(run_s)
            end_blk = start_blk + nblk
            nb = be_ref.shape[0]
            blk = lax.broadcasted_iota(jnp.int32, (nb, LANES), 0).astype(F32)
            ln = lax.broadcasted_iota(jnp.int32, (nb, LANES), 1)
            is_exp = (ln >= EXP_LANE0) & (ln < EXP_LANE0 + N_EXPERTS)
            done = jnp.sum(jnp.where(is_exp & (end_blk <= blk), 1.0, 0.0), axis=-1, keepdims=True)
            bexp = jnp.minimum(done, N_EXPERTS - 1.0)
            used = jnp.max(jnp.where(is_exp, end_blk, 0.0), axis=-1, keepdims=True)
            be_ref[...] = jnp.where(ln == 0, bexp, jnp.where(ln == 1, used, 0.0)).astype(jnp.int32)

        rr = lax.broadcasted_iota(jnp.int32, (tm, tm), 0)
        cc = lax.broadcasted_iota(jnp.int32, (tm, tm), 1)
        before = (cc < rr).astype(BF16)
        prefix = jnp.dot(before, cnt.astype(BF16), preferred_element_type=F32)
        base = off_s[...] + run_s[...] + prefix
        d1 = jnp.sum(jnp.where(oh1, base, 0.0), axis=-1, keepdims=True)
        d2 = jnp.sum(jnp.where(oh2, base, 0.0), axis=-1, keepdims=True)
        run_s[...] += colsum
        dd = jnp.exp(t2 - t1)
        g1 = p_grp / (1.0 + dd)
        g2 = p_grp * dd / (1.0 + dd)
        dest_ref[...] = jnp.where(lane == 0, d1, jnp.where(lane == 1, d2, 0.0)).astype(jnp.int32)
        gate_ref[...] = jnp.where(lane == 0, g1, jnp.where(lane == 1, g2, 0.0))


def _router_call(logits, n_blocks):
    n = logits.shape[0]
    tm = min(n, 512)
    nb_pad = ((n_blocks + 7) // 8) * 8
    return pl.pallas_call(
        _router_kernel,
        grid=(2, n // tm),
        in_specs=[pl.BlockSpec((tm, LANES), lambda p, i: (i, 0))],
        out_specs=[pl.BlockSpec((tm, LANES), lambda p, i: (i * p, 0)),
                   pl.BlockSpec((tm, LANES), lambda p, i: (i * p, 0)),
                   pl.BlockSpec((nb_pad, LANES), lambda p, i: (0, 0))],
        out_shape=[jax.ShapeDtypeStruct((n, LANES), jnp.int32),
                   jax.ShapeDtypeStruct((n, LANES), F32),
                   jax.ShapeDtypeStruct((nb_pad, LANES), jnp.int32)],
        scratch_shapes=[pltpu.VMEM((1, LANES), F32), pltpu.VMEM((1, LANES), F32),
                        pltpu.VMEM((1, LANES), F32)],
        compiler_params=_cparams(("arbitrary", "arbitrary")),
        name="router",
    )(logits)


DMA_WINDOW = 16


def _dispatch_kernel(dest_ref, h_ref, xz_ref, xp_ref, sem):
    del xz_ref
    n = h_ref.shape[0]

    def copy(tok, k):
        return pltpu.make_async_copy(h_ref.at[pl.ds(tok, 1)], xp_ref.at[pl.ds(dest_ref[2 * tok + k], 1)], sem)

    def body(tok, carry):
        copy(tok, 0).start()
        copy(tok, 1).start()

        @pl.when(tok >= DMA_WINDOW)
        def _():
            copy(tok - DMA_WINDOW, 0).wait()
            copy(tok - DMA_WINDOW, 1).wait()

        return carry

    lax.fori_loop(0, n, body, 0)

    def drain(tok, carry):
        copy(tok, 0).wait()
        copy(tok, 1).wait()
        return carry

    lax.fori_loop(max(n - DMA_WINDOW, 0), n, drain, 0)


def _dispatch_call(dest_flat, h2, n_rows):
    n, d = h2.shape
    zeros = jnp.zeros((n_rows, d), h2.dtype)
    grid_spec = pltpu.PrefetchScalarGridSpec(
        num_scalar_prefetch=1,
        grid=(1,),
        in_specs=[pl.BlockSpec(memory_space=pl.ANY), pl.BlockSpec(memory_space=pl.ANY)],
        out_specs=pl.BlockSpec(memory_space=pl.ANY),
        scratch_shapes=[pltpu.SemaphoreType.DMA(())],
    )
    return pl.pallas_call(
        _dispatch_kernel,
        grid_spec=grid_spec,
        out_shape=jax.ShapeDtypeStruct((n_rows, d), h2.dtype),
        input_output_aliases={2: 0},
        compiler_params=_cparams(("arbitrary",)),
        name="dispatch",
    )(dest_flat, h2, zeros)


def _gmm_kernel(be_ref, used_ref, x_ref, w1_ref, w3_ref, w2_ref, o_ref):
    j = pl.program_id(0)

    @pl.when(j < used_ref[0])
    def _():
        xb = x_ref[...].astype(BF16)
        h1 = jnp.dot(xb, w1_ref[0], preferred_element_type=F32)
        h3 = jnp.dot(xb, w3_ref[0], preferred_element_type=F32)
        a = (_silu(h1) * h3).astype(BF16)
        o_ref[...] = jnp.dot(a, w2_ref[0], preferred_element_type=F32)

    @pl.when(j >= used_ref[0])
    def _():
        o_ref[...] = jnp.zeros_like(o_ref)


def _gmm_call(block_expert, used, x_pad, w1b, w3b, w2b):
    n_rows, d = x_pad.shape
    de = w1b.shape[-1]
    nb = n_rows // MOE_BLOCK

    def blk(j, be, used):
        return jnp.minimum(j, used[0] - 1)

    grid_spec = pltpu.PrefetchScalarGridSpec(
        num_scalar_prefetch=2,
        grid=(nb,),
        in_specs=[pl.BlockSpec((MOE_BLOCK, d), lambda j, be, u: (blk(j, be, u), 0)),
                  pl.BlockSpec((1, d, de), lambda j, be, u: (be[blk(j, be, u)], 0, 0)),
                  pl.BlockSpec((1, d, de), lambda j, be, u: (be[blk(j, be, u)], 0, 0)),
                  pl.BlockSpec((1, de, d), lambda j, be, u: (be[blk(j, be, u)], 0, 0))],
        out_specs=pl.BlockSpec((MOE_BLOCK, d), lambda j, be, u: (j, 0)),
    )
    return pl.pallas_call(
        _gmm_kernel,
        grid_spec=grid_spec,
        out_shape=jax.ShapeDtypeStruct((n_rows, d), F32),
        compiler_params=_cparams(("arbitrary",)),
        name="gmm",
    )(block_expert, used, x_pad, w1b, w3b, w2b)


def _combine_kernel(dest_ref, y_ref, x1_ref, gate_ref, mods_ref, fg_ref, o_ref, ybuf, sem, *, seq):
    i = pl.program_id(0)
    tm = x1_ref.shape[0]
    tok0 = i * tm

    def copy(r, k):
        return pltpu.make_async_copy(y_ref.at[pl.ds(dest_ref[2 * (tok0 + r) + k], 1)],
                                     ybuf.at[k, pl.ds(r, 1)], sem)

    def start(r, carry):
        copy(r, 0).start()
        copy(r, 1).start()
        return carry

    lax.fori_loop(0, tm, start, 0)

    def wait(r, carry):
        copy(r, 0).wait()
        copy(r, 1).wait()
        return carry

    lax.fori_loop(0, tm, wait, 0)

    b = tok0 // seq
    gt2 = mods_ref[5, pl.ds(b, 1), :]
    gate = gate_ref[...]
    moe = gate[:, 0:1] * ybuf[0] + gate[:, 1:2] * ybuf[1]
    x = x1_ref[...] + gt2 * moe
    o_ref[...] = x * lax.rsqrt(jnp.mean(x * x, axis=-1, keepdims=True) + EPS) * fg_ref[...]


def _combine_call(dest_flat, y_pad, x1, gate, mods, final_g, seq):
    n, d = x1.shape
    tm = min(seq, 256)
    grid_spec = pltpu.PrefetchScalarGridSpec(
        num_scalar_prefetch=1,
        grid=(n // tm,),
        in_specs=[pl.BlockSpec(memory_space=pl.ANY),
                  pl.BlockSpec((tm, d), lambda i, dest: (i, 0)),
                  pl.BlockSpec((tm, LANES), lambda i, dest: (i, 0)),
                  pl.BlockSpec((6, 8, d), lambda i, dest: (0, 0, 0)),
                  pl.BlockSpec((1, d), lambda i, dest: (0, 0))],
        out_specs=pl.BlockSpec((tm, d), lambda i, dest: (i, 0)),
        scratch_shapes=[pltpu.VMEM((2, tm, d), F32), pltpu.SemaphoreType.DMA(())],
    )
    return pl.pallas_call(
        functools.partial(_combine_kernel, seq=seq),
        grid_spec=grid_spec,
        out_shape=jax.ShapeDtypeStruct((n, d), F32),
        compiler_params=_cparams(("arbitrary",)),
        name="combine",
    )(dest_flat, y_pad, x1, gate, mods, final_g)


def _pad_lanes(a, lane0):
    return jnp.zeros((LANES,), F32).at[lane0:lane0 + a.shape[0]].set(a.astype(F32))


def kernel(x, c, ctx, c_ctx, w_mod, b_mod, norm1_g, w_in, pool_w, pool_scale, conv_w,
           a_log_f, dt_bias_f, a_log_b, dt_bias_b, out_norm_g, w_out, norm2_g,
           w_grp, b_grp, w_rt, b_rt, w1, w3, w2, final_g):
    bsz, t, d = x.shape
    depth = w_mod.shape[0]
    assert depth == 1, "single-layer problem: the context stream is read but never updated"
    heads = a_log_f.shape[1]
    pool_width = pool_w.shape[1] * pool_w.shape[2]
    dn_width = heads * HEAD_DIM
    q0 = pool_width
    z0 = q0 + 3 * dn_width
    ab0 = z0 + dn_width
    n_tok = bsz * t
    l = 0

    c8 = jnp.zeros((8, d), F32).at[:bsz].set(c).at[bsz].set(c_ctx)
    mod = _mod_call(c8, w_mod[l], b_mod[l])
    mods = mod.reshape(8, 6, d).transpose(1, 0, 2)

    w_in_bf = w_in[l].astype(BF16)
    wab_bf = jnp.zeros((d, LANES), BF16).at[:, :4 * heads].set(w_in_bf[:, ab0:])
    g1 = norm1_g[l].reshape(1, d)
    proj, ab = _inproj_call(x, mods, g1, w_in_bf, wab_bf, ab0, None)
    proj_c, ab_c = _inproj_call(ctx, mods, g1, w_in_bf, wab_bf, ab0, bsz)

    prm = jnp.zeros((8, LANES), F32)
    prm = prm.at[0].set(_pad_lanes(jnp.concatenate([a_log_f[l], a_log_b[l]]), 2 * heads))
    prm = prm.at[1].set(_pad_lanes(jnp.concatenate([dt_bias_f[l], dt_bias_b[l]]), 2 * heads))
    g, g_t = _gates_call(ab, prm, heads)
    gc, gc_t = _gates_call(ab_c, prm, heads)
    gr = g_t.reshape(bsz, LANES, t // CHUNK, CHUNK)
    grc = gc_t.reshape(bsz, LANES, ctx.shape[1] // CHUNK, CHUNK)

    dn = _delta_call(proj, proj_c, conv_w[l], g, gc, gr, grc, out_norm_g[l].reshape(1, HEAD_DIM),
                     heads, q0 // HEAD_DIM, z0 // HEAD_DIM)
    pool = _pool_call(proj, pool_w[l], pool_scale[l].reshape(1, pool_width), t // GRID_W, GRID_W)

    wr = jnp.zeros((d, LANES), F32).at[:, :N_GROUPS].set(w_grp[l]).at[:, EXP_LANE0:EXP_LANE0 + N_EXPERTS].set(w_rt[l])
    br = jnp.zeros((LANES,), F32).at[:N_GROUPS].set(b_grp[l]).at[EXP_LANE0:EXP_LANE0 + N_EXPERTS].set(b_rt[l])
    x1, h2, logits = _outproj_call(pool, dn, w_out[l].astype(BF16), x, mods, norm2_g[l].reshape(1, d),
                                   wr, br.reshape(1, LANES))

    n_blocks = (n_tok * 2 + N_EXPERTS * (MOE_BLOCK - 1) + MOE_BLOCK - 1) // MOE_BLOCK
    dest, gate, be = _router_call(logits.reshape(n_tok, LANES), n_blocks)
    dest_flat = dest[:, :2].reshape(-1)
    block_expert = be[:n_blocks, 0]
    used = be[0:1, 1]
    x_pad = _dispatch_call(dest_flat, h2.reshape(n_tok, d), n_blocks * MOE_BLOCK)
    y_pad = _gmm_call(block_expert, used, x_pad, w1[l].astype(BF16), w3[l].astype(BF16), w2[l].astype(BF16))
    out = _combine_call(dest_flat, y_pad, x1.reshape(n_tok, d), gate, mods, final_g.reshape(1, d), t)
    return out.reshape(bsz, t, d)
```

```python
import functools

import jax
import jax.numpy as jnp
from jax import lax
from jax.experimental import pallas as pl
from jax.experimental.pallas import tpu as pltpu

F32 = jnp.float32
BF16 = jnp.bfloat16

GRID_W = 64
POOL_WINDOWS = (2, 4, 8, 16)
HEAD_DIM = 128
CONV_WIDTH = 5
CHUNK = 64
N_GROUPS = 4
EXPERTS_PER_GROUP = 8
N_EXPERTS = N_GROUPS * EXPERTS_PER_GROUP
MOE_BLOCK = 128
EPS = 1e-6
LANES = 128
PRE_UNROLL = 4
EXP_LANE0 = N_GROUPS

VMEM_LIMIT = 56 * 1024 * 1024


def _cparams(sem):
    return pltpu.CompilerParams(dimension_semantics=sem, vmem_limit_bytes=VMEM_LIMIT)


def _dot(a, b):
    return jnp.dot(a.astype(BF16), b.astype(BF16), preferred_element_type=F32)


def _dot_hi(a, b):
    return jnp.dot(a, b, preferred_element_type=F32, precision=lax.Precision.HIGHEST)


def _silu(x):
    return x * jax.nn.sigmoid(x)


def _mod_kernel(c_ref, w_ref, b_ref, o_ref):
    o_ref[...] = _dot(_silu(c_ref[...]), w_ref[...]) + b_ref[...]


def _mod_call(c8, w_mod, b_mod):
    d, n = w_mod.shape
    tn = 512
    return pl.pallas_call(
        _mod_kernel,
        grid=(n // tn,),
        in_specs=[pl.BlockSpec((8, d), lambda j: (0, 0)),
                  pl.BlockSpec((d, tn), lambda j: (0, j)),
                  pl.BlockSpec((1, tn), lambda j: (0, j))],
        out_specs=pl.BlockSpec((8, tn), lambda j: (0, j)),
        out_shape=jax.ShapeDtypeStruct((8, n), F32),
        compiler_params=_cparams(("parallel",)),
        name="mod",
    )(c8, w_mod, b_mod.reshape(1, n))


def _inproj_kernel(x_ref, mods_ref, g_ref, w_ref, wab_ref, o_ref, ab_ref, hn_ref, *, mod_row):
    b = pl.program_id(0)
    j = pl.program_id(2)

    @pl.when(j == 0)
    def _():
        x = x_ref[0]
        y = x * lax.rsqrt(jnp.mean(x * x, axis=-1, keepdims=True) + EPS) * g_ref[...]
        row = b if mod_row is None else mod_row
        sh = mods_ref[0, pl.ds(row, 1), :]
        sc = mods_ref[1, pl.ds(row, 1), :]
        hb = (y * (1 + sc) + sh).astype(BF16)
        hn_ref[...] = hb
        ab_ref[0] = jnp.dot(hb, wab_ref[...], preferred_element_type=F32)

    o_ref[0] = jnp.dot(hn_ref[...], w_ref[...], preferred_element_type=F32)


def _inproj_call(x, mods, g, w_bf, wab_bf, n_main, mod_row):
    bsz, t, d = x.shape
    tm = min(t, 1024)
    tn = 512
    kern = functools.partial(_inproj_kernel, mod_row=mod_row)
    return pl.pallas_call(
        kern,
        grid=(bsz, t // tm, n_main // tn),
        in_specs=[pl.BlockSpec((1, tm, d), lambda b, i, j: (b, i, 0)),
                  pl.BlockSpec((2, 8, d), lambda b, i, j: (0, 0, 0)),
                  pl.BlockSpec((1, d), lambda b, i, j: (0, 0)),
                  pl.BlockSpec((d, tn), lambda b, i, j: (0, j)),
                  pl.BlockSpec((d, LANES), lambda b, i, j: (0, 0))],
        out_specs=[pl.BlockSpec((1, tm, tn), lambda b, i, j: (b, i, j)),
                   pl.BlockSpec((1, tm, LANES), lambda b, i, j: (b, i, 0))],
        out_shape=[jax.ShapeDtypeStruct((bsz, t, n_main), F32),
                   jax.ShapeDtypeStruct((bsz, t, LANES), F32)],
        scratch_shapes=[pltpu.VMEM((tm, d), BF16)],
        compiler_params=_cparams(("parallel", "parallel", "arbitrary")),
        name="inproj",
    )(x, mods, g, w_bf, wab_bf)


def _gates_kernel(ab_ref, prm_ref, g_ref, gt_ref, *, heads):
    ab = ab_ref[0]
    t = ab.shape[0]
    h2, h3, h4, h6 = 2 * heads, 3 * heads, 4 * heads, 6 * heads
    beta = jax.nn.sigmoid(ab)
    xx = ab + prm_ref[1:2, :]
    softplus = jnp.maximum(xx, 0.0) + jnp.log1p(jnp.exp(-jnp.abs(xx)))
    g = -jnp.exp(prm_ref[0:1, :]) * softplus
    pos = lax.broadcasted_iota(jnp.int32, ab.shape, 0) & (CHUNK - 1)
    cs = g
    ss = g
    s = 1
    while s < CHUNK:
        cs = cs + jnp.where(pos >= s, pltpu.roll(cs, s, 0), 0.0)
        ss = ss + jnp.where(pos < CHUNK - s, pltpu.roll(ss, t - s, 0), 0.0)
        s *= 2
    tot = pltpu.roll(cs + ss - g, h2, 1)
    lane = lax.broadcasted_iota(jnp.int32, ab.shape, 1)
    out = jnp.where(lane < h2, beta,
                    jnp.where(lane < h3, cs,
                              jnp.where(lane < h4, ss,
                                        jnp.where(lane < h6, tot, 0.0))))
    g_ref[0] = out
    gt_ref[0] = out.T


def _gates_call(ab, prm, heads):
    bsz, t, _ = ab.shape
    return pl.pallas_call(
        functools.partial(_gates_kernel, heads=heads),
        grid=(bsz,),
        in_specs=[pl.BlockSpec((1, t, LANES), lambda b: (b, 0, 0)),
                  pl.BlockSpec((8, LANES), lambda b: (0, 0))],
        out_specs=[pl.BlockSpec((1, t, LANES), lambda b: (b, 0, 0)),
                   pl.BlockSpec((1, LANES, t), lambda b: (b, 0, 0))],
        out_shape=[jax.ShapeDtypeStruct((bsz, t, LANES), F32),
                   jax.ShapeDtypeStruct((bsz, LANES, t), F32)],
        compiler_params=_cparams(("parallel",)),
        name="gates",
    )(ab, prm)


def _conv_silu(x, w):
    n = x.shape[0]
    row = lax.broadcasted_iota(jnp.int32, x.shape, 0)
    acc = x * w[CONV_WIDTH // 2:CONV_WIDTH // 2 + 1, :]
    for j in range(CONV_WIDTH):
        d = j - CONV_WIDTH // 2
        if d == 0:
            continue
        xs = pltpu.roll(x, (-d) % n, 0)
        valid = (row + d >= 0) & (row + d < n)
        acc = acc + jnp.where(valid, xs, 0.0) * w[j:j + 1, :]
    return _silu(acc)


def _l2norm(a):
    return a * lax.rsqrt(jnp.sum(a * a, axis=-1, keepdims=True) + EPS)


def _lane_col(g, lane_idx):
    lane = lax.broadcasted_iota(jnp.int32, g.shape, 1)
    return jnp.sum(jnp.where(lane == lane_idx, g, 0.0), axis=-1, keepdims=True)


def _chunk_terms(chains):
    c, hd = chains[0][0].shape
    ri = lax.broadcasted_iota(jnp.int32, (c, c), 0)
    ci = lax.broadcasted_iota(jnp.int32, (c, c), 1)
    eye = jnp.where(ri == ci, 1.0, 0.0)
    right = lax.broadcasted_iota(jnp.int32, (c, 2 * c), 1) >= c
    nt = (((1,), (1,)), ((), ()))
    tn = (((0,), (0,)), ((), ()))

    decs, kn_bs, zs = [], [], []
    for kn_c, kb_c, _, _, _, _, gcc, gcr, upper in chains:
        incl = (ri <= ci) if upper else (ri >= ci)
        strict = (ri < ci) if upper else (ri > ci)
        dec = jnp.where(incl, jnp.exp(jnp.where(incl, gcc - gcr, 0.0)), 0.0)
        kn_b = kn_c.astype(BF16)
        kk = lax.dot_general(kb_c.astype(BF16), kn_b, nt, preferred_element_type=F32)
        decs.append(dec)
        kn_bs.append(kn_b)
        zs.append(jnp.concatenate([-jnp.where(strict, kk * dec, 0.0), eye], axis=1))
    n = 1
    while n < c:
        zs = [_dot(z[:, :c], z) + jnp.where(right, z, 0.0) for z in zs]
        n *= 2
    uw_bs = [_dot(z[:, c:], ch[3]).astype(BF16) for z, ch in zip(zs, chains)]
    wns = [lax.dot_general(ch[4].astype(BF16), uw_b, tn, preferred_element_type=F32)
           for uw_b, ch in zip(uw_bs, chains)]
    qks = [None if ch[2] is None else
           lax.dot_general(ch[2].astype(BF16), kn_b, nt, preferred_element_type=F32) * dec
           for ch, kn_b, dec in zip(chains, kn_bs, decs)]
    qws = [None if qk is None else jnp.dot(qk.astype(BF16), uw_b, preferred_element_type=F32)
           for qk, uw_b in zip(qks, uw_bs)]
    out = []
    for ch, wn, qw in zip(chains, wns, qws):
        nc, w2 = wn[:, :hd], wn[:, hd:]
        if qw is None:
            out.append((w2, nc, None, None))
        else:
            out.append((w2, nc, ch[5] - qw[:, hd:], qw[:, :hd]))
    return out


def _delta_kernel(q_ref, k_ref, v_ref, z_ref, kc_ref, vc_ref, cwq_ref, cwk_ref, cwv_ref,
                  g_ref, gc_ref, grf_ref, grb_ref, grcf_ref, grcb_ref, ong_ref,
                  out_ref,
                  qn_s, kn_s, kb_s, rhs_s, kd_s, qd_s, col_s,
                  knc_s, kbc_s, rhsc_s, kdc_s, colc_s,
                  w2_s, nc_s, qp_s, o0_s, w2c_s, ncc_s, o_s, *, heads):
    h = pl.program_id(1)
    t = q_ref.shape[1]
    tc = kc_ref.shape[1]
    hd = HEAD_DIM
    n_lat = t // CHUNK
    n_ctx = tc // CHUNK

    qn = _l2norm(_conv_silu(q_ref[0], cwq_ref[...])) * (hd ** -0.5)
    kn = _l2norm(_conv_silu(k_ref[0], cwk_ref[...]))
    vv = _conv_silu(v_ref[0], cwv_ref[...])
    knc = _l2norm(_conv_silu(kc_ref[0], cwk_ref[...]))
    vvc = _conv_silu(vc_ref[0], cwv_ref[...])
    qn_s[...] = qn
    kn_s[...] = kn
    knc_s[...] = knc
    g_lat = g_ref[0]
    g_ctx = gc_ref[0]

    for d in range(2):
        for (gt_, kn_, vv_, kb_r, rhs_r, kd_r, col_r, qn_, qd_r) in (
                (g_lat, kn, vv, kb_s, rhs_s, kd_s, col_s, qn, qd_s),
                (g_ctx, knc, vvc, kbc_s, rhsc_s, kdc_s, colc_s, None, None)):
            beta = _lane_col(gt_, d * heads + h)
            gcum = _lane_col(gt_, (2 + d) * heads + h)
            gtot = _lane_col(gt_, (4 + d) * heads + h)
            e = jnp.exp(gcum)
            kb = kn_ * beta
            kb_r[d] = kb
            rhs_r[d, :, 0:hd] = vv_ * beta
            rhs_r[d, :, hd:2 * hd] = kb * e
            kd_r[d] = kn_ * jnp.exp(gtot - gcum)
            lane = lax.broadcasted_iota(jnp.int32, (gcum.shape[0], LANES), 1)
            col_r[d] = jnp.where(lane == 0, gcum, jnp.where(lane == 1, jnp.exp(gtot), 0.0))
            if qn_ is not None:
                qd_r[d] = qn_ * e

    gr_lat = (grf_ref, grb_ref)
    gr_ctx = (grcf_ref, grcb_ref)

    un_ctx = min(PRE_UNROLL, n_ctx)
    un_lat = min(PRE_UNROLL, n_lat)

    def pre_ctx(i, carry):
        ids = [(i * un_ctx + u, d) for u in range(un_ctx) for d in range(2)]
        chains = []
        for c, d in ids:
            rows = pl.ds(pl.multiple_of(c * CHUNK, CHUNK), CHUNK)
            chains.append((knc_s[rows, :], kbc_s[d, rows, :], None, rhsc_s[d, rows, :], kdc_s[d, rows, :],
                           None, colc_s[d, rows, 0:1], gr_ctx[d][0, 0, pl.ds(c, 1), :], d == 1))
        for (c, d), (w2, nc, _, _) in zip(ids, _chunk_terms(chains)):
            m0 = pl.multiple_of(c * hd, hd)
            w2c_s[d, pl.ds(m0, hd), :] = w2
            ncc_s[d, pl.ds(m0, hd), :] = nc
        return carry

    lax.fori_loop(0, n_ctx // un_ctx, pre_ctx, 0)

    def pre_lat(i, carry):
        ids = [(i * un_lat + u, d) for u in range(un_lat) for d in range(2)]
        chains = []
        for c, d in ids:
            rows = pl.ds(pl.multiple_of(c * CHUNK, CHUNK), CHUNK)
            chains.append((kn_s[rows, :], kb_s[d, rows, :], qn_s[rows, :], rhs_s[d, rows, :], kd_s[d, rows, :],
                           qd_s[d, rows, :], col_s[d, rows, 0:1], gr_lat[d][0, 0, pl.ds(c, 1), :], d == 1))
        for (c, d), (w2, nc, qp, o0) in zip(ids, _chunk_terms(chains)):
            rows = pl.ds(pl.multiple_of(c * CHUNK, CHUNK), CHUNK)
            m0 = pl.multiple_of(c * hd, hd)
            w2_s[d, pl.ds(m0, hd), :] = w2
            nc_s[d, pl.ds(m0, hd), :] = nc
            qp_s[d, rows, :] = qp
            o0_s[d, rows, :] = o0
        return carry

    lax.fori_loop(0, n_lat // un_lat, pre_lat, 0)

    def scan_ctx(i, states):
        new = []
        for d in range(2):
            s = states[d]
            c = i if d == 0 else n_ctx - 1 - i
            m0 = pl.multiple_of(c * hd, hd)
            gt = colc_s[d, pl.ds(pl.multiple_of(c * CHUNK, CHUNK), 1), 1:2]
            s = gt * s + ncc_s[d, pl.ds(m0, hd), :] - _dot(w2c_s[d, pl.ds(m0, hd), :], s)
            new.append(s)
        return tuple(new)

    zero = jnp.zeros((hd, hd), F32)
    states = lax.fori_loop(0, n_ctx, scan_ctx, (zero, zero))

    def scan_lat(i, states):
        new = []
        for d in range(2):
            s = states[d]
            c = i if d == 0 else n_lat - 1 - i
            r0 = pl.multiple_of(c * CHUNK, CHUNK)
            rows = pl.ds(r0, CHUNK)
            m0 = pl.multiple_of(c * hd, hd)
            s_b = s.astype(BF16)
            o_s[d, rows, :] = jnp.dot(qp_s[d, rows, :].astype(BF16), s_b,
                                      preferred_element_type=F32) + o0_s[d, rows, :]
            gt = col_s[d, pl.ds(r0, 1), 1:2]
            s = gt * s + nc_s[d, pl.ds(m0, hd), :] - jnp.dot(
                w2_s[d, pl.ds(m0, hd), :].astype(BF16), s_b, preferred_element_type=F32)
            new.append(s)
        return tuple(new)

    lax.fori_loop(0, n_lat, scan_lat, states)

    o = o_s[0] + o_s[1]
    o = o * lax.rsqrt(jnp.mean(o * o, axis=-1, keepdims=True) + EPS) * ong_ref[...]
    out_ref[0] = (o * _silu(z_ref[0])).astype(out_ref.dtype)


def _delta_call(proj, proj_c, conv_w, g, gc, gr, grc, ong, heads, q_blk0, z_blk0):
    bsz, t, _ = proj.shape
    tc = proj_c.shape[1]
    hd = HEAD_DIM
    n_lat, n_ctx = t // CHUNK, tc // CHUNK

    def col(off):
        return lambda b, h: (b, 0, off + h)

    def cw(off):
        return lambda b, h: (0, off + h)

    def grow(off):
        return lambda b, h: (b, off + h, 0, 0)

    in_specs = [
        pl.BlockSpec((1, t, hd), col(q_blk0)),
        pl.BlockSpec((1, t, hd), col(q_blk0 + heads)),
        pl.BlockSpec((1, t, hd), col(q_blk0 + 2 * heads)),
        pl.BlockSpec((1, t, hd), col(z_blk0)),
        pl.BlockSpec((1, tc, hd), col(q_blk0 + heads)),
        pl.BlockSpec((1, tc, hd), col(q_blk0 + 2 * heads)),
        pl.BlockSpec((CONV_WIDTH, hd), cw(0)),
        pl.BlockSpec((CONV_WIDTH, hd), cw(heads)),
        pl.BlockSpec((CONV_WIDTH, hd), cw(2 * heads)),
        pl.BlockSpec((1, t, LANES), lambda b, h: (b, 0, 0)),
        pl.BlockSpec((1, tc, LANES), lambda b, h: (b, 0, 0)),
        pl.BlockSpec((1, 1, n_lat, CHUNK), grow(2 * heads)),
        pl.BlockSpec((1, 1, n_lat, CHUNK), grow(3 * heads)),
        pl.BlockSpec((1, 1, n_ctx, CHUNK), grow(2 * heads)),
        pl.BlockSpec((1, 1, n_ctx, CHUNK), grow(3 * heads)),
        pl.BlockSpec((1, hd), lambda b, h: (0, 0)),
    ]
    scratch = [
        pltpu.VMEM((t, hd), F32), pltpu.VMEM((t, hd), F32),
        pltpu.VMEM((2, t, hd), F32), pltpu.VMEM((2, t, 2 * hd), F32),
        pltpu.VMEM((2, t, hd), F32), pltpu.VMEM((2, t, hd), F32),
        pltpu.VMEM((2, t, LANES), F32),
        pltpu.VMEM((tc, hd), F32), pltpu.VMEM((2, tc, hd), F32),
        pltpu.VMEM((2, tc, 2 * hd), F32), pltpu.VMEM((2, tc, hd), F32),
        pltpu.VMEM((2, tc, LANES), F32),
        pltpu.VMEM((2, n_lat * hd, hd), F32), pltpu.VMEM((2, n_lat * hd, hd), F32),
        pltpu.VMEM((2, t, hd), F32), pltpu.VMEM((2, t, hd), F32),
        pltpu.VMEM((2, n_ctx * hd, hd), F32), pltpu.VMEM((2, n_ctx * hd, hd), F32),
        pltpu.VMEM((2, t, hd), F32),
    ]
    return pl.pallas_call(
        functools.partial(_delta_kernel, heads=heads),
        grid=(bsz, heads),
        in_specs=in_specs,
        out_specs=pl.BlockSpec((1, t, hd), lambda b, h: (b, 0, h)),
        out_shape=jax.ShapeDtypeStruct((bsz, t, heads * hd), BF16),
        scratch_shapes=scratch,
        compiler_params=_cparams(("parallel", "parallel")),
        name="delta",
    )(proj, proj, proj, proj, proj_c, proj_c, conv_w, conv_w, conv_w, g, gc, gr, gr, grc, grc, ong)


def _shift_rows(x, d, idx, size, stride):
    n = x.shape[0]
    xs = pltpu.roll(x, (-d * stride) % n, 0)
    return jnp.where((idx + d >= 0) & (idx + d < size), xs, 0.0)


def _box_sum_1d(x, win, idx, size, stride):
    m = win // 2
    lead = x
    trail = x
    k = 1
    while k < m:
        lead = lead + _shift_rows(lead, k, idx, size, stride)
        trail = trail + _shift_rows(trail, -k, idx, size, stride)
        k *= 2
    return lead + _shift_rows(trail, -1, idx, size, stride)


def _pool_kernel(u_ref, pw_ref, ps_ref, o_ref, *, rows, cols):
    t = u_ref.shape[1]
    gc = pw_ref.shape[1]
    tok = lax.broadcasted_iota(jnp.int32, (t, gc), 0)
    ci = tok % cols
    ri = tok // cols
    for gi, win in enumerate(POOL_WINDOWS):
        lo = win // 2
        hi = win - lo
        u = u_ref[0, :, gi * gc:(gi + 1) * gc]
        s = _box_sum_1d(u, win, ci, cols, 1)
        s = _box_sum_1d(s, win, ri, rows, cols)
        cnt_c = jnp.minimum(ci + hi, cols) - jnp.maximum(ci - lo, 0)
        cnt_r = jnp.minimum(ri + hi, rows) - jnp.maximum(ri - lo, 0)
        mean = s / (cnt_c * cnt_r).astype(F32)
        y = _dot(mean - u, pw_ref[gi]) * ps_ref[:, gi * gc:(gi + 1) * gc]
        o_ref[0, :, gi * gc:(gi + 1) * gc] = y.astype(o_ref.dtype)


def _pool_call(proj, pool_w, pool_scale, rows, cols):
    bsz, t, _ = proj.shape
    ng, gc, _ = pool_w.shape
    pwid = ng * gc
    return pl.pallas_call(
        functools.partial(_pool_kernel, rows=rows, cols=cols),
        grid=(bsz,),
        in_specs=[pl.BlockSpec((1, t, pwid), lambda b: (b, 0, 0)),
                  pl.BlockSpec((ng, gc, gc), lambda b: (0, 0, 0)),
                  pl.BlockSpec((1, pwid), lambda b: (0, 0))],
        out_specs=pl.BlockSpec((1, t, pwid), lambda b: (b, 0, 0)),
        out_shape=jax.ShapeDtypeStruct((bsz, t, pwid), BF16),
        compiler_params=_cparams(("parallel",)),
        name="pool",
    )(proj, pool_w, pool_scale)


def _outproj_kernel(pool_ref, dn_ref, wa_ref, wb_ref, x_ref, mods_ref, g2_ref, wr_ref, br_ref,
                    x1_ref, h2_ref, lg_ref):
    b = pl.program_id(0)
    mix = (jnp.dot(pool_ref[0], wa_ref[...], preferred_element_type=F32)
           + jnp.dot(dn_ref[0], wb_ref[...], preferred_element_type=F32))
    gt1 = mods_ref[2, pl.ds(b, 1), :]
    sh2 = mods_ref[3, pl.ds(b, 1), :]
    sc2 = mods_ref[4, pl.ds(b, 1), :]
    x1 = x_ref[0] + gt1 * mix
    x1_ref[0] = x1
    y = x1 * lax.rsqrt(jnp.mean(x1 * x1, axis=-1, keepdims=True) + EPS) * g2_ref[...]
    h2 = y * (1 + sc2) + sh2
    h2_ref[0] = h2
    lg_ref[0] = _dot_hi(h2, wr_ref[...]) + br_ref[...]


def _outproj_call(pool, dn, w_out_bf, x, mods, g2, wr, br):
    bsz, t, d = x.shape
    half = pool.shape[-1]
    tm = min(t, 256)
    return pl.pallas_call(
        _outproj_kernel,
        grid=(bsz, t // tm),
        in_specs=[pl.BlockSpec((1, tm, half), lambda b, i: (b, i, 0)),
                  pl.BlockSpec((1, tm, half), lambda b, i: (b, i, 0)),
                  pl.BlockSpec((half, d), lambda b, i: (0, 0)),
                  pl.BlockSpec((half, d), lambda b, i: (1, 0)),
                  pl.BlockSpec((1, tm, d), lambda b, i: (b, i, 0)),
                  pl.BlockSpec((6, 8, d), lambda b, i: (0, 0, 0)),
                  pl.BlockSpec((1, d), lambda b, i: (0, 0)),
                  pl.BlockSpec((d, LANES), lambda b, i: (0, 0)),
                  pl.BlockSpec((1, LANES), lambda b, i: (0, 0))],
        out_specs=[pl.BlockSpec((1, tm, d), lambda b, i: (b, i, 0)),
                   pl.BlockSpec((1, tm, d), lambda b, i: (b, i, 0)),
                   pl.BlockSpec((1, tm, LANES), lambda b, i: (b, i, 0))],
        out_shape=[jax.ShapeDtypeStruct((bsz, t, d), F32),
                   jax.ShapeDtypeStruct((bsz, t, d), F32),
                   jax.ShapeDtypeStruct((bsz, t, LANES), F32)],
        compiler_params=_cparams(("parallel", "parallel")),
        name="outproj",
    )(pool, dn, w_out_bf, w_out_bf, x, mods, g2, wr, br)


def _router_kernel(lg_ref, dest_ref, gate_ref, be_ref, cnt_s, run_s, off_s):
    p = pl.program_id(0)
    i = pl.program_id(1)
    lg = lg_ref[...]
    tm = lg.shape[0]
    lane = lax.broadcasted_iota(jnp.int32, lg.shape, 1)
    lane_f = lane.astype(F32)
    neg = -jnp.inf
    big = float(LANES)

    grp = jnp.where(lane < N_GROUPS, lg, neg)
    gmax = jnp.max(grp, axis=-1, keepdims=True)
    gidx = jnp.min(jnp.where(grp == gmax, lane_f, big), axis=-1, keepdims=True)
    p_grp = 1.0 / jnp.sum(jnp.where(lane < N_GROUPS, jnp.exp(lg - gmax), 0.0), axis=-1, keepdims=True)
    lo = EXP_LANE0 + EXPERTS_PER_GROUP * gidx
    ev = jnp.where((lane_f >= lo) & (lane_f < lo + EXPERTS_PER_GROUP), lg, neg)
    t1 = jnp.max(ev, axis=-1, keepdims=True)
    i1 = jnp.min(jnp.where(ev == t1, lane_f, big), axis=-1, keepdims=True)
    ev2 = jnp.where(lane_f == i1, neg, ev)
    t2 = jnp.max(ev2, axis=-1, keepdims=True)
    i2 = jnp.min(jnp.where(ev2 == t2, lane_f, big), axis=-1, keepdims=True)
    oh1 = lane_f == i1
    oh2 = lane_f == i2
    cnt = oh1.astype(F32) + oh2.astype(F32)
    colsum = jnp.sum(cnt, axis=0, keepdims=True)

    @pl.when(p == 0)
    def _():
        @pl.when(i == 0)
        def _():
            cnt_s[...] = jnp.zeros_like(cnt_s)

        cnt_s[...] += colsum

    @pl.when(p == 1)
    def _():
        @pl.when(i == 0)
        def _():
            nblk = jnp.floor((cnt_s[...] + (MOE_BLOCK - 1)) * (1.0 / MOE_BLOCK))
            r = lax.broadcasted_iota(jnp.int32, (LANES, LANES), 0)
            c = lax.broadcasted_iota(jnp.int32, (LANES, LANES), 1)
            tri = (r < c).astype(BF16)
            nb8 = jnp.broadcast_to(nblk, (8, LANES))
            start_blk = jnp.dot(nb8.astype(BF16), tri, preferred_element_type=F32)[0:1, :]
            off_s[...] = start_blk * MOE_BLOCK
            run_s[...] = jnp.zeros_like(run_s)
            end_blk = start_blk + nblk
            nb = be_ref.shape[0]
            blk = lax.broadcasted_iota(jnp.int32, (nb, LANES), 0).astype(F32)
            ln = lax.broadcasted_iota(jnp.int32, (nb, LANES), 1)
            is_exp = (ln >= EXP_LANE0) & (ln < EXP_LANE0 + N_EXPERTS)
            done = jnp.sum(jnp.where(is_exp & (end_blk <= blk), 1.0, 0.0), axis=-1, keepdims=True)
            bexp = jnp.minimum(done, N_EXPERTS - 1.0)
            used = jnp.max(jnp.where(is_exp, end_blk, 0.0), axis=-1, keepdims=True)
            be_ref[...] = jnp.where(ln == 0, bexp, jnp.where(ln == 1, used, 0.0)).astype(jnp.int32)

        rr = lax.broadcasted_iota(jnp.int32, (tm, tm), 0)
        cc = lax.broadcasted_iota(jnp.int32, (tm, tm), 1)
        before = (cc < rr).astype(BF16)
        prefix = jnp.dot(before, cnt.astype(BF16), preferred_element_type=F32)
        base = off_s[...] + run_s[...] + prefix
        d1 = jnp.sum(jnp.where(oh1, base, 0.0), axis=-1, keepdims=True)
        d2 = jnp.sum(jnp.where(oh2, base, 0.0), axis=-1, keepdims=True)
        run_s[...] += colsum
        dd = jnp.exp(t2 - t1)
        g1 = p_grp / (1.0 + dd)
        g2 = p_grp * dd / (1.0 + dd)
        dest_ref[...] = jnp.where(lane == 0, d1, jnp.where(lane == 1, d2, 0.0)).astype(jnp.int32)
        gate_ref[...] = jnp.where(lane == 0, g1, jnp.where(lane == 1, g2, 0.0))


def _router_call(logits, n_blocks):
    n = logits.shape[0]
    tm = min(n, 512)
    nb_pad = ((n_blocks + 7) // 8) * 8
    return pl.pallas_call(
        _router_kernel,
        grid=(2, n // tm),
        in_specs=[pl.BlockSpec((tm, LANES), lambda p, i: (i, 0))],
        out_specs=[pl.BlockSpec((tm, LANES), lambda p, i: (i * p, 0)),
                   pl.BlockSpec((tm, LANES), lambda p, i: (i * p, 0)),
                   pl.BlockSpec((nb_pad, LANES), lambda p, i: (0, 0))],
        out_shape=[jax.ShapeDtypeStruct((n, LANES), jnp.int32),
                   jax.ShapeDtypeStruct((n, LANES), F32),
                   jax.ShapeDtypeStruct((nb_pad, LANES), jnp.int32)],
        scratch_shapes=[pltpu.VMEM((1, LANES), F32), pltpu.VMEM((1, LANES), F32),
                        pltpu.VMEM((1, LANES), F32)],
        compiler_params=_cparams(("arbitrary", "arbitrary")),
        name="router",
    )(logits)


def _slots_kernel(dest_ref, inv_ref):
    def clear(s, carry):
        inv_ref[s] = 0
        return carry

    lax.fori_loop(0, inv_ref.shape[0], clear, 0, unroll=8)

    def put(a, carry):
        inv_ref[dest_ref[a]] = a // 2
        return carry

    lax.fori_loop(0, dest_ref.shape[0], put, 0, unroll=8)


def _slots_call(dest_flat, n_rows):
    return pl.pallas_call(
        _slots_kernel,
        in_specs=[pl.BlockSpec(memory_space=pltpu.SMEM)],
        out_specs=pl.BlockSpec(memory_space=pltpu.SMEM),
        out_shape=jax.ShapeDtypeStruct((n_rows,), jnp.int32),
        name="slots",
    )(dest_flat)


def _gmm_kernel(be_ref, used_ref, inv_ref, h_ref, w1_ref, w3_ref, w2_ref, o_ref, xbuf, sem):
    j = pl.program_id(0)
    used = used_ref[0]

    def row_copy(blk, r, slot):
        tok = inv_ref[blk * MOE_BLOCK + r]
        return pltpu.make_async_copy(h_ref.at[pl.ds(tok, 1)], xbuf.at[slot, pl.ds(r, 1)], sem.at[slot])

    def gather(blk, slot):
        def body(r, carry):
            row_copy(blk, r, slot).start()
            return carry

        lax.fori_loop(0, MOE_BLOCK, body, 0, unroll=8)

    @pl.when(j == 0)
    def _():
        gather(0, 0)

    @pl.when(j + 1 < used)
    def _():
        gather(j + 1, (j + 1) % 2)

    @pl.when(j < used)
    def _():
        slot = j % 2

        def wait(r, carry):
            row_copy(j, r, slot).wait()
            return carry

        lax.fori_loop(0, MOE_BLOCK, wait, 0, unroll=8)
        xb = xbuf[slot].astype(BF16)
        h1 = jnp.dot(xb, w1_ref[0], preferred_element_type=F32)
        h3 = jnp.dot(xb, w3_ref[0], preferred_element_type=F32)
        a = (_silu(h1) * h3).astype(BF16)
        o_ref[...] = jnp.dot(a, w2_ref[0], preferred_element_type=F32)

    @pl.when(j >= used)
    def _():
        o_ref[...] = jnp.zeros_like(o_ref)


def _gmm_call(block_expert, used, inv, h2, w1b, w3b, w2b):
    n_rows = inv.shape[0]
    d = h2.shape[1]
    de = w1b.shape[-1]
    nb = n_rows // MOE_BLOCK

    def wmap(j, be, u, inv):
        return (be[jnp.minimum(j, u[0] - 1)], 0, 0)

    grid_spec = pltpu.PrefetchScalarGridSpec(
        num_scalar_prefetch=3,
        grid=(nb,),
        in_specs=[pl.BlockSpec(memory_space=pl.ANY),
                  pl.BlockSpec((1, d, de), wmap),
                  pl.BlockSpec((1, d, de), wmap),
                  pl.BlockSpec((1, de, d), wmap)],
        out_specs=pl.BlockSpec((MOE_BLOCK, d), lambda j, be, u, inv: (j, 0)),
        scratch_shapes=[pltpu.VMEM((2, MOE_BLOCK, d), F32), pltpu.SemaphoreType.DMA((2,))],
    )
    return pl.pallas_call(
        _gmm_kernel,
        grid_spec=grid_spec,
        out_shape=jax.ShapeDtypeStruct((n_rows, d), F32),
        compiler_params=_cparams(("arbitrary",)),
        name="gmm",
    )(block_expert, used, inv, h2, w1b, w3b, w2b)


def _combine_kernel(dest_ref, y_ref, x1_ref, gate_ref, mods_ref, fg_ref, o_ref, ybuf, sem, *, seq):
    i = pl.program_id(0)
    tm = x1_ref.shape[0]
    tok0 = i * tm

    def copy(r, k):
        return pltpu.make_async_copy(y_ref.at[pl.ds(dest_ref[2 * (tok0 + r) + k], 1)],
                                     ybuf.at[k, pl.ds(r, 1)], sem)

    def start(r, carry):
        copy(r, 0).start()
        copy(r, 1).start()
        return carry

    lax.fori_loop(0, tm, start, 0)

    def wait(r, carry):
        copy(r, 0).wait()
        copy(r, 1).wait()
        return carry

    lax.fori_loop(0, tm, wait, 0)

    b = tok0 // seq
    gt2 = mods_ref[5, pl.ds(b, 1), :]
    gate = gate_ref[...]
    moe = gate[:, 0:1] * ybuf[0] + gate[:, 1:2] * ybuf[1]
    x = x1_ref[...] + gt2 * moe
    o_ref[...] = x * lax.rsqrt(jnp.mean(x * x, axis=-1, keepdims=True) + EPS) * fg_ref[...]


def _combine_call(dest_flat, y_pad, x1, gate, mods, final_g, seq):
    n, d = x1.shape
    tm = min(seq, 256)
    grid_spec = pltpu.PrefetchScalarGridSpec(
        num_scalar_prefetch=1,
        grid=(n // tm,),
        in_specs=[pl.BlockSpec(memory_space=pl.ANY),
                  pl.BlockSpec((tm, d), lambda i, dest: (i, 0)),
                  pl.BlockSpec((tm, LANES), lambda i, dest: (i, 0)),
                  pl.BlockSpec((6, 8, d), lambda i, dest: (0, 0, 0)),
                  pl.BlockSpec((1, d), lambda i, dest: (0, 0))],
        out_specs=pl.BlockSpec((tm, d), lambda i, dest: (i, 0)),
        scratch_shapes=[pltpu.VMEM((2, tm, d), F32), pltpu.SemaphoreType.DMA(())],
    )
    return pl.pallas_call(
        functools.partial(_combine_kernel, seq=seq),
        grid_spec=grid_spec,
        out_shape=jax.ShapeDtypeStruct((n, d), F32),
        compiler_params=_cparams(("arbitrary",)),
        name="combine",
    )(dest_flat, y_pad, x1, gate, mods, final_g)


def _pad_lanes(a, lane0):
    return jnp.zeros((LANES,), F32).at[lane0:lane0 + a.shape[0]].set(a.astype(F32))


def kernel(x, c, ctx, c_ctx, w_mod, b_mod, norm1_g, w_in, pool_w, pool_scale, conv_w,
           a_log_f, dt_bias_f, a_log_b, dt_bias_b, out_norm_g, w_out, norm2_g,
           w_grp, b_grp, w_rt, b_rt, w1, w3, w2, final_g):
    bsz, t, d = x.shape
    depth = w_mod.shape[0]
    assert depth == 1, "single-layer problem: the context stream is read but never updated"
    heads = a_log_f.shape[1]
    pool_width = pool_w.shape[1] * pool_w.shape[2]
    dn_width = heads * HEAD_DIM
    q0 = pool_width
    z0 = q0 + 3 * dn_width
    ab0 = z0 + dn_width
    n_tok = bsz * t
    l = 0

    c8 = jnp.zeros((8, d), F32).at[:bsz].set(c).at[bsz].set(c_ctx)
    mod = _mod_call(c8, w_mod[l], b_mod[l])
    mods = mod.reshape(8, 6, d).transpose(1, 0, 2)

    w_in_bf = w_in[l].astype(BF16)
    wab_bf = jnp.zeros((d, LANES), BF16).at[:, :4 * heads].set(w_in_bf[:, ab0:])
    g1 = norm1_g[l].reshape(1, d)
    proj, ab = _inproj_call(x, mods, g1, w_in_bf, wab_bf, ab0, None)
    proj_c, ab_c = _inproj_call(ctx, mods, g1, w_in_bf, wab_bf, ab0, bsz)

    prm = jnp.zeros((8, LANES), F32)
    prm = prm.at[0].set(_pad_lanes(jnp.concatenate([a_log_f[l], a_log_b[l]]), 2 * heads))
    prm = prm.at[1].set(_pad_lanes(jnp.concatenate([dt_bias_f[l], dt_bias_b[l]]), 2 * heads))
    g, g_t = _gates_call(ab, prm, heads)
    gc, gc_t = _gates_call(ab_c, prm, heads)
    gr = g_t.reshape(bsz, LANES, t // CHUNK, CHUNK)
    grc = gc_t.reshape(bsz, LANES, ctx.shape[1] // CHUNK, CHUNK)

    dn = _delta_call(proj, proj_c, conv_w[l], g, gc, gr, grc, out_norm_g[l].reshape(1, HEAD_DIM),
                     heads, q0 // HEAD_DIM, z0 // HEAD_DIM)
    pool = _pool_call(proj, pool_w[l], pool_scale[l].reshape(1, pool_width), t // GRID_W, GRID_W)

    wr = jnp.zeros((d, LANES), F32).at[:, :N_GROUPS].set(w_grp[l]).at[:, EXP_LANE0:EXP_LANE0 + N_EXPERTS].set(w_rt[l])
    br = jnp.zeros((LANES,), F32).at[:N_GROUPS].set(b_grp[l]).at[EXP_LANE0:EXP_LANE0 + N_EXPERTS].set(b_rt[l])
    x1, h2, logits = _outproj_call(pool, dn, w_out[l].astype(BF16), x, mods, norm2_g[l].reshape(1, d),
                                   wr, br.reshape(1, LANES))

    n_blocks = (n_tok * 2 + N_EXPERTS * (MOE_BLOCK - 1) + MOE_BLOCK - 1) // MOE_BLOCK
    dest, gate, be = _router_call(logits.reshape(n_tok, LANES), n_blocks)
    dest_flat = dest[:, :2].reshape(-1)
    block_expert = be[:n_blocks, 0]
    used = be[0:1, 1]
    inv = _slots_call(dest_flat, n_blocks * MOE_BLOCK)
    y_pad = _gmm_call(block_expert, used, inv, h2.reshape(n_tok, d),
                      w1[l].astype(BF16), w3[l].astype(BF16), w2[l].astype(BF16))
    out = _combine_call(dest_flat, y_pad, x1.reshape(n_tok, d), gate, mods, final_g.reshape(1, d), t)
    return out.reshape(bsz, t, d)
```

```python
import functools

import jax
import jax.numpy as jnp
from jax import lax
from jax.experimental import pallas as pl
from jax.experimental.pallas import tpu as pltpu

F32 = jnp.float32
BF16 = jnp.bfloat16

GRID_W = 64
POOL_WINDOWS = (2, 4, 8, 16)
HEAD_DIM = 128
CONV_WIDTH = 5
CHUNK = 64
N_GROUPS = 4
EXPERTS_PER_GROUP = 8
N_EXPERTS = N_GROUPS * EXPERTS_PER_GROUP
MOE_BLOCK = 128
EPS = 1e-6
LANES = 128
PRE_UNROLL = 4
EXP_LANE0 = N_GROUPS

VMEM_LIMIT = 56 * 1024 * 1024


def _cparams(sem):
    return pltpu.CompilerParams(dimension_semantics=sem, vmem_limit_bytes=VMEM_LIMIT)


def _dot(a, b):
    return jnp.dot(a.astype(BF16), b.astype(BF16), preferred_element_type=F32)


def _dot_split(a, b):
    a_hi = a.astype(BF16)
    b_hi = b.astype(BF16)
    a_lo = (a - a_hi.astype(F32)).astype(BF16)
    b_lo = (b - b_hi.astype(F32)).astype(BF16)
    return (jnp.dot(a_hi, b_hi, preferred_element_type=F32)
            + jnp.dot(a_lo, b_hi, preferred_element_type=F32)
            + jnp.dot(a_hi, b_lo, preferred_element_type=F32))


def _silu(x):
    return x * jax.nn.sigmoid(x)


def _mod_kernel(c_ref, w_ref, b_ref, o_ref):
    o_ref[...] = _dot(_silu(c_ref[...]), w_ref[...]) + b_ref[...]


def _mod_call(c8, w_mod, b_mod):
    d, n = w_mod.shape
    tn = 512
    return pl.pallas_call(
        _mod_kernel,
        grid=(n // tn,),
        in_specs=[pl.BlockSpec((8, d), lambda j: (0, 0)),
                  pl.BlockSpec((d, tn), lambda j: (0, j)),
                  pl.BlockSpec((1, tn), lambda j: (0, j))],
        out_specs=pl.BlockSpec((8, tn), lambda j: (0, j)),
        out_shape=jax.ShapeDtypeStruct((8, n), F32),
        compiler_params=_cparams(("parallel",)),
        name="mod",
    )(c8, w_mod, b_mod.reshape(1, n))


def _inproj_kernel(x_ref, mods_ref, g_ref, w_ref, wab_ref, o_ref, ab_ref, hn_ref, *, mod_row):
    b = pl.program_id(0)
    j = pl.program_id(2)

    @pl.when(j == 0)
    def _():
        x = x_ref[0]
        y = x * lax.rsqrt(jnp.mean(x * x, axis=-1, keepdims=True) + EPS) * g_ref[...]
        row = b if mod_row is None else mod_row
        sh = mods_ref[0, pl.ds(row, 1), :]
        sc = mods_ref[1, pl.ds(row, 1), :]
        hb = (y * (1 + sc) + sh).astype(BF16)
        hn_ref[...] = hb
        ab_ref[0] = jnp.dot(hb, wab_ref[...], preferred_element_type=F32)

    o_ref[0] = jnp.dot(hn_ref[...], w_ref[...], preferred_element_type=F32)


def _inproj_call(x, mods, g, w_bf, wab_bf, n_main, mod_row):
    bsz, t, d = x.shape
    tm = min(t, 1024)
    tn = 1024
    kern = functools.partial(_inproj_kernel, mod_row=mod_row)
    return pl.pallas_call(
        kern,
        grid=(bsz, t // tm, n_main // tn),
        in_specs=[pl.BlockSpec((1, tm, d), lambda b, i, j: (b, i, 0)),
                  pl.BlockSpec((2, 8, d), lambda b, i, j: (0, 0, 0)),
                  pl.BlockSpec((1, d), lambda b, i, j: (0, 0)),
                  pl.BlockSpec((d, tn), lambda b, i, j: (0, j)),
                  pl.BlockSpec((d, LANES), lambda b, i, j: (0, 0))],
        out_specs=[pl.BlockSpec((1, tm, tn), lambda b, i, j: (b, i, j)),
                   pl.BlockSpec((1, tm, LANES), lambda b, i, j: (b, i, 0))],
        out_shape=[jax.ShapeDtypeStruct((bsz, t, n_main), F32),
                   jax.ShapeDtypeStruct((bsz, t, LANES), F32)],
        scratch_shapes=[pltpu.VMEM((tm, d), BF16)],
        compiler_params=_cparams(("parallel", "parallel", "arbitrary")),
        name="inproj",
    )(x, mods, g, w_bf, wab_bf)


def _gates_kernel(ab_ref, prm_ref, g_ref, gt_ref, *, heads):
    ab = ab_ref[0]
    t = ab.shape[0]
    h2, h3, h4, h6 = 2 * heads, 3 * heads, 4 * heads, 6 * heads
    beta = jax.nn.sigmoid(ab)
    xx = ab + prm_ref[1:2, :]
    softplus = jnp.maximum(xx, 0.0) + jnp.log1p(jnp.exp(-jnp.abs(xx)))
    g = -jnp.exp(prm_ref[0:1, :]) * softplus
    pos = lax.broadcasted_iota(jnp.int32, ab.shape, 0) & (CHUNK - 1)
    cs = g
    ss = g
    s = 1
    while s < CHUNK:
        cs = cs + jnp.where(pos >= s, pltpu.roll(cs, s, 0), 0.0)
        ss = ss + jnp.where(pos < CHUNK - s, pltpu.roll(ss, t - s, 0), 0.0)
        s *= 2
    tot = pltpu.roll(cs + ss - g, h2, 1)
    lane = lax.broadcasted_iota(jnp.int32, ab.shape, 1)
    out = jnp.where(lane < h2, beta,
                    jnp.where(lane < h3, cs,
                              jnp.where(lane < h4, ss,
                                        jnp.where(lane < h6, tot, 0.0))))
    g_ref[0] = out
    gt_ref[0] = out.T


def _gates_call(ab, prm, heads):
    bsz, t, _ = ab.shape
    return pl.pallas_call(
        functools.partial(_gates_kernel, heads=heads),
        grid=(bsz,),
        in_specs=[pl.BlockSpec((1, t, LANES), lambda b: (b, 0, 0)),
                  pl.BlockSpec((8, LANES), lambda b: (0, 0))],
        out_specs=[pl.BlockSpec((1, t, LANES), lambda b: (b, 0, 0)),
                   pl.BlockSpec((1, LANES, t), lambda b: (b, 0, 0))],
        out_shape=[jax.ShapeDtypeStruct((bsz, t, LANES), F32),
                   jax.ShapeDtypeStruct((bsz, LANES, t), F32)],
        compiler_params=_cparams(("parallel",)),
        name="gates",
    )(ab, prm)


def _conv_silu(x, w):
    n = x.shape[0]
    row = lax.broadcasted_iota(jnp.int32, x.shape, 0)
    acc = x * w[CONV_WIDTH // 2:CONV_WIDTH // 2 + 1, :]
    for j in range(CONV_WIDTH):
        d = j - CONV_WIDTH // 2
        if d == 0:
            continue
        xs = pltpu.roll(x, (-d) % n, 0)
        valid = (row + d >= 0) & (row + d < n)
        acc = acc + jnp.where(valid, xs, 0.0) * w[j:j + 1, :]
    return _silu(acc)


def _l2norm(a):
    return a * lax.rsqrt(jnp.sum(a * a, axis=-1, keepdims=True) + EPS)


def _lane_col(g, lane_idx):
    lane = lax.broadcasted_iota(jnp.int32, g.shape, 1)
    return jnp.sum(jnp.where(lane == lane_idx, g, 0.0), axis=-1, keepdims=True)


def _chunk_terms(chains):
    c, hd = chains[0][0].shape
    ri = lax.broadcasted_iota(jnp.int32, (c, c), 0)
    ci = lax.broadcasted_iota(jnp.int32, (c, c), 1)
    eye = jnp.where(ri == ci, 1.0, 0.0)
    right = lax.broadcasted_iota(jnp.int32, (c, 2 * c), 1) >= c
    nt = (((1,), (1,)), ((), ()))
    tn = (((0,), (0,)), ((), ()))

    decs, kn_bs, zs = [], [], []
    for kn_c, kb_c, _, _, _, _, gcc, gcr, upper in chains:
        incl = (ri <= ci) if upper else (ri >= ci)
        strict = (ri < ci) if upper else (ri > ci)
        dec = jnp.where(incl, jnp.exp(jnp.where(incl, gcc - gcr, 0.0)), 0.0)
        kn_b = kn_c.astype(BF16)
        kk = lax.dot_general(kb_c.astype(BF16), kn_b, nt, preferred_element_type=F32)
        decs.append(dec)
        kn_bs.append(kn_b)
        zs.append(jnp.concatenate([-jnp.where(strict, kk * dec, 0.0), eye], axis=1))
    n = 1
    while n < c:
        zs = [_dot(z[:, :c], z) + jnp.where(right, z, 0.0) for z in zs]
        n *= 2
    uw_bs = [_dot(z[:, c:], ch[3]).astype(BF16) for z, ch in zip(zs, chains)]
    wns = [lax.dot_general(ch[4].astype(BF16), uw_b, tn, preferred_element_type=F32)
           for uw_b, ch in zip(uw_bs, chains)]
    qks = [None if ch[2] is None else
           lax.dot_general(ch[2].astype(BF16), kn_b, nt, preferred_element_type=F32) * dec
           for ch, kn_b, dec in zip(chains, kn_bs, decs)]
    qws = [None if qk is None else jnp.dot(qk.astype(BF16), uw_b, preferred_element_type=F32)
           for qk, uw_b in zip(qks, uw_bs)]
    out = []
    for ch, wn, qw in zip(chains, wns, qws):
        nc, w2 = wn[:, :hd], wn[:, hd:]
        if qw is None:
            out.append((w2, nc, None, None))
        else:
            out.append((w2, nc, ch[5] - qw[:, hd:], qw[:, :hd]))
    return out


def _delta_kernel(q_ref, k_ref, v_ref, z_ref, kc_ref, vc_ref, cwq_ref, cwk_ref, cwv_ref,
                  g_ref, gc_ref, grf_ref, grb_ref, grcf_ref, grcb_ref, ong_ref,
                  out_ref,
                  qn_s, kn_s, kb_s, rhs_s, kd_s, qd_s, col_s,
                  knc_s, kbc_s, rhsc_s, kdc_s, colc_s,
                  w2_s, nc_s, qp_s, o0_s, w2c_s, ncc_s, o_s, *, heads):
    h = pl.program_id(1)
    t = q_ref.shape[1]
    tc = kc_ref.shape[1]
    hd = HEAD_DIM
    n_lat = t // CHUNK
    n_ctx = tc // CHUNK

    qn = _l2norm(_conv_silu(q_ref[0], cwq_ref[...])) * (hd ** -0.5)
    kn = _l2norm(_conv_silu(k_ref[0], cwk_ref[...]))
    vv = _conv_silu(v_ref[0], cwv_ref[...])
    knc = _l2norm(_conv_silu(kc_ref[0], cwk_ref[...]))
    vvc = _conv_silu(vc_ref[0], cwv_ref[...])
    qn_s[...] = qn
    kn_s[...] = kn
    knc_s[...] = knc
    g_lat = g_ref[0]
    g_ctx = gc_ref[0]

    for d in range(2):
        for (gt_, kn_, vv_, kb_r, rhs_r, kd_r, col_r, qn_, qd_r) in (
                (g_lat, kn, vv, kb_s, rhs_s, kd_s, col_s, qn, qd_s),
                (g_ctx, knc, vvc, kbc_s, rhsc_s, kdc_s, colc_s, None, None)):
            beta = _lane_col(gt_, d * heads + h)
            gcum = _lane_col(gt_, (2 + d) * heads + h)
            gtot = _lane_col(gt_, (4 + d) * heads + h)
            e = jnp.exp(gcum)
            kb = kn_ * beta
            kb_r[d] = kb
            rhs_r[d, :, 0:hd] = vv_ * beta
            rhs_r[d, :, hd:2 * hd] = kb * e
            kd_r[d] = kn_ * jnp.exp(gtot - gcum)
            lane = lax.broadcasted_iota(jnp.int32, (gcum.shape[0], LANES), 1)
            col_r[d] = jnp.where(lane == 0, gcum, jnp.where(lane == 1, jnp.exp(gtot), 0.0))
            if qn_ is not None:
                qd_r[d] = qn_ * e

    gr_lat = (grf_ref, grb_ref)
    gr_ctx = (grcf_ref, grcb_ref)

    un_ctx = min(PRE_UNROLL, n_ctx)
    un_lat = min(PRE_UNROLL, n_lat)

    def pre_ctx(i, carry):
        ids = [(i * un_ctx + u, d) for u in range(un_ctx) for d in range(2)]
        chains = []
        for c, d in ids:
            rows = pl.ds(pl.multiple_of(c * CHUNK, CHUNK), CHUNK)
            chains.append((knc_s[rows, :], kbc_s[d, rows, :], None, rhsc_s[d, rows, :], kdc_s[d, rows, :],
                           None, colc_s[d, rows, 0:1], gr_ctx[d][0, 0, pl.ds(c, 1), :], d == 1))
        for (c, d), (w2, nc, _, _) in zip(ids, _chunk_terms(chains)):
            m0 = pl.multiple_of(c * hd, hd)
            w2c_s[d, pl.ds(m0, hd), :] = w2
            ncc_s[d, pl.ds(m0, hd), :] = nc
        return carry

    lax.fori_loop(0, n_ctx // un_ctx, pre_ctx, 0)

    def pre_lat(i, carry):
        ids = [(i * un_lat + u, d) for u in range(un_lat) for d in range(2)]
        chains = []
        for c, d in ids:
            rows = pl.ds(pl.multiple_of(c * CHUNK, CHUNK), CHUNK)
            chains.append((kn_s[rows, :], kb_s[d, rows, :], qn_s[rows, :], rhs_s[d, rows, :], kd_s[d, rows, :],
                           qd_s[d, rows, :], col_s[d, rows, 0:1], gr_lat[d][0, 0, pl.ds(c, 1), :], d == 1))
        for (c, d), (w2, nc, qp, o0) in zip(ids, _chunk_terms(chains)):
            rows = pl.ds(pl.multiple_of(c * CHUNK, CHUNK), CHUNK)
            m0 = pl.multiple_of(c * hd, hd)
            w2_s[d, pl.ds(m0, hd), :] = w2
            nc_s[d, pl.ds(m0, hd), :] = nc
            qp_s[d, rows, :] = qp
            o0_s[d, rows, :] = o0
        return carry

    lax.fori_loop(0, n_lat // un_lat, pre_lat, 0)

    def scan_ctx(i, states):
        new = []
        for d in range(2):
            s = states[d]
            c = i if d == 0 else n_ctx - 1 - i
            m0 = pl.multiple_of(c * hd, hd)
            gt = colc_s[d, pl.ds(pl.multiple_of(c * CHUNK, CHUNK), 1), 1:2]
            s = gt * s + ncc_s[d, pl.ds(m0, hd), :] - _dot(w2c_s[d, pl.ds(m0, hd), :], s)
            new.append(s)
        return tuple(new)

    zero = jnp.zeros((hd, hd), F32)
    states = lax.fori_loop(0, n_ctx, scan_ctx, (zero, zero))

    def scan_lat(i, states):
        new = []
        for d in range(2):
            s = states[d]
            c = i if d == 0 else n_lat - 1 - i
            r0 = pl.multiple_of(c * CHUNK, CHUNK)
            rows = pl.ds(r0, CHUNK)
            m0 = pl.multiple_of(c * hd, hd)
            s_b = s.astype(BF16)
            o_s[d, rows, :] = jnp.dot(qp_s[d, rows, :].astype(BF16), s_b,
                                      preferred_element_type=F32) + o0_s[d, rows, :]
            gt = col_s[d, pl.ds(r0, 1), 1:2]
            s = gt * s + nc_s[d, pl.ds(m0, hd), :] - jnp.dot(
                w2_s[d, pl.ds(m0, hd), :].astype(BF16), s_b, preferred_element_type=F32)
            new.append(s)
        return tuple(new)

    lax.fori_loop(0, n_lat, scan_lat, states)

    o = o_s[0] + o_s[1]
    o = o * lax.rsqrt(jnp.mean(o * o, axis=-1, keepdims=True) + EPS) * ong_ref[...]
    out_ref[0] = (o * _silu(z_ref[0])).astype(out_ref.dtype)


def _delta_call(proj, proj_c, conv_w, g, gc, gr, grc, ong, heads, q_blk0, z_blk0):
    bsz, t, _ = proj.shape
    tc = proj_c.shape[1]
    hd = HEAD_DIM
    n_lat, n_ctx = t // CHUNK, tc // CHUNK

    def col(off):
        return lambda b, h: (b, 0, off + h)

    def cw(off):
        return lambda b, h: (0, off + h)

    def grow(off):
        return lambda b, h: (b, off + h, 0, 0)

    in_specs = [
        pl.BlockSpec((1, t, hd), col(q_blk0)),
        pl.BlockSpec((1, t, hd), col(q_blk0 + heads)),
        pl.BlockSpec((1, t, hd), col(q_blk0 + 2 * heads)),
        pl.BlockSpec((1, t, hd), col(z_blk0)),
        pl.BlockSpec((1, tc, hd), col(q_blk0 + heads)),
        pl.BlockSpec((1, tc, hd), col(q_blk0 + 2 * heads)),
        pl.BlockSpec((CONV_WIDTH, hd), cw(0)),
        pl.BlockSpec((CONV_WIDTH, hd), cw(heads)),
        pl.BlockSpec((CONV_WIDTH, hd), cw(2 * heads)),
        pl.BlockSpec((1, t, LANES), lambda b, h: (b, 0, 0)),
        pl.BlockSpec((1, tc, LANES), lambda b, h: (b, 0, 0)),
        pl.BlockSpec((1, 1, n_lat, CHUNK), grow(2 * heads)),
        pl.BlockSpec((1, 1, n_lat, CHUNK), grow(3 * heads)),
        pl.BlockSpec((1, 1, n_ctx, CHUNK), grow(2 * heads)),
        pl.BlockSpec((1, 1, n_ctx, CHUNK), grow(3 * heads)),
        pl.BlockSpec((1, hd), lambda b, h: (0, 0)),
    ]
    scratch = [
        pltpu.VMEM((t, hd), F32), pltpu.VMEM((t, hd), F32),
        pltpu.VMEM((2, t, hd), F32), pltpu.VMEM((2, t, 2 * hd), F32),
        pltpu.VMEM((2, t, hd), F32), pltpu.VMEM((2, t, hd), F32),
        pltpu.VMEM((2, t, LANES), F32),
        pltpu.VMEM((tc, hd), F32), pltpu.VMEM((2, tc, hd), F32),
        pltpu.VMEM((2, tc, 2 * hd), F32), pltpu.VMEM((2, tc, hd), F32),
        pltpu.VMEM((2, tc, LANES), F32),
        pltpu.VMEM((2, n_lat * hd, hd), F32), pltpu.VMEM((2, n_lat * hd, hd), F32),
        pltpu.VMEM((2, t, hd), F32), pltpu.VMEM((2, t, hd), F32),
        pltpu.VMEM((2, n_ctx * hd, hd), F32), pltpu.VMEM((2, n_ctx * hd, hd), F32),
        pltpu.VMEM((2, t, hd), F32),
    ]
    return pl.pallas_call(
        functools.partial(_delta_kernel, heads=heads),
        grid=(bsz, heads),
        in_specs=in_specs,
        out_specs=pl.BlockSpec((1, t, hd), lambda b, h: (b, 0, h)),
        out_shape=jax.ShapeDtypeStruct((bsz, t, heads * hd), BF16),
        scratch_shapes=scratch,
        compiler_params=_cparams(("parallel", "parallel")),
        name="delta",
    )(proj, proj, proj, proj, proj_c, proj_c, conv_w, conv_w, conv_w, g, gc, gr, gr, grc, grc, ong)


def _shift_rows(x, d, idx, size, stride):
    n = x.shape[0]
    xs = pltpu.roll(x, (-d * stride) % n, 0)
    return jnp.where((idx + d >= 0) & (idx + d < size), xs, 0.0)


def _box_sum_1d(x, win, idx, size, stride):
    m = win // 2
    lead = x
    trail = x
    k = 1
    while k < m:
        lead = lead + _shift_rows(lead, k, idx, size, stride)
        trail = trail + _shift_rows(trail, -k, idx, size, stride)
        k *= 2
    return lead + _shift_rows(trail, -1, idx, size, stride)


def _pool_kernel(u_ref, pw_ref, ps_ref, o_ref, *, rows, cols):
    t = u_ref.shape[1]
    gc = pw_ref.shape[1]
    tok = lax.broadcasted_iota(jnp.int32, (t, gc), 0)
    ci = tok % cols
    ri = tok // cols
    for gi, win in enumerate(POOL_WINDOWS):
        lo = win // 2
        hi = win - lo
        u = u_ref[0, :, gi * gc:(gi + 1) * gc]
        s = _box_sum_1d(u, win, ci, cols, 1)
        s = _box_sum_1d(s, win, ri, rows, cols)
        cnt_c = jnp.minimum(ci + hi, cols) - jnp.maximum(ci - lo, 0)
        cnt_r = jnp.minimum(ri + hi, rows) - jnp.maximum(ri - lo, 0)
        mean = s / (cnt_c * cnt_r).astype(F32)
        y = _dot(mean - u, pw_ref[gi]) * ps_ref[:, gi * gc:(gi + 1) * gc]
        o_ref[0, :, gi * gc:(gi + 1) * gc] = y.astype(o_ref.dtype)


def _pool_call(proj, pool_w, pool_scale, rows, cols):
    bsz, t, _ = proj.shape
    ng, gc, _ = pool_w.shape
    pwid = ng * gc
    return pl.pallas_call(
        functools.partial(_pool_kernel, rows=rows, cols=cols),
        grid=(bsz,),
        in_specs=[pl.BlockSpec((1, t, pwid), lambda b: (b, 0, 0)),
                  pl.BlockSpec((ng, gc, gc), lambda b: (0, 0, 0)),
                  pl.BlockSpec((1, pwid), lambda b: (0, 0))],
        out_specs=pl.BlockSpec((1, t, pwid), lambda b: (b, 0, 0)),
        out_shape=jax.ShapeDtypeStruct((bsz, t, pwid), BF16),
        compiler_params=_cparams(("parallel",)),
        name="pool",
    )(proj, pool_w, pool_scale)


def _outproj_kernel(pool_ref, dn_ref, wa_ref, wb_ref, x_ref, mods_ref, g2_ref, wr_ref, br_ref,
                    x1_ref, h2_ref, lg_ref):
    b = pl.program_id(0)
    mix = (jnp.dot(pool_ref[0], wa_ref[...], preferred_element_type=F32)
           + jnp.dot(dn_ref[0], wb_ref[...], preferred_element_type=F32))
    gt1 = mods_ref[2, pl.ds(b, 1), :]
    sh2 = mods_ref[3, pl.ds(b, 1), :]
    sc2 = mods_ref[4, pl.ds(b, 1), :]
    x1 = x_ref[0] + gt1 * mix
    x1_ref[0] = x1
    y = x1 * lax.rsqrt(jnp.mean(x1 * x1, axis=-1, keepdims=True) + EPS) * g2_ref[...]
    h2 = y * (1 + sc2) + sh2
    h2_ref[0] = h2
    lg_ref[0] = _dot_split(h2, wr_ref[...]) + br_ref[...]


def _outproj_call(pool, dn, w_out_bf, x, mods, g2, wr, br):
    bsz, t, d = x.shape
    half = pool.shape[-1]
    tm = min(t, 256)
    return pl.pallas_call(
        _outproj_kernel,
        grid=(bsz, t // tm),
        in_specs=[pl.BlockSpec((1, tm, half), lambda b, i: (b, i, 0)),
                  pl.BlockSpec((1, tm, half), lambda b, i: (b, i, 0)),
                  pl.BlockSpec((half, d), lambda b, i: (0, 0)),
                  pl.BlockSpec((half, d), lambda b, i: (1, 0)),
                  pl.BlockSpec((1, tm, d), lambda b, i: (b, i, 0)),
                  pl.BlockSpec((6, 8, d), lambda b, i: (0, 0, 0)),
                  pl.BlockSpec((1, d), lambda b, i: (0, 0)),
                  pl.BlockSpec((d, LANES), lambda b, i: (0, 0)),
                  pl.BlockSpec((1, LANES), lambda b, i: (0, 0))],
        out_specs=[pl.BlockSpec((1, tm, d), lambda b, i: (b, i, 0)),
                   pl.BlockSpec((1, tm, d), lambda b, i: (b, i, 0)),
                   pl.BlockSpec((1, tm, LANES), lambda b, i: (b, i, 0))],
        out_shape=[jax.ShapeDtypeStruct((bsz, t, d), F32),
                   jax.ShapeDtypeStruct((bsz, t, d), F32),
                   jax.ShapeDtypeStruct((bsz, t, LANES), F32)],
        compiler_params=_cparams(("parallel", "parallel")),
        name="outproj",
    )(pool, dn, w_out_bf, w_out_bf, x, mods, g2, wr, br)


def _router_kernel(lg_ref, dest_ref, gate_ref, be_ref, cnt_s, run_s, off_s):
    p = pl.program_id(0)
    i = pl.program_id(1)
    lg = lg_ref[...]
    tm = lg.shape[0]
    lane = lax.broadcasted_iota(jnp.int32, lg.shape, 1)
    lane_f = lane.astype(F32)
    neg = -jnp.inf
    big = float(LANES)

    grp = jnp.where(lane < N_GROUPS, lg, neg)
    gmax = jnp.max(grp, axis=-1, keepdims=True)
    gidx = jnp.min(jnp.where(grp == gmax, lane_f, big), axis=-1, keepdims=True)
    p_grp = 1.0 / jnp.sum(jnp.where(lane < N_GROUPS, jnp.exp(lg - gmax), 0.0), axis=-1, keepdims=True)
    lo = EXP_LANE0 + EXPERTS_PER_GROUP * gidx
    ev = jnp.where((lane_f >= lo) & (lane_f < lo + EXPERTS_PER_GROUP), lg, neg)
    t1 = jnp.max(ev, axis=-1, keepdims=True)
    i1 = jnp.min(jnp.where(ev == t1, lane_f, big), axis=-1, keepdims=True)
    ev2 = jnp.where(lane_f == i1, neg, ev)
    t2 = jnp.max(ev2, axis=-1, keepdims=True)
    i2 = jnp.min(jnp.where(ev2 == t2, lane_f, big), axis=-1, keepdims=True)
    oh1 = lane_f == i1
    oh2 = lane_f == i2
    cnt = oh1.astype(F32) + oh2.astype(F32)
    colsum = jnp.sum(cnt, axis=0, keepdims=True)

    @pl.when(p == 0)
    def _():
        @pl.when(i == 0)
        def _():
            cnt_s[...] = jnp.zeros_like(cnt_s)

        cnt_s[...] += colsum

    @pl.when(p == 1)
    def _():
        @pl.when(i == 0)
        def _():
            nblk = jnp.floor((cnt_s[...] + (MOE_BLOCK - 1)) * (1.0 / MOE_BLOCK))
            r = lax.broadcasted_iota(jnp.int32, (LANES, LANES), 0)
            c = lax.broadcasted_iota(jnp.int32, (LANES, LANES), 1)
            tri = (r < c).astype(BF16)
            nb8 = jnp.broadcast_to(nblk, (8, LANES))
            start_blk = jnp.dot(nb8.astype(BF16), tri, preferred_element_type=F32)[0:1, :]
            off_s[...] = start_blk * MOE_BLOCK
            run_s[...] = jnp.zeros_like(run_s)
            end_blk = start_blk + nblk
            nb = be_ref.shape[0]
            blk = lax.broadcasted_iota(jnp.int32, (nb, LANES), 0).astype(F32)
            ln = lax.broadcasted_iota(jnp.int32, (nb, LANES), 1)
            is_exp = (ln >= EXP_LANE0) & (ln < EXP_LANE0 + N_EXPERTS)
            done = jnp.sum(jnp.where(is_exp & (end_blk <= blk), 1.0, 0.0), axis=-1, keepdims=True)
            bexp = jnp.minimum(done, N_EXPERTS - 1.0)
            used = jnp.max(jnp.where(is_exp, end_blk, 0.0), axis=-1, keepdims=True)
            be_ref[...] = jnp.where(ln == 0, bexp, jnp.where(ln == 1, used, 0.0)).astype(jnp.int32)

        rr = lax.broadcasted_iota(jnp.int32, (tm, tm), 0)
        cc = lax.broadcasted_iota(jnp.int32, (tm, tm), 1)
        before = (cc < rr).astype(BF16)
        prefix = jnp.dot(before, cnt.astype(BF16), preferred_element_type=F32)
        base = off_s[...] + run_s[...] + prefix
        d1 = jnp.sum(jnp.where(oh1, base, 0.0), axis=-1, keepdims=True)
        d2 = jnp.sum(jnp.where(oh2, base, 0.0), axis=-1, keepdims=True)
        run_s[...] += colsum
        dd = jnp.exp(t2 - t1)
        g1 = p_grp / (1.0 + dd)
        g2 = p_grp * dd / (1.0 + dd)
        dest_ref[...] = jnp.where(lane == 0, d1, jnp.where(lane == 1, d2, 0.0)).astype(jnp.int32)
        gate_ref[...] = jnp.where(lane == 0, g1, jnp.where(lane == 1, g2, 0.0))


def _router_call(logits, n_blocks):
    n = logits.shape[0]
    tm = min(n, 512)
    nb_pad = ((n_blocks + 7) // 8) * 8
    return pl.pallas_call(
        _router_kernel,
        grid=(2, n // tm),
        in_specs=[pl.BlockSpec((tm, LANES), lambda p, i: (i, 0))],
        out_specs=[pl.BlockSpec((tm, LANES), lambda p, i: (i * p, 0)),
                   pl.BlockSpec((tm, LANES), lambda p, i: (i * p, 0)),
                   pl.BlockSpec((nb_pad, LANES), lambda p, i: (0, 0))],
        out_shape=[jax.ShapeDtypeStruct((n, LANES), jnp.int32),
                   jax.ShapeDtypeStruct((n, LANES), F32),
                   jax.ShapeDtypeStruct((nb_pad, LANES), jnp.int32)],
        scratch_shapes=[pltpu.VMEM((1, LANES), F32), pltpu.VMEM((1, LANES), F32),
                        pltpu.VMEM((1, LANES), F32)],
        compiler_params=_cparams(("arbitrary", "arbitrary")),
        name="router",
    )(logits)


def _slots_kernel(dest_ref, zero_ref, inv_ref, sem):
    fill = pltpu.make_async_copy(zero_ref, inv_ref, sem)
    fill.start()
    fill.wait()

    def put(a, carry):
        inv_ref[dest_ref[a]] = a // 2
        return carry

    lax.fori_loop(0, dest_ref.shape[0], put, 0, unroll=16)


def _slots_call(dest_flat, n_rows):
    return pl.pallas_call(
        _slots_kernel,
        in_specs=[pl.BlockSpec(memory_space=pltpu.SMEM), pl.BlockSpec(memory_space=pl.ANY)],
        out_specs=pl.BlockSpec(memory_space=pltpu.SMEM),
        out_shape=jax.ShapeDtypeStruct((n_rows,), jnp.int32),
        scratch_shapes=[pltpu.SemaphoreType.DMA(())],
        name="slots",
    )(dest_flat, jnp.zeros((n_rows,), jnp.int32))


GMM_GROUP = 8
GMM_TILES = 4


def _unit_tables(block_expert, used, n_blocks):
    n_units_max = N_EXPERTS + n_blocks // GMM_GROUP
    valid = jnp.arange(n_blocks) < used
    nblk_e = jnp.sum((block_expert[None, :] == jnp.arange(N_EXPERTS)[:, None]) & valid[None, :],
                     axis=1).astype(jnp.int32)
    first_e = jnp.cumsum(nblk_e) - nblk_e
    units_e = (nblk_e + GMM_GROUP - 1) // GMM_GROUP
    uend = jnp.cumsum(units_e)
    n_units = uend[-1]
    u = jnp.arange(n_units_max, dtype=jnp.int32)
    ue = jnp.minimum(jnp.searchsorted(uend, u, side="right"), N_EXPERTS - 1).astype(jnp.int32)
    k = u - (uend - units_e)[ue]
    live = u < n_units
    ub = jnp.where(live, first_e[ue] + k * GMM_GROUP, 0).astype(jnp.int32)
    un = jnp.where(live, jnp.clip(nblk_e[ue] - k * GMM_GROUP, 0, GMM_GROUP), 0).astype(jnp.int32)
    return ue, ub, un, n_units.reshape(1).astype(jnp.int32)


def _gmm_kernel(ue_ref, ub_ref, un_ref, nu_ref, used_ref, inv_ref, h_ref, w1_ref, w3_ref, w2_ref, y_ref,
                xbuf, acc, gsem, osem, *, n_blocks):
    del ue_ref
    u = pl.program_id(0)
    jt = pl.program_id(1)
    n_units = nu_ref[0]
    slot = u % 2
    last_tile = pl.num_programs(1) - 1

    def blk_rows(i):
        return pl.ds(pl.multiple_of(i * MOE_BLOCK, MOE_BLOCK), MOE_BLOCK)

    def gather(unit, s):
        base = ub_ref[unit] * MOE_BLOCK

        def body(r, carry):
            pltpu.make_async_copy(h_ref.at[pl.ds(inv_ref[base + r], 1)], xbuf.at[s, pl.ds(r, 1)],
                                  gsem.at[s]).start()
            return carry

        lax.fori_loop(0, un_ref[unit] * MOE_BLOCK, body, 0)

    def wait_gather(unit, s):
        rows = pl.ds(0, un_ref[unit] * MOE_BLOCK)
        pltpu.make_async_copy(xbuf.at[s, rows], xbuf.at[s, rows], gsem.at[s]).wait()

    def out_copy(unit, s, i):
        dst = pl.ds(pl.multiple_of((ub_ref[unit] + i) * MOE_BLOCK, MOE_BLOCK), MOE_BLOCK)
        return pltpu.make_async_copy(acc.at[s, blk_rows(i)], y_ref.at[dst], osem.at[s])

    def wait_out(unit, s):
        def body(i, carry):
            out_copy(unit, s, i).wait()
            return carry

        lax.fori_loop(0, un_ref[unit], body, 0)

    @pl.when((u == 0) & (jt == 0))
    def _():
        gather(0, 0)

    @pl.when((jt == 0) & (u < n_units))
    def _():
        @pl.when(u >= 2)
        def _():
            wait_out(u - 2, slot)

        @pl.when(u + 1 < n_units)
        def _():
            gather(u + 1, 1 - slot)

        wait_gather(u, slot)

    @pl.when(u < n_units)
    def _():
        w1b = w1_ref[0].astype(BF16)
        w3b = w3_ref[0].astype(BF16)
        w2b = w2_ref[0].astype(BF16)

        def block(i, carry):
            rows = blk_rows(i)
            xb = xbuf[slot, rows, :].astype(BF16)
            h1 = jnp.dot(xb, w1b, preferred_element_type=F32)
            h3 = jnp.dot(xb, w3b, preferred_element_type=F32)
            part = jnp.dot((_silu(h1) * h3).astype(BF16), w2b, preferred_element_type=F32)

            @pl.when(jt == 0)
            def _():
                acc[slot, rows, :] = part

            @pl.when(jt > 0)
            def _():
                acc[slot, rows, :] += part

            return carry

        lax.fori_loop(0, un_ref[u], block, 0)

        @pl.when(jt == last_tile)
        def _():
            def start(i, carry):
                out_copy(u, slot, i).start()
                return carry

            lax.fori_loop(0, un_ref[u], start, 0)

    @pl.when((u == pl.num_programs(0) - 1) & (jt == last_tile))
    def _():
        last = n_units - 1
        wait_out(last, last % 2)

        @pl.when(n_units >= 2)
        def _():
            wait_out(last - 1, (last - 1) % 2)

        acc[0, blk_rows(0), :] = jnp.zeros((MOE_BLOCK, acc.shape[2]), F32)

        def zero_copy(b):
            dst = pl.ds(pl.multiple_of(b * MOE_BLOCK, MOE_BLOCK), MOE_BLOCK)
            return pltpu.make_async_copy(acc.at[0, blk_rows(0)], y_ref.at[dst], osem.at[0])

        def start(b, carry):
            zero_copy(b).start()
            return carry

        def wait(b, carry):
            zero_copy(b).wait()
            return carry

        lax.fori_loop(used_ref[0], n_blocks, start, 0)
        lax.fori_loop(used_ref[0], n_blocks, wait, 0)


def _gmm_call(block_expert, used, inv, h2, w1, w3, w2):
    n_rows = inv.shape[0]
    d = h2.shape[1]
    de = w1.shape[-1]
    n_blocks = n_rows // MOE_BLOCK
    tj = de // GMM_TILES
    ue, ub, un, nu = _unit_tables(block_expert, used[0], n_blocks)

    def live(u, jt, nu):
        ok = u < nu[0]
        return jnp.where(ok, u, nu[0] - 1), jnp.where(ok, jt, GMM_TILES - 1)

    def w13_map(u, jt, ue, ub, un, nu, used, inv):
        uu, jj = live(u, jt, nu)
        return (ue[uu], 0, jj)

    def w2_map(u, jt, ue, ub, un, nu, used, inv):
        uu, jj = live(u, jt, nu)
        return (ue[uu], jj, 0)

    rows = GMM_GROUP * MOE_BLOCK
    grid_spec = pltpu.PrefetchScalarGridSpec(
        num_scalar_prefetch=6,
        grid=(ue.shape[0], GMM_TILES),
        in_specs=[pl.BlockSpec(memory_space=pl.ANY),
                  pl.BlockSpec((1, d, tj), w13_map),
                  pl.BlockSpec((1, d, tj), w13_map),
                  pl.BlockSpec((1, tj, d), w2_map)],
        out_specs=pl.BlockSpec(memory_space=pl.ANY),
        scratch_shapes=[pltpu.VMEM((2, rows, d), F32), pltpu.VMEM((2, rows, d), F32),
                        pltpu.SemaphoreType.DMA((2,)), pltpu.SemaphoreType.DMA((2,))],
    )
    return pl.pallas_call(
        functools.partial(_gmm_kernel, n_blocks=n_blocks),
        grid_spec=grid_spec,
        out_shape=jax.ShapeDtypeStruct((n_rows, d), F32),
        compiler_params=_cparams(("arbitrary", "arbitrary")),
        name="gmm",
    )(ue, ub, un, nu, used, inv, h2, w1, w3, w2)


def _combine_kernel(dest_ref, y_ref, x1_ref, gate_ref, mods_ref, fg_ref, o_ref, ybuf, sem, *, seq):
    i = pl.program_id(0)
    tm = x1_ref.shape[0]
    slot = i % 2

    def gather(step, s):
        tok0 = step * tm

        def body(r, carry):
            for k in range(2):
                pltpu.make_async_copy(y_ref.at[pl.ds(dest_ref[2 * (tok0 + r) + k], 1)],
                                      ybuf.at[s, k, pl.ds(r, 1)], sem.at[s]).start()
            return carry

        lax.fori_loop(0, tm, body, 0, unroll=4)

    @pl.when(i == 0)
    def _():
        gather(0, 0)

    @pl.when(i + 1 < pl.num_programs(0))
    def _():
        gather(i + 1, 1 - slot)

    pltpu.make_async_copy(ybuf.at[slot], ybuf.at[slot], sem.at[slot]).wait()

    b = (i * tm) // seq
    gt2 = mods_ref[5, pl.ds(b, 1), :]
    gate = gate_ref[...]
    moe = gate[:, 0:1] * ybuf[slot, 0] + gate[:, 1:2] * ybuf[slot, 1]
    x = x1_ref[...] + gt2 * moe
    o_ref[...] = x * lax.rsqrt(jnp.mean(x * x, axis=-1, keepdims=True) + EPS) * fg_ref[...]


def _combine_call(dest_flat, y_pad, x1, gate, mods, final_g, seq):
    n, d = x1.shape
    tm = min(seq, 256)
    grid_spec = pltpu.PrefetchScalarGridSpec(
        num_scalar_prefetch=1,
        grid=(n // tm,),
        in_specs=[pl.BlockSpec(memory_space=pl.ANY),
                  pl.BlockSpec((tm, d), lambda i, dest: (i, 0)),
                  pl.BlockSpec((tm, LANES), lambda i, dest: (i, 0)),
                  pl.BlockSpec((6, 8, d), lambda i, dest: (0, 0, 0)),
                  pl.BlockSpec((1, d), lambda i, dest: (0, 0))],
        out_specs=pl.BlockSpec((tm, d), lambda i, dest: (i, 0)),
        scratch_shapes=[pltpu.VMEM((2, 2, tm, d), F32), pltpu.SemaphoreType.DMA((2,))],
    )
    return pl.pallas_call(
        functools.partial(_combine_kernel, seq=seq),
        grid_spec=grid_spec,
        out_shape=jax.ShapeDtypeStruct((n, d), F32),
        compiler_params=_cparams(("arbitrary",)),
        name="combine",
    )(dest_flat, y_pad, x1, gate, mods, final_g)


def _pad_lanes(a, lane0):
    return jnp.zeros((LANES,), F32).at[lane0:lane0 + a.shape[0]].set(a.astype(F32))


def kernel(x, c, ctx, c_ctx, w_mod, b_mod, norm1_g, w_in, pool_w, pool_scale, conv_w,
           a_log_f, dt_bias_f, a_log_b, dt_bias_b, out_norm_g, w_out, norm2_g,
           w_grp, b_grp, w_rt, b_rt, w1, w3, w2, final_g):
    bsz, t, d = x.shape
    depth = w_mod.shape[0]
    assert depth == 1, "single-layer problem: the context stream is read but never updated"
    heads = a_log_f.shape[1]
    pool_width = pool_w.shape[1] * pool_w.shape[2]
    dn_width = heads * HEAD_DIM
    q0 = pool_width
    z0 = q0 + 3 * dn_width
    ab0 = z0 + dn_width
    n_tok = bsz * t
    l = 0

    c8 = jnp.zeros((8, d), F32).at[:bsz].set(c).at[bsz].set(c_ctx)
    mod = _mod_call(c8, w_mod[l], b_mod[l])
    mods = mod.reshape(8, 6, d).transpose(1, 0, 2)

    w_in_bf = w_in[l].astype(BF16)
    wab_bf = jnp.zeros((d, LANES), BF16).at[:, :4 * heads].set(w_in_bf[:, ab0:])
    g1 = norm1_g[l].reshape(1, d)
    proj, ab = _inproj_call(x, mods, g1, w_in_bf, wab_bf, ab0, None)
    proj_c, ab_c = _inproj_call(ctx, mods, g1, w_in_bf, wab_bf, ab0, bsz)

    prm = jnp.zeros((8, LANES), F32)
    prm = prm.at[0].set(_pad_lanes(jnp.concatenate([a_log_f[l], a_log_b[l]]), 2 * heads))
    prm = prm.at[1].set(_pad_lanes(jnp.concatenate([dt_bias_f[l], dt_bias_b[l]]), 2 * heads))
    g, g_t = _gates_call(ab, prm, heads)
    gc, gc_t = _gates_call(ab_c, prm, heads)
    gr = g_t.reshape(bsz, LANES, t // CHUNK, CHUNK)
    grc = gc_t.reshape(bsz, LANES, ctx.shape[1] // CHUNK, CHUNK)

    dn = _delta_call(proj, proj_c, conv_w[l], g, gc, gr, grc, out_norm_g[l].reshape(1, HEAD_DIM),
                     heads, q0 // HEAD_DIM, z0 // HEAD_DIM)
    pool = _pool_call(proj, pool_w[l], pool_scale[l].reshape(1, pool_width), t // GRID_W, GRID_W)

    wr = jnp.zeros((d, LANES), F32).at[:, :N_GROUPS].set(w_grp[l]).at[:, EXP_LANE0:EXP_LANE0 + N_EXPERTS].set(w_rt[l])
    br = jnp.zeros((LANES,), F32).at[:N_GROUPS].set(b_grp[l]).at[EXP_LANE0:EXP_LANE0 + N_EXPERTS].set(b_rt[l])
    x1, h2, logits = _outproj_call(pool, dn, w_out[l].astype(BF16), x, mods, norm2_g[l].reshape(1, d),
                                   wr, br.reshape(1, LANES))

    n_blocks = (n_tok * 2 + N_EXPERTS * (MOE_BLOCK - 1) + MOE_BLOCK - 1) // MOE_BLOCK
    dest, gate, be = _router_call(logits.reshape(n_tok, LANES), n_blocks)
    dest_flat = dest[:, :2].reshape(-1)
    block_expert = be[:n_blocks, 0]
    used = be[0:1, 1]
    inv = _slots_call(dest_flat, n_blocks * MOE_BLOCK)
    y_pad = _gmm_call(block_expert, used, inv, h2.reshape(n_tok, d), w1[l], w3[l], w2[l])
    out = _combine_call(dest_flat, y_pad, x1.reshape(n_tok, d), gate, mods, final_g.reshape(1, d), t)
    return out.reshape(bsz, t, d)
```

```python
import functools

import jax
import jax.numpy as jnp
from jax import lax
from jax.experimental import pallas as pl
from jax.experimental.pallas import tpu as pltpu

F32 = jnp.float32
BF16 = jnp.bfloat16

GRID_W = 64
POOL_WINDOWS = (2, 4, 8, 16)
HEAD_DIM = 128
CONV_WIDTH = 5
CHUNK = 64
N_GROUPS = 4
EXPERTS_PER_GROUP = 8
N_EXPERTS = N_GROUPS * EXPERTS_PER_GROUP
MOE_BLOCK = 128
EPS = 1e-6
LANES = 128
PRE_UNROLL = 8
EXP_LANE0 = N_GROUPS

VMEM_LIMIT = 56 * 1024 * 1024


def _cparams(sem):
    return pltpu.CompilerParams(dimension_semantics=sem, vmem_limit_bytes=VMEM_LIMIT)


def _dot(a, b):
    return jnp.dot(a.astype(BF16), b.astype(BF16), preferred_element_type=F32)


def _dot_split(a, b):
    a_hi = a.astype(BF16)
    b_hi = b.astype(BF16)
    a_lo = (a - a_hi.astype(F32)).astype(BF16)
    b_lo = (b - b_hi.astype(F32)).astype(BF16)
    return (jnp.dot(a_hi, b_hi, preferred_element_type=F32)
            + jnp.dot(a_lo, b_hi, preferred_element_type=F32)
            + jnp.dot(a_hi, b_lo, preferred_element_type=F32))


def _silu(x):
    return x * jax.nn.sigmoid(x)


def _mod_kernel(c_ref, w_ref, b_ref, o_ref):
    o_ref[...] = _dot(_silu(c_ref[...]), w_ref[...]) + b_ref[...]


def _mod_call(c8, w_mod, b_mod):
    d, n = w_mod.shape
    tn = 512
    return pl.pallas_call(
        _mod_kernel,
        grid=(n // tn,),
        in_specs=[pl.BlockSpec((8, d), lambda j: (0, 0)),
                  pl.BlockSpec((d, tn), lambda j: (0, j)),
                  pl.BlockSpec((1, tn), lambda j: (0, j))],
        out_specs=pl.BlockSpec((8, tn), lambda j: (0, j)),
        out_shape=jax.ShapeDtypeStruct((8, n), F32),
        compiler_params=_cparams(("parallel",)),
        name="mod",
    )(c8, w_mod, b_mod.reshape(1, n))


def _inproj_kernel(x_ref, mods_ref, g_ref, w_ref, wab_ref, o_ref, ab_ref, hn_ref, *, mod_row):
    b = pl.program_id(0)
    j = pl.program_id(2)

    @pl.when(j == 0)
    def _():
        x = x_ref[0]
        y = x * lax.rsqrt(jnp.mean(x * x, axis=-1, keepdims=True) + EPS) * g_ref[...]
        row = b if mod_row is None else mod_row
        sh = mods_ref[0, pl.ds(row, 1), :]
        sc = mods_ref[1, pl.ds(row, 1), :]
        hb = (y * (1 + sc) + sh).astype(BF16)
        hn_ref[...] = hb
        ab_ref[0] = jnp.dot(hb, wab_ref[...], preferred_element_type=F32)

    o_ref[0] = jnp.dot(hn_ref[...], w_ref[...], preferred_element_type=F32)


def _inproj_call(x, mods, g, w_bf, wab_bf, n_main, mod_row):
    bsz, t, d = x.shape
    tm = min(t, 1024)
    tn = 1024
    kern = functools.partial(_inproj_kernel, mod_row=mod_row)
    return pl.pallas_call(
        kern,
        grid=(bsz, t // tm, n_main // tn),
        in_specs=[pl.BlockSpec((1, tm, d), lambda b, i, j: (b, i, 0)),
                  pl.BlockSpec((2, 8, d), lambda b, i, j: (0, 0, 0)),
                  pl.BlockSpec((1, d), lambda b, i, j: (0, 0)),
                  pl.BlockSpec((d, tn), lambda b, i, j: (0, j)),
                  pl.BlockSpec((d, LANES), lambda b, i, j: (0, 0))],
        out_specs=[pl.BlockSpec((1, tm, tn), lambda b, i, j: (b, i, j)),
                   pl.BlockSpec((1, tm, LANES), lambda b, i, j: (b, i, 0))],
        out_shape=[jax.ShapeDtypeStruct((bsz, t, n_main), F32),
                   jax.ShapeDtypeStruct((bsz, t, LANES), F32)],
        scratch_shapes=[pltpu.VMEM((tm, d), BF16)],
        compiler_params=_cparams(("parallel", "parallel", "arbitrary")),
        name="inproj",
    )(x, mods, g, w_bf, wab_bf)


def _gates_kernel(ab_ref, prm_ref, g_ref, gt_ref, *, heads):
    ab = ab_ref[0]
    t = ab.shape[0]
    h2, h3, h4, h6 = 2 * heads, 3 * heads, 4 * heads, 6 * heads
    beta = jax.nn.sigmoid(ab)
    xx = ab + prm_ref[1:2, :]
    softplus = jnp.maximum(xx, 0.0) + jnp.log1p(jnp.exp(-jnp.abs(xx)))
    g = -jnp.exp(prm_ref[0:1, :]) * softplus
    pos = lax.broadcasted_iota(jnp.int32, ab.shape, 0) & (CHUNK - 1)
    cs = g
    ss = g
    s = 1
    while s < CHUNK:
        cs = cs + jnp.where(pos >= s, pltpu.roll(cs, s, 0), 0.0)
        ss = ss + jnp.where(pos < CHUNK - s, pltpu.roll(ss, t - s, 0), 0.0)
        s *= 2
    tot = pltpu.roll(cs + ss - g, h2, 1)
    lane = lax.broadcasted_iota(jnp.int32, ab.shape, 1)
    out = jnp.where(lane < h2, beta,
                    jnp.where(lane < h3, cs,
                              jnp.where(lane < h4, ss,
                                        jnp.where(lane < h6, tot, 0.0))))
    g_ref[0] = out
    gt_ref[0] = out.T


def _gates_call(ab, prm, heads):
    bsz, t, _ = ab.shape
    return pl.pallas_call(
        functools.partial(_gates_kernel, heads=heads),
        grid=(bsz,),
        in_specs=[pl.BlockSpec((1, t, LANES), lambda b: (b, 0, 0)),
                  pl.BlockSpec((8, LANES), lambda b: (0, 0))],
        out_specs=[pl.BlockSpec((1, t, LANES), lambda b: (b, 0, 0)),
                   pl.BlockSpec((1, LANES, t), lambda b: (b, 0, 0))],
        out_shape=[jax.ShapeDtypeStruct((bsz, t, LANES), F32),
                   jax.ShapeDtypeStruct((bsz, LANES, t), F32)],
        compiler_params=_cparams(("parallel",)),
        name="gates",
    )(ab, prm)


def _conv_silu(x, w):
    n = x.shape[0]
    row = lax.broadcasted_iota(jnp.int32, x.shape, 0)
    acc = x * w[CONV_WIDTH // 2:CONV_WIDTH // 2 + 1, :]
    for j in range(CONV_WIDTH):
        d = j - CONV_WIDTH // 2
        if d == 0:
            continue
        xs = pltpu.roll(x, (-d) % n, 0)
        valid = (row + d >= 0) & (row + d < n)
        acc = acc + jnp.where(valid, xs, 0.0) * w[j:j + 1, :]
    return _silu(acc)


def _l2norm(a):
    return a * lax.rsqrt(jnp.sum(a * a, axis=-1, keepdims=True) + EPS)


def _lane_col(g, lane_idx):
    lane = lax.broadcasted_iota(jnp.int32, g.shape, 1)
    return jnp.sum(jnp.where(lane == lane_idx, g, 0.0), axis=-1, keepdims=True)


def _chunk_terms(chains, between=()):
    pending = list(between)

    def stage_done():
        if pending:
            pending.pop(0)()

    c, hd = chains[0][0].shape
    ri = lax.broadcasted_iota(jnp.int32, (c, c), 0)
    ci = lax.broadcasted_iota(jnp.int32, (c, c), 1)
    eye = jnp.where(ri == ci, 1.0, 0.0)
    right = lax.broadcasted_iota(jnp.int32, (c, 2 * c), 1) >= c
    nt = (((1,), (1,)), ((), ()))
    tn = (((0,), (0,)), ((), ()))

    decs, kn_bs, zs = [], [], []
    for kn_c, kb_c, _, _, _, _, gcc, gcr, upper in chains:
        incl = (ri <= ci) if upper else (ri >= ci)
        strict = (ri < ci) if upper else (ri > ci)
        dec = jnp.where(incl, jnp.exp(jnp.where(incl, gcc - gcr, 0.0)), 0.0)
        kn_b = kn_c.astype(BF16)
        kk = lax.dot_general(kb_c.astype(BF16), kn_b, nt, preferred_element_type=F32)
        decs.append(dec)
        kn_bs.append(kn_b)
        zs.append(jnp.concatenate([-jnp.where(strict, kk * dec, 0.0), eye], axis=1))
    stage_done()
    n = 1
    while n < c:
        zs = [_dot(z[:, :c], z) + jnp.where(right, z, 0.0) for z in zs]
        stage_done()
        n *= 2
    uw_bs = [_dot(z[:, c:], ch[3]).astype(BF16) for z, ch in zip(zs, chains)]
    stage_done()
    wns = [lax.dot_general(ch[4].astype(BF16), uw_b, tn, preferred_element_type=F32)
           for uw_b, ch in zip(uw_bs, chains)]
    stage_done()
    qks = [None if ch[2] is None else
           lax.dot_general(ch[2].astype(BF16), kn_b, nt, preferred_element_type=F32) * dec
           for ch, kn_b, dec in zip(chains, kn_bs, decs)]
    stage_done()
    qws = [None if qk is None else jnp.dot(qk.astype(BF16), uw_b, preferred_element_type=F32)
           for qk, uw_b in zip(qks, uw_bs)]
    while pending:
        stage_done()
    out = []
    for ch, wn, qw in zip(chains, wns, qws):
        nc, w2 = wn[:, :hd], wn[:, hd:]
        if qw is None:
            out.append((w2, nc, None, None))
        else:
            out.append((w2, nc, ch[5] - qw[:, hd:], qw[:, :hd]))
    return out


def _delta_kernel(q_ref, k_ref, v_ref, z_ref, kc_ref, vc_ref, cwq_ref, cwk_ref, cwv_ref,
                  g_ref, gc_ref, grf_ref, grb_ref, grcf_ref, grcb_ref, ong_ref,
                  out_ref,
                  qn_s, kn_s, kb_s, rhs_s, kd_s, qd_s, col_s,
                  knc_s, kbc_s, rhsc_s, kdc_s, colc_s,
                  w2_s, nc_s, qp_s, o0_s, w2c_s, ncc_s, o_s, *, heads):
    h = pl.program_id(1)
    t = q_ref.shape[1]
    tc = kc_ref.shape[1]
    hd = HEAD_DIM
    n_lat = t // CHUNK
    n_ctx = tc // CHUNK

    qn = _l2norm(_conv_silu(q_ref[0], cwq_ref[...])) * (hd ** -0.5)
    kn = _l2norm(_conv_silu(k_ref[0], cwk_ref[...]))
    vv = _conv_silu(v_ref[0], cwv_ref[...])
    knc = _l2norm(_conv_silu(kc_ref[0], cwk_ref[...]))
    vvc = _conv_silu(vc_ref[0], cwv_ref[...])
    qn_s[...] = qn
    kn_s[...] = kn
    knc_s[...] = knc
    g_lat = g_ref[0]
    g_ctx = gc_ref[0]

    for d in range(2):
        for (gt_, kn_, vv_, kb_r, rhs_r, kd_r, col_r, qn_, qd_r) in (
                (g_lat, kn, vv, kb_s, rhs_s, kd_s, col_s, qn, qd_s),
                (g_ctx, knc, vvc, kbc_s, rhsc_s, kdc_s, colc_s, None, None)):
            beta = _lane_col(gt_, d * heads + h)
            gcum = _lane_col(gt_, (2 + d) * heads + h)
            gtot = _lane_col(gt_, (4 + d) * heads + h)
            e = jnp.exp(gcum)
            kb = kn_ * beta
            kb_r[d] = kb
            rhs_r[d, :, 0:hd] = vv_ * beta
            rhs_r[d, :, hd:2 * hd] = kb * e
            kd_r[d] = kn_ * jnp.exp(gtot - gcum)
            lane = lax.broadcasted_iota(jnp.int32, (gcum.shape[0], LANES), 1)
            col_r[d] = jnp.where(lane == 0, gcum, jnp.where(lane == 1, jnp.exp(gtot), 0.0))
            if qn_ is not None:
                qd_r[d] = qn_ * e

    gr_lat = (grf_ref, grb_ref)
    gr_ctx = (grcf_ref, grcb_ref)

    un_ctx = min(PRE_UNROLL, n_ctx)
    un_lat = min(PRE_UNROLL, n_lat)

    def pre_ctx(i, carry):
        ids = [(i * un_ctx + u, d) for u in range(un_ctx) for d in range(2)]
        chains = []
        for c, d in ids:
            rows = pl.ds(pl.multiple_of(c * CHUNK, CHUNK), CHUNK)
            chains.append((knc_s[rows, :], kbc_s[d, rows, :], None, rhsc_s[d, rows, :], kdc_s[d, rows, :],
                           None, colc_s[d, rows, 0:1], gr_ctx[d][0, 0, pl.ds(c, 1), :], d == 1))
        for (c, d), (w2, nc, _, _) in zip(ids, _chunk_terms(chains)):
            m0 = pl.multiple_of(c * hd, hd)
            w2c_s[d, pl.ds(m0, hd), :] = w2
            ncc_s[d, pl.ds(m0, hd), :] = nc
        return carry

    lax.fori_loop(0, n_ctx // un_ctx, pre_ctx, 0)

    def lat_chunk(i, u, d):
        k = i * un_lat + u
        return k if d == 0 else n_lat - 1 - k

    def pre_lat(i, between=()):
        ids = [(lat_chunk(i, u, d), d) for u in range(un_lat) for d in range(2)]
        chains = []
        for c, d in ids:
            rows = pl.ds(pl.multiple_of(c * CHUNK, CHUNK), CHUNK)
            chains.append((kn_s[rows, :], kb_s[d, rows, :], qn_s[rows, :], rhs_s[d, rows, :], kd_s[d, rows, :],
                           qd_s[d, rows, :], col_s[d, rows, 0:1], gr_lat[d][0, 0, pl.ds(c, 1), :], d == 1))
        for (c, d), (w2, nc, qp, o0) in zip(ids, _chunk_terms(chains, between)):
            rows = pl.ds(pl.multiple_of(c * CHUNK, CHUNK), CHUNK)
            m0 = pl.multiple_of(c * hd, hd)
            w2_s[d, pl.ds(m0, hd), :] = w2
            nc_s[d, pl.ds(m0, hd), :] = nc
            qp_s[d, rows, :] = qp
            o0_s[d, rows, :] = o0

    def scan_ctx(i, states):
        new = []
        for d in range(2):
            s = states[d]
            c = i if d == 0 else n_ctx - 1 - i
            m0 = pl.multiple_of(c * hd, hd)
            gt = colc_s[d, pl.ds(pl.multiple_of(c * CHUNK, CHUNK), 1), 1:2]
            s = gt * s + ncc_s[d, pl.ds(m0, hd), :] - _dot(w2c_s[d, pl.ds(m0, hd), :], s)
            new.append(s)
        return tuple(new)

    zero = jnp.zeros((hd, hd), F32)
    states = lax.fori_loop(0, n_ctx, scan_ctx, (zero, zero))

    def scan_lat_steps(i, box):
        def step(u):
            def run():
                for d in range(2):
                    s = box[d]
                    c = lat_chunk(i, u, d)
                    r0 = pl.multiple_of(c * CHUNK, CHUNK)
                    rows = pl.ds(r0, CHUNK)
                    m0 = pl.multiple_of(c * hd, hd)
                    s_b = s.astype(BF16)
                    o_s[d, rows, :] = jnp.dot(qp_s[d, rows, :].astype(BF16), s_b,
                                              preferred_element_type=F32) + o0_s[d, rows, :]
                    gt = col_s[d, pl.ds(r0, 1), 1:2]
                    box[d] = gt * s + nc_s[d, pl.ds(m0, hd), :] - jnp.dot(
                        w2_s[d, pl.ds(m0, hd), :].astype(BF16), s_b, preferred_element_type=F32)
            return run

        return [step(u) for u in range(un_lat)]

    pre_lat(0)

    def lat_body(i, states):
        box = list(states)
        pre_lat(i, scan_lat_steps(i - 1, box))
        return tuple(box)

    states = lax.fori_loop(1, n_lat // un_lat, lat_body, states)
    box = list(states)
    for run in scan_lat_steps(n_lat // un_lat - 1, box):
        run()

    o = o_s[0] + o_s[1]
    o = o * lax.rsqrt(jnp.mean(o * o, axis=-1, keepdims=True) + EPS) * ong_ref[...]
    out_ref[0] = (o * _silu(z_ref[0])).astype(out_ref.dtype)


def _delta_call(proj, proj_c, conv_w, g, gc, gr, grc, ong, heads, q_blk0, z_blk0):
    bsz, t, _ = proj.shape
    tc = proj_c.shape[1]
    hd = HEAD_DIM
    n_lat, n_ctx = t // CHUNK, tc // CHUNK

    def col(off):
        return lambda b, h: (b, 0, off + h)

    def cw(off):
        return lambda b, h: (0, off + h)

    def grow(off):
        return lambda b, h: (b, off + h, 0, 0)

    in_specs = [
        pl.BlockSpec((1, t, hd), col(q_blk0)),
        pl.BlockSpec((1, t, hd), col(q_blk0 + heads)),
        pl.BlockSpec((1, t, hd), col(q_blk0 + 2 * heads)),
        pl.BlockSpec((1, t, hd), col(z_blk0)),
        pl.BlockSpec((1, tc, hd), col(q_blk0 + heads)),
        pl.BlockSpec((1, tc, hd), col(q_blk0 + 2 * heads)),
        pl.BlockSpec((CONV_WIDTH, hd), cw(0)),
        pl.BlockSpec((CONV_WIDTH, hd), cw(heads)),
        pl.BlockSpec((CONV_WIDTH, hd), cw(2 * heads)),
        pl.BlockSpec((1, t, LANES), lambda b, h: (b, 0, 0)),
        pl.BlockSpec((1, tc, LANES), lambda b, h: (b, 0, 0)),
        pl.BlockSpec((1, 1, n_lat, CHUNK), grow(2 * heads)),
        pl.BlockSpec((1, 1, n_lat, CHUNK), grow(3 * heads)),
        pl.BlockSpec((1, 1, n_ctx, CHUNK), grow(2 * heads)),
        pl.BlockSpec((1, 1, n_ctx, CHUNK), grow(3 * heads)),
        pl.BlockSpec((1, hd), lambda b, h: (0, 0)),
    ]
    scratch = [
        pltpu.VMEM((t, hd), F32), pltpu.VMEM((t, hd), F32),
        pltpu.VMEM((2, t, hd), F32), pltpu.VMEM((2, t, 2 * hd), F32),
        pltpu.VMEM((2, t, hd), F32), pltpu.VMEM((2, t, hd), F32),
        pltpu.VMEM((2, t, LANES), F32),
        pltpu.VMEM((tc, hd), F32), pltpu.VMEM((2, tc, hd), F32),
        pltpu.VMEM((2, tc, 2 * hd), F32), pltpu.VMEM((2, tc, hd), F32),
        pltpu.VMEM((2, tc, LANES), F32),
        pltpu.VMEM((2, n_lat * hd, hd), F32), pltpu.VMEM((2, n_lat * hd, hd), F32),
        pltpu.VMEM((2, t, hd), F32), pltpu.VMEM((2, t, hd), F32),
        pltpu.VMEM((2, n_ctx * hd, hd), F32), pltpu.VMEM((2, n_ctx * hd, hd), F32),
        pltpu.VMEM((2, t, hd), F32),
    ]
    return pl.pallas_call(
        functools.partial(_delta_kernel, heads=heads),
        grid=(bsz, heads),
        in_specs=in_specs,
        out_specs=pl.BlockSpec((1, t, hd), lambda b, h: (b, 0, h)),
        out_shape=jax.ShapeDtypeStruct((bsz, t, heads * hd), BF16),
        scratch_shapes=scratch,
        compiler_params=_cparams(("parallel", "parallel")),
        name="delta",
    )(proj, proj, proj, proj, proj_c, proj_c, conv_w, conv_w, conv_w, g, gc, gr, gr, grc, grc, ong)


def _shift_rows(x, d, idx, size, stride):
    n = x.shape[0]
    xs = pltpu.roll(x, (-d * stride) % n, 0)
    return jnp.where((idx + d >= 0) & (idx + d < size), xs, 0.0)


def _box_sum_1d(x, win, idx, size, stride):
    m = win // 2
    lead = x
    trail = x
    k = 1
    while k < m:
        lead = lead + _shift_rows(lead, k, idx, size, stride)
        trail = trail + _shift_rows(trail, -k, idx, size, stride)
        k *= 2
    return lead + _shift_rows(trail, -1, idx, size, stride)


def _pool_kernel(u_ref, pw_ref, ps_ref, o_ref, *, rows, cols):
    t = u_ref.shape[1]
    gc = pw_ref.shape[1]
    tok = lax.broadcasted_iota(jnp.int32, (t, gc), 0)
    ci = tok % cols
    ri = tok // cols
    for gi, win in enumerate(POOL_WINDOWS):
        lo = win // 2
        hi = win - lo
        u = u_ref[0, :, gi * gc:(gi + 1) * gc]
        s = _box_sum_1d(u, win, ci, cols, 1)
        s = _box_sum_1d(s, win, ri, rows, cols)
        cnt_c = jnp.minimum(ci + hi, cols) - jnp.maximum(ci - lo, 0)
        cnt_r = jnp.minimum(ri + hi, rows) - jnp.maximum(ri - lo, 0)
        mean = s / (cnt_c * cnt_r).astype(F32)
        y = _dot(mean - u, pw_ref[gi]) * ps_ref[:, gi * gc:(gi + 1) * gc]
        o_ref[0, :, gi * gc:(gi + 1) * gc] = y.astype(o_ref.dtype)


def _pool_call(proj, pool_w, pool_scale, rows, cols):
    bsz, t, _ = proj.shape
    ng, gc, _ = pool_w.shape
    pwid = ng * gc
    return pl.pallas_call(
        functools.partial(_pool_kernel, rows=rows, cols=cols),
        grid=(bsz,),
        in_specs=[pl.BlockSpec((1, t, pwid), lambda b: (b, 0, 0)),
                  pl.BlockSpec((ng, gc, gc), lambda b: (0, 0, 0)),
                  pl.BlockSpec((1, pwid), lambda b: (0, 0))],
        out_specs=pl.BlockSpec((1, t, pwid), lambda b: (b, 0, 0)),
        out_shape=jax.ShapeDtypeStruct((bsz, t, pwid), BF16),
        compiler_params=_cparams(("parallel",)),
        name="pool",
    )(proj, pool_w, pool_scale)


def _outproj_kernel(pool_ref, dn_ref, wa_ref, wb_ref, x_ref, mods_ref, g2_ref, wr_ref, br_ref,
                    x1_ref, h2_ref, lg_ref):
    b = pl.program_id(0)
    mix = (jnp.dot(pool_ref[0], wa_ref[...], preferred_element_type=F32)
           + jnp.dot(dn_ref[0], wb_ref[...], preferred_element_type=F32))
    gt1 = mods_ref[2, pl.ds(b, 1), :]
    sh2 = mods_ref[3, pl.ds(b, 1), :]
    sc2 = mods_ref[4, pl.ds(b, 1), :]
    x1 = x_ref[0] + gt1 * mix
    x1_ref[0] = x1
    y = x1 * lax.rsqrt(jnp.mean(x1 * x1, axis=-1, keepdims=True) + EPS) * g2_ref[...]
    h2 = y * (1 + sc2) + sh2
    h2_ref[0] = h2
    lg_ref[0] = _dot_split(h2, wr_ref[...]) + br_ref[...]


def _outproj_call(pool, dn, w_out_bf, x, mods, g2, wr, br):
    bsz, t, d = x.shape
    half = pool.shape[-1]
    tm = min(t, 256)
    return pl.pallas_call(
        _outproj_kernel,
        grid=(bsz, t // tm),
        in_specs=[pl.BlockSpec((1, tm, half), lambda b, i: (b, i, 0)),
                  pl.BlockSpec((1, tm, half), lambda b, i: (b, i, 0)),
                  pl.BlockSpec((half, d), lambda b, i: (0, 0)),
                  pl.BlockSpec((half, d), lambda b, i: (1, 0)),
                  pl.BlockSpec((1, tm, d), lambda b, i: (b, i, 0)),
                  pl.BlockSpec((6, 8, d), lambda b, i: (0, 0, 0)),
                  pl.BlockSpec((1, d), lambda b, i: (0, 0)),
                  pl.BlockSpec((d, LANES), lambda b, i: (0, 0)),
                  pl.BlockSpec((1, LANES), lambda b, i: (0, 0))],
        out_specs=[pl.BlockSpec((1, tm, d), lambda b, i: (b, i, 0)),
                   pl.BlockSpec((1, tm, d), lambda b, i: (b, i, 0)),
                   pl.BlockSpec((1, tm, LANES), lambda b, i: (b, i, 0))],
        out_shape=[jax.ShapeDtypeStruct((bsz, t, d), F32),
                   jax.ShapeDtypeStruct((bsz, t, d), F32),
                   jax.ShapeDtypeStruct((bsz, t, LANES), F32)],
        compiler_params=_cparams(("parallel", "parallel")),
        name="outproj",
    )(pool, dn, w_out_bf, w_out_bf, x, mods, g2, wr, br)


def _router_kernel(lg_ref, dest_ref, gate_ref, be_ref, cnt_s, run_s, off_s):
    p = pl.program_id(0)
    i = pl.program_id(1)
    lg = lg_ref[...]
    tm = lg.shape[0]
    lane = lax.broadcasted_iota(jnp.int32, lg.shape, 1)
    lane_f = lane.astype(F32)
    neg = -jnp.inf
    big = float(LANES)

    grp = jnp.where(lane < N_GROUPS, lg, neg)
    gmax = jnp.max(grp, axis=-1, keepdims=True)
    gidx = jnp.min(jnp.where(grp == gmax, lane_f, big), axis=-1, keepdims=True)
    p_grp = 1.0 / jnp.sum(jnp.where(lane < N_GROUPS, jnp.exp(lg - gmax), 0.0), axis=-1, keepdims=True)
    lo = EXP_LANE0 + EXPERTS_PER_GROUP * gidx
    ev = jnp.where((lane_f >= lo) & (lane_f < lo + EXPERTS_PER_GROUP), lg, neg)
    t1 = jnp.max(ev, axis=-1, keepdims=True)
    i1 = jnp.min(jnp.where(ev == t1, lane_f, big), axis=-1, keepdims=True)
    ev2 = jnp.where(lane_f == i1, neg, ev)
    t2 = jnp.max(ev2, axis=-1, keepdims=True)
    i2 = jnp.min(jnp.where(ev2 == t2, lane_f, big), axis=-1, keepdims=True)
    oh1 = lane_f == i1
    oh2 = lane_f == i2
    cnt = oh1.astype(F32) + oh2.astype(F32)
    colsum = jnp.sum(cnt, axis=0, keepdims=True)

    @pl.when(p == 0)
    def _():
        @pl.when(i == 0)
        def _():
            cnt_s[...] = jnp.zeros_like(cnt_s)

        cnt_s[...] += colsum

    @pl.when(p == 1)
    def _():
        @pl.when(i == 0)
        def _():
            nblk = jnp.floor((cnt_s[...] + (MOE_BLOCK - 1)) * (1.0 / MOE_BLOCK))
            r = lax.broadcasted_iota(jnp.int32, (LANES, LANES), 0)
            c = lax.broadcasted_iota(jnp.int32, (LANES, LANES), 1)
            tri = (r < c).astype(BF16)
            nb8 = jnp.broadcast_to(nblk, (8, LANES))
            start_blk = jnp.dot(nb8.astype(BF16), tri, preferred_element_type=F32)[0:1, :]
            off_s[...] = start_blk * MOE_BLOCK
            run_s[...] = jnp.zeros_like(run_s)
            end_blk = start_blk + nblk
            nb = be_ref.shape[0]
            blk = lax.broadcasted_iota(jnp.int32, (nb, LANES), 0).astype(F32)
            ln = lax.broadcasted_iota(jnp.int32, (nb, LANES), 1)
            is_exp = (ln >= EXP_LANE0) & (ln < EXP_LANE0 + N_EXPERTS)
            done = jnp.sum(jnp.where(is_exp & (end_blk <= blk), 1.0, 0.0), axis=-1, keepdims=True)
            bexp = jnp.minimum(done, N_EXPERTS - 1.0)
            used = jnp.max(jnp.where(is_exp, end_blk, 0.0), axis=-1, keepdims=True)
            be_ref[...] = jnp.where(ln == 0, bexp, jnp.where(ln == 1, used, 0.0)).astype(jnp.int32)

        rr = lax.broadcasted_iota(jnp.int32, (tm, tm), 0)
        cc = lax.broadcasted_iota(jnp.int32, (tm, tm), 1)
        before = (cc < rr).astype(BF16)
        prefix = jnp.dot(before, cnt.astype(BF16), preferred_element_type=F32)
        base = off_s[...] + run_s[...] + prefix
        d1 = jnp.sum(jnp.where(oh1, base, 0.0), axis=-1, keepdims=True)
        d2 = jnp.sum(jnp.where(oh2, base, 0.0), axis=-1, keepdims=True)
        run_s[...] += colsum
        dd = jnp.exp(t2 - t1)
        g1 = p_grp / (1.0 + dd)
        g2 = p_grp * dd / (1.0 + dd)
        dest_ref[...] = jnp.where(lane == 0, d1, jnp.where(lane == 1, d2, 0.0)).astype(jnp.int32)
        gate_ref[...] = jnp.where(lane == 0, g1, jnp.where(lane == 1, g2, 0.0))


def _router_call(logits, n_blocks):
    n = logits.shape[0]
    tm = min(n, 512)
    nb_pad = ((n_blocks + 7) // 8) * 8
    return pl.pallas_call(
        _router_kernel,
        grid=(2, n // tm),
        in_specs=[pl.BlockSpec((tm, LANES), lambda p, i: (i, 0))],
        out_specs=[pl.BlockSpec((tm, LANES), lambda p, i: (i * p, 0)),
                   pl.BlockSpec((tm, LANES), lambda p, i: (i * p, 0)),
                   pl.BlockSpec((nb_pad, LANES), lambda p, i: (0, 0))],
        out_shape=[jax.ShapeDtypeStruct((n, LANES), jnp.int32),
                   jax.ShapeDtypeStruct((n, LANES), F32),
                   jax.ShapeDtypeStruct((nb_pad, LANES), jnp.int32)],
        scratch_shapes=[pltpu.VMEM((1, LANES), F32), pltpu.VMEM((1, LANES), F32),
                        pltpu.VMEM((1, LANES), F32)],
        compiler_params=_cparams(("arbitrary", "arbitrary")),
        name="router",
    )(logits)


def _slots_kernel(dest_ref, zero_ref, inv_ref, sem):
    fill = pltpu.make_async_copy(zero_ref, inv_ref, sem)
    fill.start()
    fill.wait()

    def put(tok, carry):
        inv_ref[dest_ref[2 * tok]] = tok
        inv_ref[dest_ref[2 * tok + 1]] = tok
        return carry

    lax.fori_loop(0, dest_ref.shape[0] // 2, put, 0, unroll=8)


def _slots_call(dest_flat, n_rows):
    return pl.pallas_call(
        _slots_kernel,
        in_specs=[pl.BlockSpec(memory_space=pltpu.SMEM), pl.BlockSpec(memory_space=pl.ANY)],
        out_specs=pl.BlockSpec(memory_space=pltpu.SMEM),
        out_shape=jax.ShapeDtypeStruct((n_rows,), jnp.int32),
        scratch_shapes=[pltpu.SemaphoreType.DMA(())],
        name="slots",
    )(dest_flat, jnp.zeros((n_rows,), jnp.int32))


GMM_GROUP = 6
GMM_TILES = 4


def _unit_tables(block_expert, used, n_blocks):
    n_units_max = N_EXPERTS + n_blocks // GMM_GROUP
    valid = jnp.arange(n_blocks) < used
    nblk_e = jnp.sum((block_expert[None, :] == jnp.arange(N_EXPERTS)[:, None]) & valid[None, :],
                     axis=1).astype(jnp.int32)
    first_e = jnp.cumsum(nblk_e) - nblk_e
    units_e = (nblk_e + GMM_GROUP - 1) // GMM_GROUP
    uend = jnp.cumsum(units_e)
    n_units = uend[-1]
    u = jnp.arange(n_units_max, dtype=jnp.int32)
    ue = jnp.minimum(jnp.sum(uend[None, :] <= u[:, None], axis=1), N_EXPERTS - 1).astype(jnp.int32)
    k = u - (uend - units_e)[ue]
    live = u < n_units
    ub = jnp.where(live, first_e[ue] + k * GMM_GROUP, 0).astype(jnp.int32)
    un = jnp.where(live, jnp.clip(nblk_e[ue] - k * GMM_GROUP, 0, GMM_GROUP), 0).astype(jnp.int32)
    return ue, ub, un, n_units.reshape(1).astype(jnp.int32)


def _gmm_kernel(ue_ref, ub_ref, un_ref, nu_ref, used_ref, inv_ref, h_ref, w1_ref, w3_ref, w2_ref, y_ref,
                xbuf, acc, w1b_s, w3b_s, w2b_s, gsem, osem, *, n_blocks):
    del ue_ref
    u = pl.program_id(0)
    jt = pl.program_id(1)
    n_units = nu_ref[0]
    slot = u % 2
    last_tile = pl.num_programs(1) - 1

    def blk_rows(i):
        return pl.ds(pl.multiple_of(i * MOE_BLOCK, MOE_BLOCK), MOE_BLOCK)

    def gather(unit, s):
        base = ub_ref[unit] * MOE_BLOCK

        def body(r, carry):
            pltpu.make_async_copy(h_ref.at[pl.ds(inv_ref[base + r], 1)], xbuf.at[s, pl.ds(r, 1)],
                                  gsem.at[s]).start()
            return carry

        lax.fori_loop(0, un_ref[unit] * MOE_BLOCK, body, 0)

    def wait_gather(unit, s):
        rows = pl.ds(0, un_ref[unit] * MOE_BLOCK)
        pltpu.make_async_copy(xbuf.at[s, rows], xbuf.at[s, rows], gsem.at[s]).wait()

    def out_copy(unit, s, i):
        dst = pl.ds(pl.multiple_of((ub_ref[unit] + i) * MOE_BLOCK, MOE_BLOCK), MOE_BLOCK)
        return pltpu.make_async_copy(acc.at[s, blk_rows(i)], y_ref.at[dst], osem.at[s])

    def wait_out(unit, s):
        def body(i, carry):
            out_copy(unit, s, i).wait()
            return carry

        lax.fori_loop(0, un_ref[unit], body, 0)

    @pl.when((u == 0) & (jt == 0))
    def _():
        gather(0, 0)

    @pl.when((jt == 0) & (u < n_units))
    def _():
        @pl.when(u >= 2)
        def _():
            wait_out(u - 2, slot)

        @pl.when(u + 1 < n_units)
        def _():
            gather(u + 1, 1 - slot)

        wait_gather(u, slot)

    def tile_pass(first):
        w1b = w1_ref[0].astype(BF16)
        w3b = w3_ref[0].astype(BF16)
        w2b = w2_ref[0].astype(BF16)
        w1b_s[...] = w1b
        w3b_s[...] = w3b
        w2b_s[...] = w2b

        def rows_pass(r0, m, a1, a3, a2):
            rows = pl.ds(r0, m)
            xb = xbuf[slot, rows, :].astype(BF16)
            h1 = jnp.dot(xb, a1, preferred_element_type=F32)
            h3 = jnp.dot(xb, a3, preferred_element_type=F32)
            part = jnp.dot((_silu(h1) * h3).astype(BF16), a2, preferred_element_type=F32)
            if first:
                acc[slot, rows, :] = part
            else:
                acc[slot, rows, :] += part

        rows_pass(0, MOE_BLOCK, w1b, w3b, w2b)
        rest = un_ref[u] - 1

        def pair(i, carry):
            r0 = pl.multiple_of(MOE_BLOCK + i * 2 * MOE_BLOCK, MOE_BLOCK)
            rows_pass(r0, 2 * MOE_BLOCK, w1b_s[...], w3b_s[...], w2b_s[...])
            return carry

        lax.fori_loop(0, rest // 2, pair, 0)

        @pl.when(rest % 2 == 1)
        def _():
            rows_pass(pl.multiple_of(rest * MOE_BLOCK, MOE_BLOCK), MOE_BLOCK,
                      w1b_s[...], w3b_s[...], w2b_s[...])

    @pl.when((u < n_units) & (jt == 0))
    def _():
        tile_pass(True)

    @pl.when((u < n_units) & (jt > 0))
    def _():
        tile_pass(False)

    @pl.when((u < n_units) & (jt == last_tile))
    def _():
        def start(i, carry):
            out_copy(u, slot, i).start()
            return carry

        lax.fori_loop(0, un_ref[u], start, 0)

    @pl.when((u == pl.num_programs(0) - 1) & (jt == last_tile))
    def _():
        last = n_units - 1
        wait_out(last, last % 2)

        @pl.when(n_units >= 2)
        def _():
            wait_out(last - 1, (last - 1) % 2)

        acc[0, blk_rows(0), :] = jnp.zeros((MOE_BLOCK, acc.shape[2]), F32)

        def zero_copy(b):
            dst = pl.ds(pl.multiple_of(b * MOE_BLOCK, MOE_BLOCK), MOE_BLOCK)
            return pltpu.make_async_copy(acc.at[0, blk_rows(0)], y_ref.at[dst], osem.at[0])

        def start(b, carry):
            zero_copy(b).start()
            return carry

        def wait(b, carry):
            zero_copy(b).wait()
            return carry

        lax.fori_loop(used_ref[0], n_blocks, start, 0)
        lax.fori_loop(used_ref[0], n_blocks, wait, 0)


def _gmm_call(block_expert, used, inv, h2, w1, w3, w2):
    n_rows = inv.shape[0]
    d = h2.shape[1]
    de = w1.shape[-1]
    n_blocks = n_rows // MOE_BLOCK
    tj = de // GMM_TILES
    ue, ub, un, nu = _unit_tables(block_expert, used[0], n_blocks)

    def live(u, jt, nu):
        ok = u < nu[0]
        return jnp.where(ok, u, nu[0] - 1), jnp.where(ok, jt, GMM_TILES - 1)

    def w13_map(u, jt, ue, ub, un, nu, used, inv):
        uu, jj = live(u, jt, nu)
        return (ue[uu], 0, jj)

    def w2_map(u, jt, ue, ub, un, nu, used, inv):
        uu, jj = live(u, jt, nu)
        return (ue[uu], jj, 0)

    rows = GMM_GROUP * MOE_BLOCK
    grid_spec = pltpu.PrefetchScalarGridSpec(
        num_scalar_prefetch=6,
        grid=(ue.shape[0], GMM_TILES),
        in_specs=[pl.BlockSpec(memory_space=pl.ANY),
                  pl.BlockSpec((1, d, tj), w13_map),
                  pl.BlockSpec((1, d, tj), w13_map),
                  pl.BlockSpec((1, tj, d), w2_map)],
        out_specs=pl.BlockSpec(memory_space=pl.ANY),
        scratch_shapes=[pltpu.VMEM((2, rows, d), F32), pltpu.VMEM((2, rows, d), F32),
                        pltpu.VMEM((d, tj), BF16), pltpu.VMEM((d, tj), BF16), pltpu.VMEM((tj, d), BF16),
                        pltpu.SemaphoreType.DMA((2,)), pltpu.SemaphoreType.DMA((2,))],
    )
    return pl.pallas_call(
        functools.partial(_gmm_kernel, n_blocks=n_blocks),
        grid_spec=grid_spec,
        out_shape=jax.ShapeDtypeStruct((n_rows, d), F32),
        compiler_params=_cparams(("arbitrary", "arbitrary")),
        name="gmm",
    )(ue, ub, un, nu, used, inv, h2, w1, w3, w2)


def _combine_kernel(dest_ref, y_ref, x1_ref, gate_ref, mods_ref, fg_ref, o_ref, ybuf, sem, *, seq):
    i = pl.program_id(0)
    tm = x1_ref.shape[0]
    slot = i % 2

    def gather(step, s):
        tok0 = step * tm

        def body(r, carry):
            for k in range(2):
                pltpu.make_async_copy(y_ref.at[pl.ds(dest_ref[2 * (tok0 + r) + k], 1)],
                                      ybuf.at[s, k, pl.ds(r, 1)], sem.at[s]).start()
            return carry

        lax.fori_loop(0, tm, body, 0, unroll=4)

    @pl.when(i == 0)
    def _():
        gather(0, 0)

    @pl.when(i + 1 < pl.num_programs(0))
    def _():
        gather(i + 1, 1 - slot)

    pltpu.make_async_copy(ybuf.at[slot], ybuf.at[slot], sem.at[slot]).wait()

    b = (i * tm) // seq
    gt2 = mods_ref[5, pl.ds(b, 1), :]
    gate = gate_ref[...]
    moe = gate[:, 0:1] * ybuf[slot, 0] + gate[:, 1:2] * ybuf[slot, 1]
    x = x1_ref[...] + gt2 * moe
    o_ref[...] = x * lax.rsqrt(jnp.mean(x * x, axis=-1, keepdims=True) + EPS) * fg_ref[...]


def _combine_call(dest_flat, y_pad, x1, gate, mods, final_g, seq):
    n, d = x1.shape
    tm = min(seq, 256)
    grid_spec = pltpu.PrefetchScalarGridSpec(
        num_scalar_prefetch=1,
        grid=(n // tm,),
        in_specs=[pl.BlockSpec(memory_space=pl.ANY),
                  pl.BlockSpec((tm, d), lambda i, dest: (i, 0)),
                  pl.BlockSpec((tm, LANES), lambda i, dest: (i, 0)),
                  pl.BlockSpec((6, 8, d), lambda i, dest: (0, 0, 0)),
                  pl.BlockSpec((1, d), lambda i, dest: (0, 0))],
        out_specs=pl.BlockSpec((tm, d), lambda i, dest: (i, 0)),
        scratch_shapes=[pltpu.VMEM((2, 2, tm, d), F32), pltpu.SemaphoreType.DMA((2,))],
    )
    return pl.pallas_call(
        functools.partial(_combine_kernel, seq=seq),
        grid_spec=grid_spec,
        out_shape=jax.ShapeDtypeStruct((n, d), F32),
        compiler_params=_cparams(("arbitrary",)),
        name="combine",
    )(dest_flat, y_pad, x1, gate, mods, final_g)


def _pad_lanes(a, lane0):
    return jnp.zeros((LANES,), F32).at[lane0:lane0 + a.shape[0]].set(a.astype(F32))


def kernel(x, c, ctx, c_ctx, w_mod, b_mod, norm1_g, w_in, pool_w, pool_scale, conv_w,
           a_log_f, dt_bias_f, a_log_b, dt_bias_b, out_norm_g, w_out, norm2_g,
           w_grp, b_grp, w_rt, b_rt, w1, w3, w2, final_g):
    bsz, t, d = x.shape
    depth = w_mod.shape[0]
    assert depth == 1, "single-layer problem: the context stream is read but never updated"
    heads = a_log_f.shape[1]
    pool_width = pool_w.shape[1] * pool_w.shape[2]
    dn_width = heads * HEAD_DIM
    q0 = pool_width
    z0 = q0 + 3 * dn_width
    ab0 = z0 + dn_width
    n_tok = bsz * t
    l = 0

    c8 = jnp.zeros((8, d), F32).at[:bsz].set(c).at[bsz].set(c_ctx)
    mod = _mod_call(c8, w_mod[l], b_mod[l])
    mods = mod.reshape(8, 6, d).transpose(1, 0, 2)

    w_in_bf = w_in[l].astype(BF16)
    wab_bf = jnp.zeros((d, LANES), BF16).at[:, :4 * heads].set(w_in_bf[:, ab0:])
    g1 = norm1_g[l].reshape(1, d)
    proj, ab = _inproj_call(x, mods, g1, w_in_bf, wab_bf, ab0, None)
    proj_c, ab_c = _inproj_call(ctx, mods, g1, w_in_bf, wab_bf, ab0, bsz)

    prm = jnp.zeros((8, LANES), F32)
    prm = prm.at[0].set(_pad_lanes(jnp.concatenate([a_log_f[l], a_log_b[l]]), 2 * heads))
    prm = prm.at[1].set(_pad_lanes(jnp.concatenate([dt_bias_f[l], dt_bias_b[l]]), 2 * heads))
    g, g_t = _gates_call(ab, prm, heads)
    gc, gc_t = _gates_call(ab_c, prm, heads)
    gr = g_t.reshape(bsz, LANES, t // CHUNK, CHUNK)
    grc = gc_t.reshape(bsz, LANES, ctx.shape[1] // CHUNK, CHUNK)

    dn = _delta_call(proj, proj_c, conv_w[l], g, gc, gr, grc, out_norm_g[l].reshape(1, HEAD_DIM),
                     heads, q0 // HEAD_DIM, z0 // HEAD_DIM)
    pool = _pool_call(proj, pool_w[l], pool_scale[l].reshape(1, pool_width), t // GRID_W, GRID_W)

    wr = jnp.zeros((d, LANES), F32).at[:, :N_GROUPS].set(w_grp[l]).at[:, EXP_LANE0:EXP_LANE0 + N_EXPERTS].set(w_rt[l])
    br = jnp.zeros((LANES,), F32).at[:N_GROUPS].set(b_grp[l]).at[EXP_LANE0:EXP_LANE0 + N_EXPERTS].set(b_rt[l])
    x1, h2, logits = _outproj_call(pool, dn, w_out[l].astype(BF16), x, mods, norm2_g[l].reshape(1, d),
                                   wr, br.reshape(1, LANES))

    n_blocks = (n_tok * 2 + N_EXPERTS * (MOE_BLOCK - 1) + MOE_BLOCK - 1) // MOE_BLOCK
    dest, gate, be = _router_call(logits.reshape(n_tok, LANES), n_blocks)
    dest_flat = dest[:, :2].reshape(-1)
    block_expert = be[:n_blocks, 0]
    used = be[0:1, 1]
    inv = _slots_call(dest_flat, n_blocks * MOE_BLOCK)
    y_pad = _gmm_call(block_expert, used, inv, h2.reshape(n_tok, d), w1[l], w3[l], w2[l])
    out = _combine_call(dest_flat, y_pad, x1.reshape(n_tok, d), gate, mods, final_g.reshape(1, d), t)
    return out.reshape(bsz, t, d)
```

```python
import functools

import jax
import jax.numpy as jnp
from jax import lax
from jax.experimental import pallas as pl
from jax.experimental.pallas import tpu as pltpu

F32 = jnp.float32
BF16 = jnp.bfloat16

GRID_W = 64
POOL_WINDOWS = (2, 4, 8, 16)
HEAD_DIM = 128
CONV_WIDTH = 5
CHUNK = 64
N_GROUPS = 4
EXPERTS_PER_GROUP = 8
N_EXPERTS = N_GROUPS * EXPERTS_PER_GROUP
MOE_BLOCK = 128
EPS = 1e-6
LANES = 128
PRE_UNROLL = 8
EXP_LANE0 = N_GROUPS

VMEM_LIMIT = 56 * 1024 * 1024


def _cparams(sem):
    return pltpu.CompilerParams(dimension_semantics=sem, vmem_limit_bytes=VMEM_LIMIT)


def _dot(a, b):
    return jnp.dot(a.astype(BF16), b.astype(BF16), preferred_element_type=F32)


def _dot_split(a, b):
    a_hi = a.astype(BF16)
    b_hi = b.astype(BF16)
    a_lo = (a - a_hi.astype(F32)).astype(BF16)
    b_lo = (b - b_hi.astype(F32)).astype(BF16)
    return (jnp.dot(a_hi, b_hi, preferred_element_type=F32)
            + jnp.dot(a_lo, b_hi, preferred_element_type=F32)
            + jnp.dot(a_hi, b_lo, preferred_element_type=F32))


def _silu(x):
    return x * jax.nn.sigmoid(x)


def _mod_kernel(c_ref, w_ref, b_ref, o_ref):
    o_ref[...] = _dot(_silu(c_ref[...]), w_ref[...]) + b_ref[...]


def _mod_call(c8, w_mod, b_mod):
    d, n = w_mod.shape
    tn = 512
    return pl.pallas_call(
        _mod_kernel,
        grid=(n // tn,),
        in_specs=[pl.BlockSpec((8, d), lambda j: (0, 0)),
                  pl.BlockSpec((d, tn), lambda j: (0, j)),
                  pl.BlockSpec((1, tn), lambda j: (0, j))],
        out_specs=pl.BlockSpec((8, tn), lambda j: (0, j)),
        out_shape=jax.ShapeDtypeStruct((8, n), F32),
        compiler_params=_cparams(("parallel",)),
        name="mod",
    )(c8, w_mod, b_mod.reshape(1, n))


def _inproj_kernel(x_ref, mods_ref, g_ref, w_ref, wab_ref, o_ref, ab_ref, hn_ref, *, mod_row):
    b = pl.program_id(0)
    j = pl.program_id(2)

    @pl.when(j == 0)
    def _():
        x = x_ref[0]
        y = x * lax.rsqrt(jnp.mean(x * x, axis=-1, keepdims=True) + EPS) * g_ref[...]
        row = b if mod_row is None else mod_row
        sh = mods_ref[0, pl.ds(row, 1), :]
        sc = mods_ref[1, pl.ds(row, 1), :]
        hb = (y * (1 + sc) + sh).astype(BF16)
        hn_ref[...] = hb
        ab_ref[0] = jnp.dot(hb, wab_ref[...], preferred_element_type=F32)

    o_ref[0] = jnp.dot(hn_ref[...], w_ref[...].astype(BF16), preferred_element_type=F32)


def _inproj_call(x, mods, g, w_in, wab_bf, n_main, mod_row):
    bsz, t, d = x.shape
    tm = min(t, 1024)
    tn = 1024
    kern = functools.partial(_inproj_kernel, mod_row=mod_row)
    return pl.pallas_call(
        kern,
        grid=(bsz, t // tm, n_main // tn),
        in_specs=[pl.BlockSpec((1, tm, d), lambda b, i, j: (b, i, 0)),
                  pl.BlockSpec((2, 8, d), lambda b, i, j: (0, 0, 0)),
                  pl.BlockSpec((1, d), lambda b, i, j: (0, 0)),
                  pl.BlockSpec((d, tn), lambda b, i, j: (0, j)),
                  pl.BlockSpec((d, LANES), lambda b, i, j: (0, 0))],
        out_specs=[pl.BlockSpec((1, tm, tn), lambda b, i, j: (b, i, j)),
                   pl.BlockSpec((1, tm, LANES), lambda b, i, j: (b, i, 0))],
        out_shape=[jax.ShapeDtypeStruct((bsz, t, n_main), F32),
                   jax.ShapeDtypeStruct((bsz, t, LANES), F32)],
        scratch_shapes=[pltpu.VMEM((tm, d), BF16)],
        compiler_params=_cparams(("parallel", "parallel", "arbitrary")),
        name="inproj",
    )(x, mods, g, w_in, wab_bf)


def _gates_kernel(ab_ref, prm_ref, g_ref, gt_ref, *, heads):
    ab = ab_ref[0]
    t = ab.shape[0]
    h2, h3, h4, h6 = 2 * heads, 3 * heads, 4 * heads, 6 * heads
    beta = jax.nn.sigmoid(ab)
    xx = ab + prm_ref[1:2, :]
    softplus = jnp.maximum(xx, 0.0) + jnp.log1p(jnp.exp(-jnp.abs(xx)))
    g = -jnp.exp(prm_ref[0:1, :]) * softplus
    pos = lax.broadcasted_iota(jnp.int32, ab.shape, 0) & (CHUNK - 1)
    cs = g
    ss = g
    s = 1
    while s < CHUNK:
        cs = cs + jnp.where(pos >= s, pltpu.roll(cs, s, 0), 0.0)
        ss = ss + jnp.where(pos < CHUNK - s, pltpu.roll(ss, t - s, 0), 0.0)
        s *= 2
    tot = pltpu.roll(cs + ss - g, h2, 1)
    lane = lax.broadcasted_iota(jnp.int32, ab.shape, 1)
    out = jnp.where(lane < h2, beta,
                    jnp.where(lane < h3, cs,
                              jnp.where(lane < h4, ss,
                                        jnp.where(lane < h6, tot, 0.0))))
    g_ref[0] = out
    gt_ref[0] = out.T


def _gates_call(ab, prm, heads):
    bsz, t, _ = ab.shape
    return pl.pallas_call(
        functools.partial(_gates_kernel, heads=heads),
        grid=(bsz,),
        in_specs=[pl.BlockSpec((1, t, LANES), lambda b: (b, 0, 0)),
                  pl.BlockSpec((8, LANES), lambda b: (0, 0))],
        out_specs=[pl.BlockSpec((1, t, LANES), lambda b: (b, 0, 0)),
                   pl.BlockSpec((1, LANES, t), lambda b: (b, 0, 0))],
        out_shape=[jax.ShapeDtypeStruct((bsz, t, LANES), F32),
                   jax.ShapeDtypeStruct((bsz, LANES, t), F32)],
        compiler_params=_cparams(("parallel",)),
        name="gates",
    )(ab, prm)


def _conv_silu(x, w):
    n = x.shape[0]
    row = lax.broadcasted_iota(jnp.int32, x.shape, 0)
    acc = x * w[CONV_WIDTH // 2:CONV_WIDTH // 2 + 1, :]
    for j in range(CONV_WIDTH):
        d = j - CONV_WIDTH // 2
        if d == 0:
            continue
        xs = pltpu.roll(x, (-d) % n, 0)
        valid = (row + d >= 0) & (row + d < n)
        acc = acc + jnp.where(valid, xs, 0.0) * w[j:j + 1, :]
    return _silu(acc)


def _l2norm(a):
    return a * lax.rsqrt(jnp.sum(a * a, axis=-1, keepdims=True) + EPS)


def _lane_col(g, lane_idx):
    lane = lax.broadcasted_iota(jnp.int32, g.shape, 1)
    return jnp.sum(jnp.where(lane == lane_idx, g, 0.0), axis=-1, keepdims=True)


def _chunk_terms(chains, between=()):
    pending = list(between)

    def stage_done():
        if pending:
            pending.pop(0)()

    c, hd = chains[0][0].shape
    ri = lax.broadcasted_iota(jnp.int32, (c, c), 0)
    ci = lax.broadcasted_iota(jnp.int32, (c, c), 1)
    eye = jnp.where(ri == ci, 1.0, 0.0)
    right = lax.broadcasted_iota(jnp.int32, (c, 2 * c), 1) >= c
    nt = (((1,), (1,)), ((), ()))
    tn = (((0,), (0,)), ((), ()))

    decs, kn_bs, zs = [], [], []
    for kn_c, kb_c, _, _, _, _, gcc, gcr, upper in chains:
        incl = (ri <= ci) if upper else (ri >= ci)
        strict = (ri < ci) if upper else (ri > ci)
        dec = jnp.where(incl, jnp.exp(jnp.where(incl, gcc - gcr, 0.0)), 0.0)
        kn_b = kn_c.astype(BF16)
        kk = lax.dot_general(kb_c.astype(BF16), kn_b, nt, preferred_element_type=F32)
        decs.append(dec)
        kn_bs.append(kn_b)
        zs.append(jnp.concatenate([-jnp.where(strict, kk * dec, 0.0), eye], axis=1))
    stage_done()
    n = 1
    while n < c:
        zs = [_dot(z[:, :c], z) + jnp.where(right, z, 0.0) for z in zs]
        stage_done()
        n *= 2
    uw_bs = [_dot(z[:, c:], ch[3]).astype(BF16) for z, ch in zip(zs, chains)]
    stage_done()
    wns = [lax.dot_general(ch[4].astype(BF16), uw_b, tn, preferred_element_type=F32)
           for uw_b, ch in zip(uw_bs, chains)]
    stage_done()
    qks = [None if ch[2] is None else
           lax.dot_general(ch[2].astype(BF16), kn_b, nt, preferred_element_type=F32) * dec
           for ch, kn_b, dec in zip(chains, kn_bs, decs)]
    stage_done()
    qws = [None if qk is None else jnp.dot(qk.astype(BF16), uw_b, preferred_element_type=F32)
           for qk, uw_b in zip(qks, uw_bs)]
    while pending:
        stage_done()
    out = []
    for ch, wn, qw in zip(chains, wns, qws):
        nc, w2 = wn[:, :hd], wn[:, hd:]
        if qw is None:
            out.append((w2, nc, None, None))
        else:
            out.append((w2, nc, ch[5] - qw[:, hd:], qw[:, :hd]))
    return out


def _delta_kernel(q_ref, k_ref, v_ref, z_ref, kc_ref, vc_ref, cwq_ref, cwk_ref, cwv_ref,
                  g_ref, gc_ref, grf_ref, grb_ref, grcf_ref, grcb_ref, ong_ref,
                  out_ref,
                  qn_s, kn_s, kb_s, rhs_s, kd_s, qd_s, col_s,
                  knc_s, kbc_s, rhsc_s, kdc_s, colc_s,
                  w2_s, nc_s, qp_s, o0_s, w2c_s, ncc_s, o_s, *, heads):
    h = pl.program_id(1)
    t = q_ref.shape[1]
    tc = kc_ref.shape[1]
    hd = HEAD_DIM
    n_lat = t // CHUNK
    n_ctx = tc // CHUNK

    qn = _l2norm(_conv_silu(q_ref[0], cwq_ref[...])) * (hd ** -0.5)
    kn = _l2norm(_conv_silu(k_ref[0], cwk_ref[...]))
    vv = _conv_silu(v_ref[0], cwv_ref[...])
    knc = _l2norm(_conv_silu(kc_ref[0], cwk_ref[...]))
    vvc = _conv_silu(vc_ref[0], cwv_ref[...])
    qn_s[...] = qn
    kn_s[...] = kn
    knc_s[...] = knc
    g_lat = g_ref[0]
    g_ctx = gc_ref[0]

    for d in range(2):
        for (gt_, kn_, vv_, kb_r, rhs_r, kd_r, col_r, qn_, qd_r) in (
                (g_lat, kn, vv, kb_s, rhs_s, kd_s, col_s, qn, qd_s),
                (g_ctx, knc, vvc, kbc_s, rhsc_s, kdc_s, colc_s, None, None)):
            beta = _lane_col(gt_, d * heads + h)
            gcum = _lane_col(gt_, (2 + d) * heads + h)
            gtot = _lane_col(gt_, (4 + d) * heads + h)
            e = jnp.exp(gcum)
            kb = kn_ * beta
            kb_r[d] = kb
            rhs_r[d, :, 0:hd] = vv_ * beta
            rhs_r[d, :, hd:2 * hd] = kb * e
            kd_r[d] = kn_ * jnp.exp(gtot - gcum)
            lane = lax.broadcasted_iota(jnp.int32, (gcum.shape[0], LANES), 1)
            col_r[d] = jnp.where(lane == 0, gcum, jnp.where(lane == 1, jnp.exp(gtot), 0.0))
            if qn_ is not None:
                qd_r[d] = qn_ * e

    gr_lat = (grf_ref, grb_ref)
    gr_ctx = (grcf_ref, grcb_ref)

    un_ctx = min(PRE_UNROLL, n_ctx)
    un_lat = min(PRE_UNROLL, n_lat)

    def pre_ctx(i, carry):
        ids = [(i * un_ctx + u, d) for u in range(un_ctx) for d in range(2)]
        chains = []
        for c, d in ids:
            rows = pl.ds(pl.multiple_of(c * CHUNK, CHUNK), CHUNK)
            chains.append((knc_s[rows, :], kbc_s[d, rows, :], None, rhsc_s[d, rows, :], kdc_s[d, rows, :],
                           None, colc_s[d, rows, 0:1], gr_ctx[d][0, 0, pl.ds(c, 1), :], d == 1))
        for (c, d), (w2, nc, _, _) in zip(ids, _chunk_terms(chains)):
            m0 = pl.multiple_of(c * hd, hd)
            w2c_s[d, pl.ds(m0, hd), :] = w2
            ncc_s[d, pl.ds(m0, hd), :] = nc
        return carry

    lax.fori_loop(0, n_ctx // un_ctx, pre_ctx, 0)

    def lat_chunk(i, u, d):
        k = i * un_lat + u
        return k if d == 0 else n_lat - 1 - k

    def pre_lat(i, between=()):
        ids = [(lat_chunk(i, u, d), d) for u in range(un_lat) for d in range(2)]
        chains = []
        for c, d in ids:
            rows = pl.ds(pl.multiple_of(c * CHUNK, CHUNK), CHUNK)
            chains.append((kn_s[rows, :], kb_s[d, rows, :], qn_s[rows, :], rhs_s[d, rows, :], kd_s[d, rows, :],
                           qd_s[d, rows, :], col_s[d, rows, 0:1], gr_lat[d][0, 0, pl.ds(c, 1), :], d == 1))
        for (c, d), (w2, nc, qp, o0) in zip(ids, _chunk_terms(chains, between)):
            rows = pl.ds(pl.multiple_of(c * CHUNK, CHUNK), CHUNK)
            m0 = pl.multiple_of(c * hd, hd)
            w2_s[d, pl.ds(m0, hd), :] = w2
            nc_s[d, pl.ds(m0, hd), :] = nc
            qp_s[d, rows, :] = qp
            o0_s[d, rows, :] = o0

    def scan_ctx(i, states):
        new = []
        for d in range(2):
            s = states[d]
            c = i if d == 0 else n_ctx - 1 - i
            m0 = pl.multiple_of(c * hd, hd)
            gt = colc_s[d, pl.ds(pl.multiple_of(c * CHUNK, CHUNK), 1), 1:2]
            s = gt * s + ncc_s[d, pl.ds(m0, hd), :] - _dot(w2c_s[d, pl.ds(m0, hd), :], s)
            new.append(s)
        return tuple(new)

    zero = jnp.zeros((hd, hd), F32)
    states = lax.fori_loop(0, n_ctx, scan_ctx, (zero, zero))

    def scan_lat_steps(i, box):
        def step(u):
            def run():
                for d in range(2):
                    s = box[d]
                    c = lat_chunk(i, u, d)
                    r0 = pl.multiple_of(c * CHUNK, CHUNK)
                    rows = pl.ds(r0, CHUNK)
                    m0 = pl.multiple_of(c * hd, hd)
                    s_b = s.astype(BF16)
                    o_s[d, rows, :] = jnp.dot(qp_s[d, rows, :].astype(BF16), s_b,
                                              preferred_element_type=F32) + o0_s[d, rows, :]
                    gt = col_s[d, pl.ds(r0, 1), 1:2]
                    box[d] = gt * s + nc_s[d, pl.ds(m0, hd), :] - jnp.dot(
                        w2_s[d, pl.ds(m0, hd), :].astype(BF16), s_b, preferred_element_type=F32)
            return run

        return [step(u) for u in range(un_lat)]

    pre_lat(0)

    def lat_body(i, states):
        box = list(states)
        pre_lat(i, scan_lat_steps(i - 1, box))
        return tuple(box)

    states = lax.fori_loop(1, n_lat // un_lat, lat_body, states)
    box = list(states)
    for run in scan_lat_steps(n_lat // un_lat - 1, box):
        run()

    o = o_s[0] + o_s[1]
    o = o * lax.rsqrt(jnp.mean(o * o, axis=-1, keepdims=True) + EPS) * ong_ref[...]
    out_ref[0] = (o * _silu(z_ref[0])).astype(out_ref.dtype)


def _delta_call(proj, proj_c, conv_w, g, gc, gr, grc, ong, heads, q_blk0, z_blk0):
    bsz, t, _ = proj.shape
    tc = proj_c.shape[1]
    hd = HEAD_DIM
    n_lat, n_ctx = t // CHUNK, tc // CHUNK

    def col(off):
        return lambda b, h: (b, 0, off + h)

    def cw(off):
        return lambda b, h: (0, off + h)

    def grow(off):
        return lambda b, h: (b, off + h, 0, 0)

    in_specs = [
        pl.BlockSpec((1, t, hd), col(q_blk0)),
        pl.BlockSpec((1, t, hd), col(q_blk0 + heads)),
        pl.BlockSpec((1, t, hd), col(q_blk0 + 2 * heads)),
        pl.BlockSpec((1, t, hd), col(z_blk0)),
        pl.BlockSpec((1, tc, hd), col(q_blk0 + heads)),
        pl.BlockSpec((1, tc, hd), col(q_blk0 + 2 * heads)),
        pl.BlockSpec((CONV_WIDTH, hd), cw(0)),
        pl.BlockSpec((CONV_WIDTH, hd), cw(heads)),
        pl.BlockSpec((CONV_WIDTH, hd), cw(2 * heads)),
        pl.BlockSpec((1, t, LANES), lambda b, h: (b, 0, 0)),
        pl.BlockSpec((1, tc, LANES), lambda b, h: (b, 0, 0)),
        pl.BlockSpec((1, 1, n_lat, CHUNK), grow(2 * heads)),
        pl.BlockSpec((1, 1, n_lat, CHUNK), grow(3 * heads)),
        pl.BlockSpec((1, 1, n_ctx, CHUNK), grow(2 * heads)),
        pl.BlockSpec((1, 1, n_ctx, CHUNK), grow(3 * heads)),
        pl.BlockSpec((1, hd), lambda b, h: (0, 0)),
    ]
    scratch = [
        pltpu.VMEM((t, hd), F32), pltpu.VMEM((t, hd), F32),
        pltpu.VMEM((2, t, hd), F32), pltpu.VMEM((2, t, 2 * hd), F32),
        pltpu.VMEM((2, t, hd), F32), pltpu.VMEM((2, t, hd), F32),
        pltpu.VMEM((2, t, LANES), F32),
        pltpu.VMEM((tc, hd), F32), pltpu.VMEM((2, tc, hd), F32),
        pltpu.VMEM((2, tc, 2 * hd), F32), pltpu.VMEM((2, tc, hd), F32),
        pltpu.VMEM((2, tc, LANES), F32),
        pltpu.VMEM((2, n_lat * hd, hd), F32), pltpu.VMEM((2, n_lat * hd, hd), F32),
        pltpu.VMEM((2, t, hd), F32), pltpu.VMEM((2, t, hd), F32),
        pltpu.VMEM((2, n_ctx * hd, hd), F32), pltpu.VMEM((2, n_ctx * hd, hd), F32),
        pltpu.VMEM((2, t, hd), F32),
    ]
    return pl.pallas_call(
        functools.partial(_delta_kernel, heads=heads),
        grid=(bsz, heads),
        in_specs=in_specs,
        out_specs=pl.BlockSpec((1, t, hd), lambda b, h: (b, 0, h)),
        out_shape=jax.ShapeDtypeStruct((bsz, t, heads * hd), BF16),
        scratch_shapes=scratch,
        compiler_params=_cparams(("parallel", "parallel")),
        name="delta",
    )(proj, proj, proj, proj, proj_c, proj_c, conv_w, conv_w, conv_w, g, gc, gr, gr, grc, grc, ong)


def _shift_rows(x, d, idx, size, stride):
    n = x.shape[0]
    xs = pltpu.roll(x, (-d * stride) % n, 0)
    return jnp.where((idx + d >= 0) & (idx + d < size), xs, 0.0)


def _box_sum_1d(x, win, idx, size, stride):
    m = win // 2
    lead = x
    trail = x
    k = 1
    while k < m:
        lead = lead + _shift_rows(lead, k, idx, size, stride)
        trail = trail + _shift_rows(trail, -k, idx, size, stride)
        k *= 2
    return lead + _shift_rows(trail, -1, idx, size, stride)


def _pool_kernel(u_ref, pw_ref, ps_ref, o_ref, *, rows, cols):
    t = u_ref.shape[1]
    gc = pw_ref.shape[1]
    tok = lax.broadcasted_iota(jnp.int32, (t, gc), 0)
    ci = tok % cols
    ri = tok // cols
    for gi, win in enumerate(POOL_WINDOWS):
        lo = win // 2
        hi = win - lo
        u = u_ref[0, :, gi * gc:(gi + 1) * gc]
        s = _box_sum_1d(u, win, ci, cols, 1)
        s = _box_sum_1d(s, win, ri, rows, cols)
        cnt_c = jnp.minimum(ci + hi, cols) - jnp.maximum(ci - lo, 0)
        cnt_r = jnp.minimum(ri + hi, rows) - jnp.maximum(ri - lo, 0)
        mean = s / (cnt_c * cnt_r).astype(F32)
        y = _dot(mean - u, pw_ref[gi]) * ps_ref[:, gi * gc:(gi + 1) * gc]
        o_ref[0, :, gi * gc:(gi + 1) * gc] = y.astype(o_ref.dtype)


def _pool_call(proj, pool_w, pool_scale, rows, cols):
    bsz, t, _ = proj.shape
    ng, gc, _ = pool_w.shape
    pwid = ng * gc
    return pl.pallas_call(
        functools.partial(_pool_kernel, rows=rows, cols=cols),
        grid=(bsz,),
        in_specs=[pl.BlockSpec((1, t, pwid), lambda b: (b, 0, 0)),
                  pl.BlockSpec((ng, gc, gc), lambda b: (0, 0, 0)),
                  pl.BlockSpec((1, pwid), lambda b: (0, 0))],
        out_specs=pl.BlockSpec((1, t, pwid), lambda b: (b, 0, 0)),
        out_shape=jax.ShapeDtypeStruct((bsz, t, pwid), BF16),
        compiler_params=_cparams(("parallel",)),
        name="pool",
    )(proj, pool_w, pool_scale)


def _outproj_kernel(pool_ref, dn_ref, wa_ref, wb_ref, x_ref, mods_ref, g2_ref, wr_ref, br_ref,
                    x1_ref, h2_ref, lg_ref):
    b = pl.program_id(0)
    mix = (jnp.dot(pool_ref[0], wa_ref[...], preferred_element_type=F32)
           + jnp.dot(dn_ref[0], wb_ref[...], preferred_element_type=F32))
    gt1 = mods_ref[2, pl.ds(b, 1), :]
    sh2 = mods_ref[3, pl.ds(b, 1), :]
    sc2 = mods_ref[4, pl.ds(b, 1), :]
    x1 = x_ref[0] + gt1 * mix
    x1_ref[0] = x1
    y = x1 * lax.rsqrt(jnp.mean(x1 * x1, axis=-1, keepdims=True) + EPS) * g2_ref[...]
    h2 = y * (1 + sc2) + sh2
    h2_ref[0] = h2
    lg_ref[0] = _dot_split(h2, wr_ref[...]) + br_ref[...]


def _outproj_call(pool, dn, w_out_bf, x, mods, g2, wr, br):
    bsz, t, d = x.shape
    half = pool.shape[-1]
    tm = min(t, 256)
    return pl.pallas_call(
        _outproj_kernel,
        grid=(bsz, t // tm),
        in_specs=[pl.BlockSpec((1, tm, half), lambda b, i: (b, i, 0)),
                  pl.BlockSpec((1, tm, half), lambda b, i: (b, i, 0)),
                  pl.BlockSpec((half, d), lambda b, i: (0, 0)),
                  pl.BlockSpec((half, d), lambda b, i: (1, 0)),
                  pl.BlockSpec((1, tm, d), lambda b, i: (b, i, 0)),
                  pl.BlockSpec((6, 8, d), lambda b, i: (0, 0, 0)),
                  pl.BlockSpec((1, d), lambda b, i: (0, 0)),
                  pl.BlockSpec((d, LANES), lambda b, i: (0, 0)),
                  pl.BlockSpec((1, LANES), lambda b, i: (0, 0))],
        out_specs=[pl.BlockSpec((1, tm, d), lambda b, i: (b, i, 0)),
                   pl.BlockSpec((1, tm, d), lambda b, i: (b, i, 0)),
                   pl.BlockSpec((1, tm, LANES), lambda b, i: (b, i, 0))],
        out_shape=[jax.ShapeDtypeStruct((bsz, t, d), F32),
                   jax.ShapeDtypeStruct((bsz, t, d), F32),
                   jax.ShapeDtypeStruct((bsz, t, LANES), F32)],
        compiler_params=_cparams(("parallel", "parallel")),
        name="outproj",
    )(pool, dn, w_out_bf, w_out_bf, x, mods, g2, wr, br)


def _router_kernel(lg_ref, dest_ref, gate_ref, be_ref, cnt_s, run_s, off_s):
    p = pl.program_id(0)
    i = pl.program_id(1)
    lg = lg_ref[...]
    tm = lg.shape[0]
    lane = lax.broadcasted_iota(jnp.int32, lg.shape, 1)
    lane_f = lane.astype(F32)
    neg = -jnp.inf
    big = float(LANES)

    grp = jnp.where(lane < N_GROUPS, lg, neg)
    gmax = jnp.max(grp, axis=-1, keepdims=True)
    gidx = jnp.min(jnp.where(grp == gmax, lane_f, big), axis=-1, keepdims=True)
    p_grp = 1.0 / jnp.sum(jnp.where(lane < N_GROUPS, jnp.exp(lg - gmax), 0.0), axis=-1, keepdims=True)
    lo = EXP_LANE0 + EXPERTS_PER_GROUP * gidx
    ev = jnp.where((lane_f >= lo) & (lane_f < lo + EXPERTS_PER_GROUP), lg, neg)
    t1 = jnp.max(ev, axis=-1, keepdims=True)
    i1 = jnp.min(jnp.where(ev == t1, lane_f, big), axis=-1, keepdims=True)
    ev2 = jnp.where(lane_f == i1, neg, ev)
    t2 = jnp.max(ev2, axis=-1, keepdims=True)
    i2 = jnp.min(jnp.where(ev2 == t2, lane_f, big), axis=-1, keepdims=True)
    oh1 = lane_f == i1
    oh2 = lane_f == i2
    cnt = oh1.astype(F32) + oh2.astype(F32)
    colsum = jnp.sum(cnt, axis=0, keepdims=True)

    @pl.when(p == 0)
    def _():
        @pl.when(i == 0)
        def _():
            cnt_s[...] = jnp.zeros_like(cnt_s)

        cnt_s[...] += colsum

    @pl.when(p == 1)
    def _():
        @pl.when(i == 0)
        def _():
            nblk = jnp.floor((cnt_s[...] + (MOE_BLOCK - 1)) * (1.0 / MOE_BLOCK))
            r = lax.broadcasted_iota(jnp.int32, (LANES, LANES), 0)
            c = lax.broadcasted_iota(jnp.int32, (LANES, LANES), 1)
            tri = (r < c).astype(BF16)
            nb8 = jnp.broadcast_to(nblk, (8, LANES))
            start_blk = jnp.dot(nb8.astype(BF16), tri, preferred_element_type=F32)[0:1, :]
            off_s[...] = start_blk * MOE_BLOCK
            run_s[...] = jnp.zeros_like(run_s)
            end_blk = start_blk + nblk
            nb = be_ref.shape[0]
            blk = lax.broadcasted_iota(jnp.int32, (nb, LANES), 0).astype(F32)
            ln = lax.broadcasted_iota(jnp.int32, (nb, LANES), 1)
            is_exp = (ln >= EXP_LANE0) & (ln < EXP_LANE0 + N_EXPERTS)
            done = jnp.sum(jnp.where(is_exp & (end_blk <= blk), 1.0, 0.0), axis=-1, keepdims=True)
            bexp = jnp.minimum(done, N_EXPERTS - 1.0)
            used = jnp.max(jnp.where(is_exp, end_blk, 0.0), axis=-1, keepdims=True)
            be_ref[...] = jnp.where(ln == 0, bexp, jnp.where(ln == 1, used, 0.0)).astype(jnp.int32)

        rr = lax.broadcasted_iota(jnp.int32, (tm, tm), 0)
        cc = lax.broadcasted_iota(jnp.int32, (tm, tm), 1)
        before = (cc < rr).astype(BF16)
        prefix = jnp.dot(before, cnt.astype(BF16), preferred_element_type=F32)
        base = off_s[...] + run_s[...] + prefix
        d1 = jnp.sum(jnp.where(oh1, base, 0.0), axis=-1, keepdims=True)
        d2 = jnp.sum(jnp.where(oh2, base, 0.0), axis=-1, keepdims=True)
        run_s[...] += colsum
        dd = jnp.exp(t2 - t1)
        g1 = p_grp / (1.0 + dd)
        g2 = p_grp * dd / (1.0 + dd)
        dest_ref[...] = jnp.where(lane == 0, d1, jnp.where(lane == 1, d2, 0.0)).astype(jnp.int32)
        gate_ref[...] = jnp.where(lane == 0, g1, jnp.where(lane == 1, g2, 0.0))


def _router_call(logits, n_blocks):
    n = logits.shape[0]
    tm = min(n, 512)
    nb_pad = ((n_blocks + 7) // 8) * 8
    return pl.pallas_call(
        _router_kernel,
        grid=(2, n // tm),
        in_specs=[pl.BlockSpec((tm, LANES), lambda p, i: (i, 0))],
        out_specs=[pl.BlockSpec((tm, LANES), lambda p, i: (i * p, 0)),
                   pl.BlockSpec((tm, LANES), lambda p, i: (i * p, 0)),
                   pl.BlockSpec((nb_pad, LANES), lambda p, i: (0, 0))],
        out_shape=[jax.ShapeDtypeStruct((n, LANES), jnp.int32),
                   jax.ShapeDtypeStruct((n, LANES), F32),
                   jax.ShapeDtypeStruct((nb_pad, LANES), jnp.int32)],
        scratch_shapes=[pltpu.VMEM((1, LANES), F32), pltpu.VMEM((1, LANES), F32),
                        pltpu.VMEM((1, LANES), F32)],
        compiler_params=_cparams(("arbitrary", "arbitrary")),
        name="router",
    )(logits)


def _slots_kernel(dest_ref, zero_ref, inv_ref, sem):
    fill = pltpu.make_async_copy(zero_ref, inv_ref, sem)
    fill.start()
    fill.wait()

    def put(tok, carry):
        inv_ref[dest_ref[2 * tok]] = tok
        inv_ref[dest_ref[2 * tok + 1]] = tok
        return carry

    lax.fori_loop(0, dest_ref.shape[0] // 2, put, 0, unroll=8)


def _slots_call(dest_flat, n_rows):
    return pl.pallas_call(
        _slots_kernel,
        in_specs=[pl.BlockSpec(memory_space=pltpu.SMEM), pl.BlockSpec(memory_space=pl.ANY)],
        out_specs=pl.BlockSpec(memory_space=pltpu.SMEM),
        out_shape=jax.ShapeDtypeStruct((n_rows,), jnp.int32),
        scratch_shapes=[pltpu.SemaphoreType.DMA(())],
        name="slots",
    )(dest_flat, jnp.zeros((n_rows,), jnp.int32))


GMM_GROUP = 6
GMM_TILES = 4


def _unit_tables(block_expert, used, n_blocks):
    n_units_max = N_EXPERTS + n_blocks // GMM_GROUP
    valid = jnp.arange(n_blocks) < used
    nblk_e = jnp.sum((block_expert[None, :] == jnp.arange(N_EXPERTS)[:, None]) & valid[None, :],
                     axis=1).astype(jnp.int32)
    first_e = jnp.cumsum(nblk_e) - nblk_e
    units_e = (nblk_e + GMM_GROUP - 1) // GMM_GROUP
    uend = jnp.cumsum(units_e)
    n_units = uend[-1]
    u = jnp.arange(n_units_max, dtype=jnp.int32)
    ue = jnp.minimum(jnp.sum(uend[None, :] <= u[:, None], axis=1), N_EXPERTS - 1).astype(jnp.int32)
    k = u - (uend - units_e)[ue]
    live = u < n_units
    ub = jnp.where(live, first_e[ue] + k * GMM_GROUP, 0).astype(jnp.int32)
    un = jnp.where(live, jnp.clip(nblk_e[ue] - k * GMM_GROUP, 0, GMM_GROUP), 0).astype(jnp.int32)
    return ue, ub, un, n_units.reshape(1).astype(jnp.int32)


def _gmm_kernel(ue_ref, ub_ref, un_ref, nu_ref, used_ref, inv_ref, h_ref, w1_ref, w3_ref, w2_ref, y_ref,
                xbuf, acc, w1b_s, w3b_s, w2b_s, gsem, osem, *, n_blocks):
    del ue_ref
    u = pl.program_id(0)
    jt = pl.program_id(1)
    n_units = nu_ref[0]
    slot = u % 2
    last_tile = pl.num_programs(1) - 1

    def blk_rows(i):
        return pl.ds(pl.multiple_of(i * MOE_BLOCK, MOE_BLOCK), MOE_BLOCK)

    rows_max = xbuf.shape[1]
    rows_per_tile = rows_max // GMM_TILES
    last_slot_row = inv_ref.shape[0] - 1

    def row_copy(base, r, s):
        tok = inv_ref[jnp.minimum(base + r, last_slot_row)]
        return pltpu.make_async_copy(h_ref.at[pl.ds(tok, 1)], xbuf.at[s, pl.ds(r, 1)], gsem.at[s])

    def wait_gather(s):
        pltpu.make_async_copy(xbuf.at[s], xbuf.at[s], gsem.at[s]).wait()

    def out_copy(unit, s, i):
        dst = pl.ds(pl.multiple_of((ub_ref[unit] + i) * MOE_BLOCK, MOE_BLOCK), MOE_BLOCK)
        return pltpu.make_async_copy(acc.at[s, blk_rows(i)], y_ref.at[dst], osem.at[s])

    def wait_out(unit, s):
        def body(i, carry):
            out_copy(unit, s, i).wait()
            return carry

        lax.fori_loop(0, un_ref[unit], body, 0)

    @pl.when((u == 0) & (jt == 0))
    def _():
        def body(r, carry):
            row_copy(ub_ref[0] * MOE_BLOCK, r, 0).start()
            return carry

        lax.fori_loop(0, rows_max, body, 0, unroll=8)

    @pl.when((jt == 0) & (u < n_units))
    def _():
        @pl.when(u >= 2)
        def _():
            wait_out(u - 2, slot)

        wait_gather(slot)

    def tile_pass(j):
        first = j == 0
        w1b = w1_ref[0].astype(BF16)
        w3b = w3_ref[0].astype(BF16)
        w2b = w2_ref[0].astype(BF16)
        w1b_s[...] = w1b
        w3b_s[...] = w3b
        w2b_s[...] = w2b

        def rows_pass(r0, m, a1, a3, a2):
            rows = pl.ds(r0, m)
            xb = xbuf[slot, rows, :].astype(BF16)
            h1 = jnp.dot(xb, a1, preferred_element_type=F32)
            h3 = jnp.dot(xb, a3, preferred_element_type=F32)
            part = jnp.dot((_silu(h1) * h3).astype(BF16), a2, preferred_element_type=F32)
            if first:
                acc[slot, rows, :] = part
            else:
                acc[slot, rows, :] += part

        rows_pass(0, MOE_BLOCK, w1b, w3b, w2b)

        nxt_base = ub_ref[jnp.minimum(u + 1, n_units - 1)] * MOE_BLOCK
        for r in range(j * rows_per_tile, (j + 1) * rows_per_tile):
            row_copy(nxt_base, r, 1 - slot).start()

        rest = un_ref[u] - 1

        def pair(i, carry):
            r0 = pl.multiple_of(MOE_BLOCK + i * 2 * MOE_BLOCK, MOE_BLOCK)
            rows_pass(r0, 2 * MOE_BLOCK, w1b_s[...], w3b_s[...], w2b_s[...])
            return carry

        lax.fori_loop(0, rest // 2, pair, 0)

        @pl.when(rest % 2 == 1)
        def _():
            rows_pass(pl.multiple_of(rest * MOE_BLOCK, MOE_BLOCK), MOE_BLOCK,
                      w1b_s[...], w3b_s[...], w2b_s[...])

    for j in range(GMM_TILES):
        @pl.when((u < n_units) & (jt == j))
        def _(j=j):
            tile_pass(j)

    @pl.when((u < n_units) & (jt == last_tile))
    def _():
        def start(i, carry):
            out_copy(u, slot, i).start()
            return carry

        lax.fori_loop(0, un_ref[u], start, 0)

    @pl.when((u == pl.num_programs(0) - 1) & (jt == last_tile))
    def _():
        last = n_units - 1
        wait_gather(1 - last % 2)
        wait_out(last, last % 2)

        @pl.when(n_units >= 2)
        def _():
            wait_out(last - 1, (last - 1) % 2)

        acc[0, blk_rows(0), :] = jnp.zeros((MOE_BLOCK, acc.shape[2]), F32)

        def zero_copy(b):
            dst = pl.ds(pl.multiple_of(b * MOE_BLOCK, MOE_BLOCK), MOE_BLOCK)
            return pltpu.make_async_copy(acc.at[0, blk_rows(0)], y_ref.at[dst], osem.at[0])

        def start(b, carry):
            zero_copy(b).start()
            return carry

        def wait(b, carry):
            zero_copy(b).wait()
            return carry

        lax.fori_loop(used_ref[0], n_blocks, start, 0)
        lax.fori_loop(used_ref[0], n_blocks, wait, 0)


def _gmm_call(block_expert, used, inv, h2, w1, w3, w2):
    n_rows = inv.shape[0]
    d = h2.shape[1]
    de = w1.shape[-1]
    n_blocks = n_rows // MOE_BLOCK
    tj = de // GMM_TILES
    ue, ub, un, nu = _unit_tables(block_expert, used[0], n_blocks)

    def live(u, jt, nu):
        ok = u < nu[0]
        return jnp.where(ok, u, nu[0] - 1), jnp.where(ok, jt, GMM_TILES - 1)

    def w13_map(u, jt, ue, ub, un, nu, used, inv):
        uu, jj = live(u, jt, nu)
        return (ue[uu], 0, jj)

    def w2_map(u, jt, ue, ub, un, nu, used, inv):
        uu, jj = live(u, jt, nu)
        return (ue[uu], jj, 0)

    rows = GMM_GROUP * MOE_BLOCK
    grid_spec = pltpu.PrefetchScalarGridSpec(
        num_scalar_prefetch=6,
        grid=(ue.shape[0], GMM_TILES),
        in_specs=[pl.BlockSpec(memory_space=pl.ANY),
                  pl.BlockSpec((1, d, tj), w13_map),
                  pl.BlockSpec((1, d, tj), w13_map),
                  pl.BlockSpec((1, tj, d), w2_map)],
        out_specs=pl.BlockSpec(memory_space=pl.ANY),
        scratch_shapes=[pltpu.VMEM((2, rows, d), F32), pltpu.VMEM((2, rows, d), F32),
                        pltpu.VMEM((d, tj), BF16), pltpu.VMEM((d, tj), BF16), pltpu.VMEM((tj, d), BF16),
                        pltpu.SemaphoreType.DMA((2,)), pltpu.SemaphoreType.DMA((2,))],
    )
    return pl.pallas_call(
        functools.partial(_gmm_kernel, n_blocks=n_blocks),
        grid_spec=grid_spec,
        out_shape=jax.ShapeDtypeStruct((n_rows, d), F32),
        compiler_params=_cparams(("arbitrary", "arbitrary")),
        name="gmm",
    )(ue, ub, un, nu, used, inv, h2, w1, w3, w2)


def _combine_kernel(dest_ref, y_ref, x1_ref, gate_ref, mods_ref, fg_ref, o_ref, ybuf, sem, *, seq):
    i = pl.program_id(0)
    tm = x1_ref.shape[0]
    slot = i % 2

    last = pl.num_programs(0) - 1

    def row_copy(step, r, k, s):
        return pltpu.make_async_copy(y_ref.at[pl.ds(dest_ref[2 * (step * tm + r) + k], 1)],
                                     ybuf.at[s, k, pl.ds(r, 1)], sem.at[s])

    def wait_gather(s):
        pltpu.make_async_copy(ybuf.at[s], ybuf.at[s], sem.at[s]).wait()

    @pl.when(i == 0)
    def _():
        def body(r, carry):
            row_copy(0, r, 0, 0).start()
            row_copy(0, r, 1, 0).start()
            return carry

        lax.fori_loop(0, tm, body, 0, unroll=4)

    wait_gather(slot)

    nxt = jnp.minimum(i + 1, last)
    for r in range(tm):
        row_copy(nxt, r, 0, 1 - slot).start()
        row_copy(nxt, r, 1, 1 - slot).start()

    b = (i * tm) // seq
    gt2 = mods_ref[5, pl.ds(b, 1), :]
    gate = gate_ref[...]
    moe = gate[:, 0:1] * ybuf[slot, 0] + gate[:, 1:2] * ybuf[slot, 1]
    x = x1_ref[...] + gt2 * moe
    o_ref[...] = x * lax.rsqrt(jnp.mean(x * x, axis=-1, keepdims=True) + EPS) * fg_ref[...]

    @pl.when(i == last)
    def _():
        wait_gather(1 - slot)


def _combine_call(dest_flat, y_pad, x1, gate, mods, final_g, seq):
    n, d = x1.shape
    tm = min(seq, 256)
    grid_spec = pltpu.PrefetchScalarGridSpec(
        num_scalar_prefetch=1,
        grid=(n // tm,),
        in_specs=[pl.BlockSpec(memory_space=pl.ANY),
                  pl.BlockSpec((tm, d), lambda i, dest: (i, 0)),
                  pl.BlockSpec((tm, LANES), lambda i, dest: (i, 0)),
                  pl.BlockSpec((6, 8, d), lambda i, dest: (0, 0, 0)),
                  pl.BlockSpec((1, d), lambda i, dest: (0, 0))],
        out_specs=pl.BlockSpec((tm, d), lambda i, dest: (i, 0)),
        scratch_shapes=[pltpu.VMEM((2, 2, tm, d), F32), pltpu.SemaphoreType.DMA((2,))],
    )
    return pl.pallas_call(
        functools.partial(_combine_kernel, seq=seq),
        grid_spec=grid_spec,
        out_shape=jax.ShapeDtypeStruct((n, d), F32),
        compiler_params=_cparams(("arbitrary",)),
        name="combine",
    )(dest_flat, y_pad, x1, gate, mods, final_g)


def _pad_lanes(a, lane0):
    return jnp.zeros((LANES,), F32).at[lane0:lane0 + a.shape[0]].set(a.astype(F32))


def kernel(x, c, ctx, c_ctx, w_mod, b_mod, norm1_g, w_in, pool_w, pool_scale, conv_w,
           a_log_f, dt_bias_f, a_log_b, dt_bias_b, out_norm_g, w_out, norm2_g,
           w_grp, b_grp, w_rt, b_rt, w1, w3, w2, final_g):
    bsz, t, d = x.shape
    depth = w_mod.shape[0]
    assert depth == 1, "single-layer problem: the context stream is read but never updated"
    heads = a_log_f.shape[1]
    pool_width = pool_w.shape[1] * pool_w.shape[2]
    dn_width = heads * HEAD_DIM
    q0 = pool_width
    z0 = q0 + 3 * dn_width
    ab0 = z0 + dn_width
    n_tok = bsz * t
    l = 0

    c8 = jnp.zeros((8, d), F32).at[:bsz].set(c).at[bsz].set(c_ctx)
    mod = _mod_call(c8, w_mod[l], b_mod[l])
    mods = mod.reshape(8, 6, d).transpose(1, 0, 2)

    wab_bf = jnp.zeros((d, LANES), BF16).at[:, :4 * heads].set(w_in[l][:, ab0:].astype(BF16))
    g1 = norm1_g[l].reshape(1, d)
    proj, ab = _inproj_call(x, mods, g1, w_in[l], wab_bf, ab0, None)
    proj_c, ab_c = _inproj_call(ctx, mods, g1, w_in[l], wab_bf, ab0, bsz)

    prm = jnp.zeros((8, LANES), F32)
    prm = prm.at[0].set(_pad_lanes(jnp.concatenate([a_log_f[l], a_log_b[l]]), 2 * heads))
    prm = prm.at[1].set(_pad_lanes(jnp.concatenate([dt_bias_f[l], dt_bias_b[l]]), 2 * heads))
    g, g_t = _gates_call(ab, prm, heads)
    gc, gc_t = _gates_call(ab_c, prm, heads)
    gr = g_t.reshape(bsz, LANES, t // CHUNK, CHUNK)
    grc = gc_t.reshape(bsz, LANES, ctx.shape[1] // CHUNK, CHUNK)

    dn = _delta_call(proj, proj_c, conv_w[l], g, gc, gr, grc, out_norm_g[l].reshape(1, HEAD_DIM),
                     heads, q0 // HEAD_DIM, z0 // HEAD_DIM)
    pool = _pool_call(proj, pool_w[l], pool_scale[l].reshape(1, pool_width), t // GRID_W, GRID_W)

    wr = jnp.zeros((d, LANES), F32).at[:, :N_GROUPS].set(w_grp[l]).at[:, EXP_LANE0:EXP_LANE0 + N_EXPERTS].set(w_rt[l])
    br = jnp.zeros((LANES,), F32).at[:N_GROUPS].set(b_grp[l]).at[EXP_LANE0:EXP_LANE0 + N_EXPERTS].set(b_rt[l])
    x1, h2, logits = _outproj_call(pool, dn, w_out[l].astype(BF16), x, mods, norm2_g[l].reshape(1, d),
                                   wr, br.reshape(1, LANES))

    n_blocks = (n_tok * 2 + N_EXPERTS * (MOE_BLOCK - 1) + MOE_BLOCK - 1) // MOE_BLOCK
    dest, gate, be = _router_call(logits.reshape(n_tok, LANES), n_blocks)
    dest_flat = dest[:, :2].reshape(-1)
    block_expert = be[:n_blocks, 0]
    used = be[0:1, 1]
    inv = _slots_call(dest_flat, n_blocks * MOE_BLOCK)
    y_pad = _gmm_call(block_expert, used, inv, h2.reshape(n_tok, d), w1[l], w3[l], w2[l])
    out = _combine_call(dest_flat, y_pad, x1.reshape(n_tok, d), gate, mods, final_g.reshape(1, d), t)
    return out.reshape(bsz, t, d)
```

```python
import functools

import jax
import jax.numpy as jnp
from jax import lax
from jax.experimental import pallas as pl
from jax.experimental.pallas import tpu as pltpu

F32 = jnp.float32
BF16 = jnp.bfloat16

GRID_W = 64
POOL_WINDOWS = (2, 4, 8, 16)
HEAD_DIM = 128
CONV_WIDTH = 5
CHUNK = 64
N_GROUPS = 4
EXPERTS_PER_GROUP = 8
N_EXPERTS = N_GROUPS * EXPERTS_PER_GROUP
MOE_BLOCK = 128
EPS = 1e-6
LANES = 128
PRE_UNROLL = 8
EXP_LANE0 = N_GROUPS

VMEM_LIMIT = 56 * 1024 * 1024


def _cparams(sem):
    return pltpu.CompilerParams(dimension_semantics=sem, vmem_limit_bytes=VMEM_LIMIT)


def _dot(a, b):
    return jnp.dot(a.astype(BF16), b.astype(BF16), preferred_element_type=F32)


def _dot_split(a, b):
    a_hi = a.astype(BF16)
    b_hi = b.astype(BF16)
    a_lo = (a - a_hi.astype(F32)).astype(BF16)
    b_lo = (b - b_hi.astype(F32)).astype(BF16)
    return (jnp.dot(a_hi, b_hi, preferred_element_type=F32)
            + jnp.dot(a_lo, b_hi, preferred_element_type=F32)
            + jnp.dot(a_hi, b_lo, preferred_element_type=F32))


def _silu(x):
    return x * jax.nn.sigmoid(x)


def _mod_kernel(c_ref, w_ref, b_ref, o_ref):
    o_ref[...] = _dot(_silu(c_ref[...]), w_ref[...]) + b_ref[...]


def _mod_call(c8, w_mod, b_mod):
    d, n = w_mod.shape
    tn = 512
    return pl.pallas_call(
        _mod_kernel,
        grid=(n // tn,),
        in_specs=[pl.BlockSpec((8, d), lambda j: (0, 0)),
                  pl.BlockSpec((d, tn), lambda j: (0, j)),
                  pl.BlockSpec((1, tn), lambda j: (0, j))],
        out_specs=pl.BlockSpec((8, tn), lambda j: (0, j)),
        out_shape=jax.ShapeDtypeStruct((8, n), F32),
        compiler_params=_cparams(("parallel",)),
        name="mod",
    )(c8, w_mod, b_mod.reshape(1, n))


def _inproj_kernel(x_ref, mods_ref, g_ref, w_ref, wab_ref, o_ref, ab_ref, hn_ref, *, mod_row):
    b = pl.program_id(0)
    j = pl.program_id(2)

    @pl.when(j == 0)
    def _():
        x = x_ref[0]
        y = x * lax.rsqrt(jnp.mean(x * x, axis=-1, keepdims=True) + EPS) * g_ref[...]
        row = b if mod_row is None else mod_row
        sh = mods_ref[0, pl.ds(row, 1), :]
        sc = mods_ref[1, pl.ds(row, 1), :]
        hb = (y * (1 + sc) + sh).astype(BF16)
        hn_ref[...] = hb
        ab_ref[0] = jnp.dot(hb, wab_ref[...], preferred_element_type=F32)

    o_ref[0] = jnp.dot(hn_ref[...], w_ref[...], preferred_element_type=F32)


def _inproj_call(x, mods, g, w_in, wab_bf, n_main, mod_row):
    bsz, t, d = x.shape
    tm = min(t, 1024)
    tn = 1024
    kern = functools.partial(_inproj_kernel, mod_row=mod_row)
    return pl.pallas_call(
        kern,
        grid=(bsz, t // tm, n_main // tn),
        in_specs=[pl.BlockSpec((1, tm, d), lambda b, i, j: (b, i, 0)),
                  pl.BlockSpec((2, 8, d), lambda b, i, j: (0, 0, 0)),
                  pl.BlockSpec((1, d), lambda b, i, j: (0, 0)),
                  pl.BlockSpec((d, tn), lambda b, i, j: (0, j)),
                  pl.BlockSpec((d, LANES), lambda b, i, j: (0, 0))],
        out_specs=[pl.BlockSpec((1, tm, tn), lambda b, i, j: (b, i, j)),
                   pl.BlockSpec((1, tm, LANES), lambda b, i, j: (b, i, 0))],
        out_shape=[jax.ShapeDtypeStruct((bsz, t, n_main), F32),
                   jax.ShapeDtypeStruct((bsz, t, LANES), F32)],
        scratch_shapes=[pltpu.VMEM((tm, d), BF16)],
        compiler_params=_cparams(("parallel", "parallel", "arbitrary")),
        name="inproj",
    )(x, mods, g, w_in, wab_bf)


def _gates_kernel(ab_ref, prm_ref, g_ref, gt_ref, *, heads):
    ab = ab_ref[0]
    t = ab.shape[0]
    h2, h3, h4, h6 = 2 * heads, 3 * heads, 4 * heads, 6 * heads
    beta = jax.nn.sigmoid(ab)
    xx = ab + prm_ref[1:2, :]
    softplus = jnp.maximum(xx, 0.0) + jnp.log1p(jnp.exp(-jnp.abs(xx)))
    g = -jnp.exp(prm_ref[0:1, :]) * softplus
    pos = lax.broadcasted_iota(jnp.int32, ab.shape, 0) & (CHUNK - 1)
    cs = g
    ss = g
    s = 1
    while s < CHUNK:
        cs = cs + jnp.where(pos >= s, pltpu.roll(cs, s, 0), 0.0)
        ss = ss + jnp.where(pos < CHUNK - s, pltpu.roll(ss, t - s, 0), 0.0)
        s *= 2
    tot = pltpu.roll(cs + ss - g, h2, 1)
    lane = lax.broadcasted_iota(jnp.int32, ab.shape, 1)
    out = jnp.where(lane < h2, beta,
                    jnp.where(lane < h3, cs,
                              jnp.where(lane < h4, ss,
                                        jnp.where(lane < h6, tot, 0.0))))
    g_ref[0] = out
    gt_ref[0] = out.T


def _gates_call(ab, prm, heads):
    bsz, t, _ = ab.shape
    return pl.pallas_call(
        functools.partial(_gates_kernel, heads=heads),
        grid=(bsz,),
        in_specs=[pl.BlockSpec((1, t, LANES), lambda b: (b, 0, 0)),
                  pl.BlockSpec((8, LANES), lambda b: (0, 0))],
        out_specs=[pl.BlockSpec((1, t, LANES), lambda b: (b, 0, 0)),
                   pl.BlockSpec((1, LANES, t), lambda b: (b, 0, 0))],
        out_shape=[jax.ShapeDtypeStruct((bsz, t, LANES), F32),
                   jax.ShapeDtypeStruct((bsz, LANES, t), F32)],
        compiler_params=_cparams(("parallel",)),
        name="gates",
    )(ab, prm)


def _conv_silu(x, w):
    n = x.shape[0]
    row = lax.broadcasted_iota(jnp.int32, x.shape, 0)
    acc = x * w[CONV_WIDTH // 2:CONV_WIDTH // 2 + 1, :]
    for j in range(CONV_WIDTH):
        d = j - CONV_WIDTH // 2
        if d == 0:
            continue
        xs = pltpu.roll(x, (-d) % n, 0)
        valid = (row + d >= 0) & (row + d < n)
        acc = acc + jnp.where(valid, xs, 0.0) * w[j:j + 1, :]
    return _silu(acc)


def _l2norm(a):
    return a * lax.rsqrt(jnp.sum(a * a, axis=-1, keepdims=True) + EPS)


def _lane_col(g, lane_idx):
    lane = lax.broadcasted_iota(jnp.int32, g.shape, 1)
    return jnp.sum(jnp.where(lane == lane_idx, g, 0.0), axis=-1, keepdims=True)


def _chunk_terms(chains, between=()):
    pending = list(between)

    def stage_done():
        if pending:
            pending.pop(0)()

    c, hd = chains[0][0].shape
    ri = lax.broadcasted_iota(jnp.int32, (c, c), 0)
    ci = lax.broadcasted_iota(jnp.int32, (c, c), 1)
    eye = jnp.where(ri == ci, 1.0, 0.0)
    right = lax.broadcasted_iota(jnp.int32, (c, 2 * c), 1) >= c
    nt = (((1,), (1,)), ((), ()))
    tn = (((0,), (0,)), ((), ()))

    decs, kn_bs, zs = [], [], []
    for kn_c, kb_c, _, _, _, _, gcc, gcr, upper in chains:
        incl = (ri <= ci) if upper else (ri >= ci)
        strict = (ri < ci) if upper else (ri > ci)
        dec = jnp.where(incl, jnp.exp(jnp.where(incl, gcc - gcr, 0.0)), 0.0)
        kn_b = kn_c.astype(BF16)
        kk = lax.dot_general(kb_c.astype(BF16), kn_b, nt, preferred_element_type=F32)
        decs.append(dec)
        kn_bs.append(kn_b)
        zs.append(jnp.concatenate([-jnp.where(strict, kk * dec, 0.0), eye], axis=1))
    stage_done()
    n = 1
    while n < c:
        zs = [_dot(z[:, :c], z) + jnp.where(right, z, 0.0) for z in zs]
        stage_done()
        n *= 2
    uw_bs = [_dot(z[:, c:], ch[3]).astype(BF16) for z, ch in zip(zs, chains)]
    stage_done()
    wns = [lax.dot_general(ch[4].astype(BF16), uw_b, tn, preferred_element_type=F32)
           for uw_b, ch in zip(uw_bs, chains)]
    stage_done()
    qks = [None if ch[2] is None else
           lax.dot_general(ch[2].astype(BF16), kn_b, nt, preferred_element_type=F32) * dec
           for ch, kn_b, dec in zip(chains, kn_bs, decs)]
    stage_done()
    qws = [None if qk is None else jnp.dot(qk.astype(BF16), uw_b, preferred_element_type=F32)
           for qk, uw_b in zip(qks, uw_bs)]
    while pending:
        stage_done()
    out = []
    for ch, wn, qw in zip(chains, wns, qws):
        nc, w2 = wn[:, :hd], wn[:, hd:]
        if qw is None:
            out.append((w2, nc, None, None))
        else:
            out.append((w2, nc, ch[5] - qw[:, hd:], qw[:, :hd]))
    return out


def _delta_kernel(q_ref, k_ref, v_ref, z_ref, kc_ref, vc_ref, cwq_ref, cwk_ref, cwv_ref,
                  g_ref, gc_ref, grf_ref, grb_ref, grcf_ref, grcb_ref, ong_ref,
                  out_ref,
                  qn_s, kn_s, kb_s, rhs_s, kd_s, qd_s, col_s,
                  knc_s, kbc_s, rhsc_s, kdc_s, colc_s,
                  w2_s, nc_s, qp_s, o0_s, w2c_s, ncc_s, o_s, *, heads):
    h = pl.program_id(1)
    t = q_ref.shape[1]
    tc = kc_ref.shape[1]
    hd = HEAD_DIM
    n_lat = t // CHUNK
    n_ctx = tc // CHUNK

    qn = _l2norm(_conv_silu(q_ref[0], cwq_ref[...])) * (hd ** -0.5)
    kn = _l2norm(_conv_silu(k_ref[0], cwk_ref[...]))
    vv = _conv_silu(v_ref[0], cwv_ref[...])
    knc = _l2norm(_conv_silu(kc_ref[0], cwk_ref[...]))
    vvc = _conv_silu(vc_ref[0], cwv_ref[...])
    qn_s[...] = qn
    kn_s[...] = kn
    knc_s[...] = knc
    g_lat = g_ref[0]
    g_ctx = gc_ref[0]

    for d in range(2):
        for (gt_, kn_, vv_, kb_r, rhs_r, kd_r, col_r, qn_, qd_r) in (
                (g_lat, kn, vv, kb_s, rhs_s, kd_s, col_s, qn, qd_s),
                (g_ctx, knc, vvc, kbc_s, rhsc_s, kdc_s, colc_s, None, None)):
            beta = _lane_col(gt_, d * heads + h)
            gcum = _lane_col(gt_, (2 + d) * heads + h)
            gtot = _lane_col(gt_, (4 + d) * heads + h)
            e = jnp.exp(gcum)
            kb = kn_ * beta
            kb_r[d] = kb
            rhs_r[d, :, 0:hd] = vv_ * beta
            rhs_r[d, :, hd:2 * hd] = kb * e
            kd_r[d] = kn_ * jnp.exp(gtot - gcum)
            lane = lax.broadcasted_iota(jnp.int32, (gcum.shape[0], LANES), 1)
            col_r[d] = jnp.where(lane == 0, gcum, jnp.where(lane == 1, jnp.exp(gtot), 0.0))
            if qn_ is not None:
                qd_r[d] = qn_ * e

    gr_lat = (grf_ref, grb_ref)
    gr_ctx = (grcf_ref, grcb_ref)

    un_ctx = min(PRE_UNROLL, n_ctx)
    un_lat = min(PRE_UNROLL, n_lat)

    def pre_ctx(i, carry):
        ids = [(i * un_ctx + u, d) for u in range(un_ctx) for d in range(2)]
        chains = []
        for c, d in ids:
            rows = pl.ds(pl.multiple_of(c * CHUNK, CHUNK), CHUNK)
            chains.append((knc_s[rows, :], kbc_s[d, rows, :], None, rhsc_s[d, rows, :], kdc_s[d, rows, :],
                           None, colc_s[d, rows, 0:1], gr_ctx[d][0, 0, pl.ds(c, 1), :], d == 1))
        for (c, d), (w2, nc, _, _) in zip(ids, _chunk_terms(chains)):
            m0 = pl.multiple_of(c * hd, hd)
            w2c_s[d, pl.ds(m0, hd), :] = w2
            ncc_s[d, pl.ds(m0, hd), :] = nc
        return carry

    lax.fori_loop(0, n_ctx // un_ctx, pre_ctx, 0)

    def lat_chunk(i, u, d):
        k = i * un_lat + u
        return k if d == 0 else n_lat - 1 - k

    def pre_lat(i, between=()):
        ids = [(lat_chunk(i, u, d), d) for u in range(un_lat) for d in range(2)]
        chains = []
        for c, d in ids:
            rows = pl.ds(pl.multiple_of(c * CHUNK, CHUNK), CHUNK)
            chains.append((kn_s[rows, :], kb_s[d, rows, :], qn_s[rows, :], rhs_s[d, rows, :], kd_s[d, rows, :],
                           qd_s[d, rows, :], col_s[d, rows, 0:1], gr_lat[d][0, 0, pl.ds(c, 1), :], d == 1))
        for (c, d), (w2, nc, qp, o0) in zip(ids, _chunk_terms(chains, between)):
            rows = pl.ds(pl.multiple_of(c * CHUNK, CHUNK), CHUNK)
            m0 = pl.multiple_of(c * hd, hd)
            w2_s[d, pl.ds(m0, hd), :] = w2
            nc_s[d, pl.ds(m0, hd), :] = nc
            qp_s[d, rows, :] = qp
            o0_s[d, rows, :] = o0

    def scan_ctx(i, states):
        new = []
        for d in range(2):
            s = states[d]
            c = i if d == 0 else n_ctx - 1 - i
            m0 = pl.multiple_of(c * hd, hd)
            gt = colc_s[d, pl.ds(pl.multiple_of(c * CHUNK, CHUNK), 1), 1:2]
            s = gt * s + ncc_s[d, pl.ds(m0, hd), :] - _dot(w2c_s[d, pl.ds(m0, hd), :], s)
            new.append(s)
        return tuple(new)

    zero = jnp.zeros((hd, hd), F32)
    states = lax.fori_loop(0, n_ctx, scan_ctx, (zero, zero))

    def scan_lat_steps(i, box):
        def step(u):
            def run():
                for d in range(2):
                    s = box[d]
                    c = lat_chunk(i, u, d)
                    r0 = pl.multiple_of(c * CHUNK, CHUNK)
                    rows = pl.ds(r0, CHUNK)
                    m0 = pl.multiple_of(c * hd, hd)
                    s_b = s.astype(BF16)
                    o_s[d, rows, :] = jnp.dot(qp_s[d, rows, :].astype(BF16), s_b,
                                              preferred_element_type=F32) + o0_s[d, rows, :]
                    gt = col_s[d, pl.ds(r0, 1), 1:2]
                    box[d] = gt * s + nc_s[d, pl.ds(m0, hd), :] - jnp.dot(
                        w2_s[d, pl.ds(m0, hd), :].astype(BF16), s_b, preferred_element_type=F32)
            return run

        return [step(u) for u in range(un_lat)]

    pre_lat(0)

    def lat_body(i, states):
        box = list(states)
        pre_lat(i, scan_lat_steps(i - 1, box))
        return tuple(box)

    states = lax.fori_loop(1, n_lat // un_lat, lat_body, states)
    box = list(states)
    for run in scan_lat_steps(n_lat // un_lat - 1, box):
        run()

    o = o_s[0] + o_s[1]
    o = o * lax.rsqrt(jnp.mean(o * o, axis=-1, keepdims=True) + EPS) * ong_ref[...]
    out_ref[0] = (o * _silu(z_ref[0])).astype(out_ref.dtype)


def _delta_call(proj, proj_c, conv_w, g, gc, gr, grc, ong, heads, q_blk0, z_blk0):
    bsz, t, _ = proj.shape
    tc = proj_c.shape[1]
    hd = HEAD_DIM
    n_lat, n_ctx = t // CHUNK, tc // CHUNK

    def col(off):
        return lambda b, h: (b, 0, off + h)

    def cw(off):
        return lambda b, h: (0, off + h)

    def grow(off):
        return lambda b, h: (b, off + h, 0, 0)

    in_specs = [
        pl.BlockSpec((1, t, hd), col(q_blk0)),
        pl.BlockSpec((1, t, hd), col(q_blk0 + heads)),
        pl.BlockSpec((1, t, hd), col(q_blk0 + 2 * heads)),
        pl.BlockSpec((1, t, hd), col(z_blk0)),
        pl.BlockSpec((1, tc, hd), col(q_blk0 + heads)),
        pl.BlockSpec((1, tc, hd), col(q_blk0 + 2 * heads)),
        pl.BlockSpec((CONV_WIDTH, hd), cw(0)),
        pl.BlockSpec((CONV_WIDTH, hd), cw(heads)),
        pl.BlockSpec((CONV_WIDTH, hd), cw(2 * heads)),
        pl.BlockSpec((1, t, LANES), lambda b, h: (b, 0, 0)),
        pl.BlockSpec((1, tc, LANES), lambda b, h: (b, 0, 0)),
        pl.BlockSpec((1, 1, n_lat, CHUNK), grow(2 * heads)),
        pl.BlockSpec((1, 1, n_lat, CHUNK), grow(3 * heads)),
        pl.BlockSpec((1, 1, n_ctx, CHUNK), grow(2 * heads)),
        pl.BlockSpec((1, 1, n_ctx, CHUNK), grow(3 * heads)),
        pl.BlockSpec((1, hd), lambda b, h: (0, 0)),
    ]
    scratch = [
        pltpu.VMEM((t, hd), F32), pltpu.VMEM((t, hd), F32),
        pltpu.VMEM((2, t, hd), F32), pltpu.VMEM((2, t, 2 * hd), F32),
        pltpu.VMEM((2, t, hd), F32), pltpu.VMEM((2, t, hd), F32),
        pltpu.VMEM((2, t, LANES), F32),
        pltpu.VMEM((tc, hd), F32), pltpu.VMEM((2, tc, hd), F32),
        pltpu.VMEM((2, tc, 2 * hd), F32), pltpu.VMEM((2, tc, hd), F32),
        pltpu.VMEM((2, tc, LANES), F32),
        pltpu.VMEM((2, n_lat * hd, hd), F32), pltpu.VMEM((2, n_lat * hd, hd), F32),
        pltpu.VMEM((2, t, hd), F32), pltpu.VMEM((2, t, hd), F32),
        pltpu.VMEM((2, n_ctx * hd, hd), F32), pltpu.VMEM((2, n_ctx * hd, hd), F32),
        pltpu.VMEM((2, t, hd), F32),
    ]
    return pl.pallas_call(
        functools.partial(_delta_kernel, heads=heads),
        grid=(bsz, heads),
        in_specs=in_specs,
        out_specs=pl.BlockSpec((1, t, hd), lambda b, h: (b, 0, h)),
        out_shape=jax.ShapeDtypeStruct((bsz, t, heads * hd), BF16),
        scratch_shapes=scratch,
        compiler_params=_cparams(("parallel", "parallel")),
        name="delta",
    )(proj, proj, proj, proj, proj_c, proj_c, conv_w, conv_w, conv_w, g, gc, gr, gr, grc, grc, ong)


def _shift_rows(x, d, idx, size, stride):
    n = x.shape[0]
    xs = pltpu.roll(x, (-d * stride) % n, 0)
    return jnp.where((idx + d >= 0) & (idx + d < size), xs, 0.0)


def _box_sum_1d(x, win, idx, size, stride):
    m = win // 2
    lead = x
    trail = x
    k = 1
    while k < m:
        lead = lead + _shift_rows(lead, k, idx, size, stride)
        trail = trail + _shift_rows(trail, -k, idx, size, stride)
        k *= 2
    return lead + _shift_rows(trail, -1, idx, size, stride)


def _pool_kernel(u_ref, pw_ref, ps_ref, o_ref, *, rows, cols):
    t = u_ref.shape[1]
    gc = pw_ref.shape[1]
    tok = lax.broadcasted_iota(jnp.int32, (t, gc), 0)
    ci = tok % cols
    ri = tok // cols
    for gi, win in enumerate(POOL_WINDOWS):
        lo = win // 2
        hi = win - lo
        u = u_ref[0, :, gi * gc:(gi + 1) * gc]
        s = _box_sum_1d(u, win, ci, cols, 1)
        s = _box_sum_1d(s, win, ri, rows, cols)
        cnt_c = jnp.minimum(ci + hi, cols) - jnp.maximum(ci - lo, 0)
        cnt_r = jnp.minimum(ri + hi, rows) - jnp.maximum(ri - lo, 0)
        mean = s / (cnt_c * cnt_r).astype(F32)
        y = _dot(mean - u, pw_ref[gi]) * ps_ref[:, gi * gc:(gi + 1) * gc]
        o_ref[0, :, gi * gc:(gi + 1) * gc] = y.astype(o_ref.dtype)


def _pool_call(proj, pool_w, pool_scale, rows, cols):
    bsz, t, _ = proj.shape
    ng, gc, _ = pool_w.shape
    pwid = ng * gc
    return pl.pallas_call(
        functools.partial(_pool_kernel, rows=rows, cols=cols),
        grid=(bsz,),
        in_specs=[pl.BlockSpec((1, t, pwid), lambda b: (b, 0, 0)),
                  pl.BlockSpec((ng, gc, gc), lambda b: (0, 0, 0)),
                  pl.BlockSpec((1, pwid), lambda b: (0, 0))],
        out_specs=pl.BlockSpec((1, t, pwid), lambda b: (b, 0, 0)),
        out_shape=jax.ShapeDtypeStruct((bsz, t, pwid), BF16),
        compiler_params=_cparams(("parallel",)),
        name="pool",
    )(proj, pool_w, pool_scale)


def _outproj_kernel(pool_ref, dn_ref, wa_ref, wb_ref, x_ref, mods_ref, g2_ref, wr_ref, br_ref,
                    x1_ref, h2_ref, lg_ref):
    b = pl.program_id(0)
    mix = (jnp.dot(pool_ref[0], wa_ref[...], preferred_element_type=F32)
           + jnp.dot(dn_ref[0], wb_ref[...], preferred_element_type=F32))
    gt1 = mods_ref[2, pl.ds(b, 1), :]
    sh2 = mods_ref[3, pl.ds(b, 1), :]
    sc2 = mods_ref[4, pl.ds(b, 1), :]
    x1 = x_ref[0] + gt1 * mix
    x1_ref[0] = x1
    y = x1 * lax.rsqrt(jnp.mean(x1 * x1, axis=-1, keepdims=True) + EPS) * g2_ref[...]
    h2 = y * (1 + sc2) + sh2
    h2_ref[0] = h2
    lg_ref[0] = _dot_split(h2, wr_ref[...]) + br_ref[...]


def _outproj_call(pool, dn, w_out_bf, x, mods, g2, wr, br):
    bsz, t, d = x.shape
    half = pool.shape[-1]
    tm = min(t, 512)
    return pl.pallas_call(
        _outproj_kernel,
        grid=(bsz, t // tm),
        in_specs=[pl.BlockSpec((1, tm, half), lambda b, i: (b, i, 0)),
                  pl.BlockSpec((1, tm, half), lambda b, i: (b, i, 0)),
                  pl.BlockSpec((half, d), lambda b, i: (0, 0)),
                  pl.BlockSpec((half, d), lambda b, i: (1, 0)),
                  pl.BlockSpec((1, tm, d), lambda b, i: (b, i, 0)),
                  pl.BlockSpec((6, 8, d), lambda b, i: (0, 0, 0)),
                  pl.BlockSpec((1, d), lambda b, i: (0, 0)),
                  pl.BlockSpec((d, LANES), lambda b, i: (0, 0)),
                  pl.BlockSpec((1, LANES), lambda b, i: (0, 0))],
        out_specs=[pl.BlockSpec((1, tm, d), lambda b, i: (b, i, 0)),
                   pl.BlockSpec((1, tm, d), lambda b, i: (b, i, 0)),
                   pl.BlockSpec((1, tm, LANES), lambda b, i: (b, i, 0))],
        out_shape=[jax.ShapeDtypeStruct((bsz, t, d), F32),
                   jax.ShapeDtypeStruct((bsz, t, d), F32),
                   jax.ShapeDtypeStruct((bsz, t, LANES), F32)],
        compiler_params=_cparams(("parallel", "parallel")),
        name="outproj",
    )(pool, dn, w_out_bf, w_out_bf, x, mods, g2, wr, br)


def _router_kernel(lg_ref, dest_ref, gate_ref, be_ref, cnt_s, run_s, off_s):
    p = pl.program_id(0)
    i = pl.program_id(1)
    lg = lg_ref[...]
    tm = lg.shape[0]
    lane = lax.broadcasted_iota(jnp.int32, lg.shape, 1)
    lane_f = lane.astype(F32)
    neg = -jnp.inf
    big = float(LANES)

    grp = jnp.where(lane < N_GROUPS, lg, neg)
    gmax = jnp.max(grp, axis=-1, keepdims=True)
    gidx = jnp.min(jnp.where(grp == gmax, lane_f, big), axis=-1, keepdims=True)
    p_grp = 1.0 / jnp.sum(jnp.where(lane < N_GROUPS, jnp.exp(lg - gmax), 0.0), axis=-1, keepdims=True)
    lo = EXP_LANE0 + EXPERTS_PER_GROUP * gidx
    ev = jnp.where((lane_f >= lo) & (lane_f < lo + EXPERTS_PER_GROUP), lg, neg)
    t1 = jnp.max(ev, axis=-1, keepdims=True)
    i1 = jnp.min(jnp.where(ev == t1, lane_f, big), axis=-1, keepdims=True)
    ev2 = jnp.where(lane_f == i1, neg, ev)
    t2 = jnp.max(ev2, axis=-1, keepdims=True)
    i2 = jnp.min(jnp.where(ev2 == t2, lane_f, big), axis=-1, keepdims=True)
    oh1 = lane_f == i1
    oh2 = lane_f == i2
    cnt = oh1.astype(F32) + oh2.astype(F32)
    colsum = jnp.sum(cnt, axis=0, keepdims=True)

    @pl.when(p == 0)
    def _():
        @pl.when(i == 0)
        def _():
            cnt_s[...] = jnp.zeros_like(cnt_s)

        cnt_s[...] += colsum

    @pl.when(p == 1)
    def _():
        @pl.when(i == 0)
        def _():
            nblk = jnp.floor((cnt_s[...] + (MOE_BLOCK - 1)) * (1.0 / MOE_BLOCK))
            r = lax.broadcasted_iota(jnp.int32, (LANES, LANES), 0)
            c = lax.broadcasted_iota(jnp.int32, (LANES, LANES), 1)
            tri = (r < c).astype(BF16)
            nb8 = jnp.broadcast_to(nblk, (8, LANES))
            start_blk = jnp.dot(nb8.astype(BF16), tri, preferred_element_type=F32)[0:1, :]
            off_s[...] = start_blk * MOE_BLOCK
            run_s[...] = jnp.zeros_like(run_s)
            end_blk = start_blk + nblk
            nb = be_ref.shape[0]
            blk = lax.broadcasted_iota(jnp.int32, (nb, LANES), 0).astype(F32)
            ln = lax.broadcasted_iota(jnp.int32, (nb, LANES), 1)
            is_exp = (ln >= EXP_LANE0) & (ln < EXP_LANE0 + N_EXPERTS)
            done = jnp.sum(jnp.where(is_exp & (end_blk <= blk), 1.0, 0.0), axis=-1, keepdims=True)
            bexp = jnp.minimum(done, N_EXPERTS - 1.0)
            used = jnp.max(jnp.where(is_exp, end_blk, 0.0), axis=-1, keepdims=True)
            be_ref[...] = jnp.where(ln == 0, bexp, jnp.where(ln == 1, used, 0.0)).astype(jnp.int32)

        rr = lax.broadcasted_iota(jnp.int32, (tm, tm), 0)
        cc = lax.broadcasted_iota(jnp.int32, (tm, tm), 1)
        before = (cc < rr).astype(BF16)
        prefix = jnp.dot(before, cnt.astype(BF16), preferred_element_type=F32)
        base = off_s[...] + run_s[...] + prefix
        d1 = jnp.sum(jnp.where(oh1, base, 0.0), axis=-1, keepdims=True)
        d2 = jnp.sum(jnp.where(oh2, base, 0.0), axis=-1, keepdims=True)
        run_s[...] += colsum
        dd = jnp.exp(t2 - t1)
        g1 = p_grp / (1.0 + dd)
        g2 = p_grp * dd / (1.0 + dd)
        dest_ref[...] = jnp.where(lane == 0, d1, jnp.where(lane == 1, d2, 0.0)).astype(jnp.int32)
        gate_ref[...] = jnp.where(lane == 0, g1, jnp.where(lane == 1, g2, 0.0))


def _router_call(logits, n_blocks):
    n = logits.shape[0]
    tm = min(n, 512)
    nb_pad = ((n_blocks + 7) // 8) * 8
    return pl.pallas_call(
        _router_kernel,
        grid=(2, n // tm),
        in_specs=[pl.BlockSpec((tm, LANES), lambda p, i: (i, 0))],
        out_specs=[pl.BlockSpec((tm, LANES), lambda p, i: (i * p, 0)),
                   pl.BlockSpec((tm, LANES), lambda p, i: (i * p, 0)),
                   pl.BlockSpec((nb_pad, LANES), lambda p, i: (0, 0))],
        out_shape=[jax.ShapeDtypeStruct((n, LANES), jnp.int32),
                   jax.ShapeDtypeStruct((n, LANES), F32),
                   jax.ShapeDtypeStruct((nb_pad, LANES), jnp.int32)],
        scratch_shapes=[pltpu.VMEM((1, LANES), F32), pltpu.VMEM((1, LANES), F32),
                        pltpu.VMEM((1, LANES), F32)],
        compiler_params=_cparams(("arbitrary", "arbitrary")),
        name="router",
    )(logits)


def _slots_kernel(dest_ref, zero_ref, inv_ref, sem):
    fill = pltpu.make_async_copy(zero_ref, inv_ref, sem)
    fill.start()
    fill.wait()

    def put(tok, carry):
        inv_ref[dest_ref[2 * tok]] = tok
        inv_ref[dest_ref[2 * tok + 1]] = tok
        return carry

    lax.fori_loop(0, dest_ref.shape[0] // 2, put, 0, unroll=8)


def _slots_call(dest_flat, n_rows):
    return pl.pallas_call(
        _slots_kernel,
        in_specs=[pl.BlockSpec(memory_space=pltpu.SMEM), pl.BlockSpec(memory_space=pl.ANY)],
        out_specs=pl.BlockSpec(memory_space=pltpu.SMEM),
        out_shape=jax.ShapeDtypeStruct((n_rows,), jnp.int32),
        scratch_shapes=[pltpu.SemaphoreType.DMA(())],
        name="slots",
    )(dest_flat, jnp.zeros((n_rows,), jnp.int32))


GMM_GROUP = 6
GMM_TILES = 4


def _unit_tables(block_expert, used, n_blocks):
    n_units_max = N_EXPERTS + n_blocks // GMM_GROUP
    valid = jnp.arange(n_blocks) < used
    nblk_e = jnp.sum((block_expert[None, :] == jnp.arange(N_EXPERTS)[:, None]) & valid[None, :],
                     axis=1).astype(jnp.int32)
    first_e = jnp.cumsum(nblk_e) - nblk_e
    units_e = (nblk_e + GMM_GROUP - 1) // GMM_GROUP
    uend = jnp.cumsum(units_e)
    n_units = uend[-1]
    u = jnp.arange(n_units_max, dtype=jnp.int32)
    ue = jnp.minimum(jnp.sum(uend[None, :] <= u[:, None], axis=1), N_EXPERTS - 1).astype(jnp.int32)
    k = u - (uend - units_e)[ue]
    live = u < n_units
    ub = jnp.where(live, first_e[ue] + k * GMM_GROUP, 0).astype(jnp.int32)
    un = jnp.where(live, jnp.clip(nblk_e[ue] - k * GMM_GROUP, 0, GMM_GROUP), 0).astype(jnp.int32)
    return ue, ub, un, n_units.reshape(1).astype(jnp.int32)


def _gmm_kernel(ue_ref, ub_ref, un_ref, nu_ref, used_ref, inv_ref, h_ref, w1_ref, w3_ref, w2_ref, y_ref,
                xbuf, acc, w1b_s, w3b_s, w2b_s, gsem, osem, *, n_blocks):
    del ue_ref
    u = pl.program_id(0)
    jt = pl.program_id(1)
    n_units = nu_ref[0]
    slot = u % 2
    last_tile = pl.num_programs(1) - 1

    def gather(unit, s):
        base = ub_ref[unit] * MOE_BLOCK

        def body(g, carry):
            for k in range(8):
                r = g * 8 + k
                pltpu.make_async_copy(h_ref.at[pl.ds(inv_ref[base + r], 1)], xbuf.at[s, pl.ds(r, 1)],
                                      gsem.at[s]).start()
            return carry

        lax.fori_loop(0, un_ref[unit] * (MOE_BLOCK // 8), body, 0)

    def wait_gather(unit, s):
        rows = pl.ds(0, un_ref[unit] * MOE_BLOCK)
        pltpu.make_async_copy(xbuf.at[s, rows], xbuf.at[s, rows], gsem.at[s]).wait()

    def blk_rows(i):
        return pl.ds(pl.multiple_of(i * MOE_BLOCK, MOE_BLOCK), MOE_BLOCK)

    def out_copy(unit, s, i):
        dst = pl.ds(pl.multiple_of((ub_ref[unit] + i) * MOE_BLOCK, MOE_BLOCK), MOE_BLOCK)
        return pltpu.make_async_copy(acc.at[s, blk_rows(i)], y_ref.at[dst], osem.at[s])

    def wait_out(unit, s):
        def body(i, carry):
            out_copy(unit, s, i).wait()
            return carry

        lax.fori_loop(0, un_ref[unit], body, 0)

    @pl.when((u == 0) & (jt == 0))
    def _():
        gather(0, 0)

    @pl.when((jt == 0) & (u < n_units))
    def _():
        @pl.when(u >= 2)
        def _():
            wait_out(u - 2, slot)

        @pl.when(u + 1 < n_units)
        def _():
            gather(u + 1, 1 - slot)

        wait_gather(u, slot)

    def tile_pass(first):
        w1b = w1_ref[0].astype(BF16)
        w3b = w3_ref[0].astype(BF16)
        w2b = w2_ref[0].astype(BF16)
        w1b_s[...] = w1b
        w3b_s[...] = w3b
        w2b_s[...] = w2b

        def rows_pass(r0, m, a1, a3, a2):
            rows = pl.ds(r0, m)
            xb = xbuf[slot, rows, :].astype(BF16)
            h1 = jnp.dot(xb, a1, preferred_element_type=F32)
            h3 = jnp.dot(xb, a3, preferred_element_type=F32)
            part = jnp.dot((_silu(h1) * h3).astype(BF16), a2, preferred_element_type=F32)
            if first:
                acc[slot, rows, :] = part
            else:
                acc[slot, rows, :] += part

        rows_pass(0, MOE_BLOCK, w1b, w3b, w2b)
        rest = un_ref[u] - 1

        def pair(i, carry):
            r0 = pl.multiple_of(MOE_BLOCK + i * 2 * MOE_BLOCK, MOE_BLOCK)
            rows_pass(r0, 2 * MOE_BLOCK, w1b_s[...], w3b_s[...], w2b_s[...])
            return carry

        lax.fori_loop(0, rest // 2, pair, 0)

        @pl.when(rest % 2 == 1)
        def _():
            rows_pass(pl.multiple_of(rest * MOE_BLOCK, MOE_BLOCK), MOE_BLOCK,
                      w1b_s[...], w3b_s[...], w2b_s[...])

    @pl.when((u < n_units) & (jt == 0))
    def _():
        tile_pass(True)

    @pl.when((u < n_units) & (jt > 0))
    def _():
        tile_pass(False)

    @pl.when((u < n_units) & (jt == last_tile))
    def _():
        def start(i, carry):
            out_copy(u, slot, i).start()
            return carry

        lax.fori_loop(0, un_ref[u], start, 0)

    @pl.when((u == pl.num_programs(0) - 1) & (jt == last_tile))
    def _():
        last = n_units - 1
        wait_out(last, last % 2)

        @pl.when(n_units >= 2)
        def _():
            wait_out(last - 1, (last - 1) % 2)

        acc[0, blk_rows(0), :] = jnp.zeros((MOE_BLOCK, acc.shape[2]), F32)

        def zero_copy(b):
            dst = pl.ds(pl.multiple_of(b * MOE_BLOCK, MOE_BLOCK), MOE_BLOCK)
            return pltpu.make_async_copy(acc.at[0, blk_rows(0)], y_ref.at[dst], osem.at[0])

        def start(b, carry):
            zero_copy(b).start()
            return carry

        def wait(b, carry):
            zero_copy(b).wait()
            return carry

        lax.fori_loop(used_ref[0], n_blocks, start, 0)
        lax.fori_loop(used_ref[0], n_blocks, wait, 0)


def _gmm_call(block_expert, used, inv, h2, w1, w3, w2):
    n_rows = inv.shape[0]
    d = h2.shape[1]
    de = w1.shape[-1]
    n_blocks = n_rows // MOE_BLOCK
    tj = de // GMM_TILES
    ue, ub, un, nu = _unit_tables(block_expert, used[0], n_blocks)

    def live(u, jt, nu):
        ok = u < nu[0]
        return jnp.where(ok, u, nu[0] - 1), jnp.where(ok, jt, GMM_TILES - 1)

    def w13_map(u, jt, ue, ub, un, nu, used, inv):
        uu, jj = live(u, jt, nu)
        return (ue[uu], 0, jj)

    def w2_map(u, jt, ue, ub, un, nu, used, inv):
        uu, jj = live(u, jt, nu)
        return (ue[uu], jj, 0)

    rows = GMM_GROUP * MOE_BLOCK
    grid_spec = pltpu.PrefetchScalarGridSpec(
        num_scalar_prefetch=6,
        grid=(ue.shape[0], GMM_TILES),
        in_specs=[pl.BlockSpec(memory_space=pl.ANY),
                  pl.BlockSpec((1, d, tj), w13_map),
                  pl.BlockSpec((1, d, tj), w13_map),
                  pl.BlockSpec((1, tj, d), w2_map)],
        out_specs=pl.BlockSpec(memory_space=pl.ANY),
        scratch_shapes=[pltpu.VMEM((2, rows, d), F32), pltpu.VMEM((2, rows, d), F32),
                        pltpu.VMEM((d, tj), BF16), pltpu.VMEM((d, tj), BF16), pltpu.VMEM((tj, d), BF16),
                        pltpu.SemaphoreType.DMA((2,)), pltpu.SemaphoreType.DMA((2,))],
    )
    return pl.pallas_call(
        functools.partial(_gmm_kernel, n_blocks=n_blocks),
        grid_spec=grid_spec,
        out_shape=jax.ShapeDtypeStruct((n_rows, d), F32),
        compiler_params=_cparams(("arbitrary", "arbitrary")),
        name="gmm",
    )(ue, ub, un, nu, used, inv, h2, w1, w3, w2)


def _combine_kernel(dest_ref, y_ref, x1_ref, gate_ref, mods_ref, fg_ref, o_ref, ybuf, sem, *, seq):
    i = pl.program_id(0)
    tm = x1_ref.shape[0]
    slot = i % 2

    last = pl.num_programs(0) - 1

    def row_copy(step, r, k, s):
        return pltpu.make_async_copy(y_ref.at[pl.ds(dest_ref[2 * (step * tm + r) + k], 1)],
                                     ybuf.at[s, k, pl.ds(r, 1)], sem.at[s])

    def wait_gather(s):
        pltpu.make_async_copy(ybuf.at[s], ybuf.at[s], sem.at[s]).wait()

    @pl.when(i == 0)
    def _():
        def body(r, carry):
            row_copy(0, r, 0, 0).start()
            row_copy(0, r, 1, 0).start()
            return carry

        lax.fori_loop(0, tm, body, 0, unroll=4)

    wait_gather(slot)

    nxt = jnp.minimum(i + 1, last)
    for r in range(tm):
        row_copy(nxt, r, 0, 1 - slot).start()
        row_copy(nxt, r, 1, 1 - slot).start()

    b = (i * tm) // seq
    gt2 = mods_ref[5, pl.ds(b, 1), :]
    gate = gate_ref[...]
    moe = gate[:, 0:1] * ybuf[slot, 0] + gate[:, 1:2] * ybuf[slot, 1]
    x = x1_ref[...] + gt2 * moe
    o_ref[...] = x * lax.rsqrt(jnp.mean(x * x, axis=-1, keepdims=True) + EPS) * fg_ref[...]

    @pl.when(i == last)
    def _():
        wait_gather(1 - slot)


def _combine_call(dest_flat, y_pad, x1, gate, mods, final_g, seq):
    n, d = x1.shape
    tm = min(seq, 256)
    grid_spec = pltpu.PrefetchScalarGridSpec(
        num_scalar_prefetch=1,
        grid=(n // tm,),
        in_specs=[pl.BlockSpec(memory_space=pl.ANY),
                  pl.BlockSpec((tm, d), lambda i, dest: (i, 0)),
                  pl.BlockSpec((tm, LANES), lambda i, dest: (i, 0)),
                  pl.BlockSpec((6, 8, d), lambda i, dest: (0, 0, 0)),
                  pl.BlockSpec((1, d), lambda i, dest: (0, 0))],
        out_specs=pl.BlockSpec((tm, d), lambda i, dest: (i, 0)),
        scratch_shapes=[pltpu.VMEM((2, 2, tm, d), F32), pltpu.SemaphoreType.DMA((2,))],
    )
    return pl.pallas_call(
        functools.partial(_combine_kernel, seq=seq),
        grid_spec=grid_spec,
        out_shape=jax.ShapeDtypeStruct((n, d), F32),
        compiler_params=_cparams(("arbitrary",)),
        name="combine",
    )(dest_flat, y_pad, x1, gate, mods, final_g)


def _pad_lanes(a, lane0):
    return jnp.zeros((LANES,), F32).at[lane0:lane0 + a.shape[0]].set(a.astype(F32))


def kernel(x, c, ctx, c_ctx, w_mod, b_mod, norm1_g, w_in, pool_w, pool_scale, conv_w,
           a_log_f, dt_bias_f, a_log_b, dt_bias_b, out_norm_g, w_out, norm2_g,
           w_grp, b_grp, w_rt, b_rt, w1, w3, w2, final_g):
    bsz, t, d = x.shape
    depth = w_mod.shape[0]
    assert depth == 1, "single-layer problem: the context stream is read but never updated"
    heads = a_log_f.shape[1]
    pool_width = pool_w.shape[1] * pool_w.shape[2]
    dn_width = heads * HEAD_DIM
    q0 = pool_width
    z0 = q0 + 3 * dn_width
    ab0 = z0 + dn_width
    n_tok = bsz * t
    l = 0

    c8 = jnp.zeros((8, d), F32).at[:bsz].set(c).at[bsz].set(c_ctx)
    mod = _mod_call(c8, w_mod[l], b_mod[l])
    mods = mod.reshape(8, 6, d).transpose(1, 0, 2)

    w_in_bf = w_in[l].astype(BF16)
    wab_bf = jnp.zeros((d, LANES), BF16).at[:, :4 * heads].set(w_in_bf[:, ab0:])
    g1 = norm1_g[l].reshape(1, d)
    proj, ab = _inproj_call(x, mods, g1, w_in_bf, wab_bf, ab0, None)
    proj_c, ab_c = _inproj_call(ctx, mods, g1, w_in_bf, wab_bf, ab0, bsz)

    prm = jnp.zeros((8, LANES), F32)
    prm = prm.at[0].set(_pad_lanes(jnp.concatenate([a_log_f[l], a_log_b[l]]), 2 * heads))
    prm = prm.at[1].set(_pad_lanes(jnp.concatenate([dt_bias_f[l], dt_bias_b[l]]), 2 * heads))
    g, g_t = _gates_call(ab, prm, heads)
    gc, gc_t = _gates_call(ab_c, prm, heads)
    gr = g_t.reshape(bsz, LANES, t // CHUNK, CHUNK)
    grc = gc_t.reshape(bsz, LANES, ctx.shape[1] // CHUNK, CHUNK)

    dn = _delta_call(proj, proj_c, conv_w[l], g, gc, gr, grc, out_norm_g[l].reshape(1, HEAD_DIM),
                     heads, q0 // HEAD_DIM, z0 // HEAD_DIM)
    pool = _pool_call(proj, pool_w[l], pool_scale[l].reshape(1, pool_width), t // GRID_W, GRID_W)

    wr = jnp.zeros((d, LANES), F32).at[:, :N_GROUPS].set(w_grp[l]).at[:, EXP_LANE0:EXP_LANE0 + N_EXPERTS].set(w_rt[l])
    br = jnp.zeros((LANES,), F32).at[:N_GROUPS].set(b_grp[l]).at[EXP_LANE0:EXP_LANE0 + N_EXPERTS].set(b_rt[l])
    x1, h2, logits = _outproj_call(pool, dn, w_out[l].astype(BF16), x, mods, norm2_g[l].reshape(1, d),
                                   wr, br.reshape(1, LANES))

    n_blocks = (n_tok * 2 + N_EXPERTS * (MOE_BLOCK - 1) + MOE_BLOCK - 1) // MOE_BLOCK
    dest, gate, be = _router_call(logits.reshape(n_tok, LANES), n_blocks)
    dest_flat = dest[:, :2].reshape(-1)
    block_expert = be[:n_blocks, 0]
    used = be[0:1, 1]
    inv = _slots_call(dest_flat, n_blocks * MOE_BLOCK)
    y_pad = _gmm_call(block_expert, used, inv, h2.reshape(n_tok, d), w1[l], w3[l], w2[l])
    out = _combine_call(dest_flat, y_pad, x1.reshape(n_tok, d), gate, mods, final_g.reshape(1, d), t)
    return out.reshape(bsz, t, d)
```

```python
import functools

import jax
import jax.numpy as jnp
from jax import lax
from jax.experimental import pallas as pl
from jax.experimental.pallas import tpu as pltpu

F32 = jnp.float32
BF16 = jnp.bfloat16

GRID_W = 64
POOL_WINDOWS = (2, 4, 8, 16)
HEAD_DIM = 128
CONV_WIDTH = 5
CHUNK = 64
N_GROUPS = 4
EXPERTS_PER_GROUP = 8
N_EXPERTS = N_GROUPS * EXPERTS_PER_GROUP
MOE_BLOCK = 128
EPS = 1e-6
LANES = 128
PRE_UNROLL = 8
EXP_LANE0 = N_GROUPS

VMEM_LIMIT = 56 * 1024 * 1024


def _cparams(sem):
    return pltpu.CompilerParams(dimension_semantics=sem, vmem_limit_bytes=VMEM_LIMIT)


def _dot(a, b):
    return jnp.dot(a.astype(BF16), b.astype(BF16), preferred_element_type=F32)


def _dot_split(a, b):
    a_hi = a.astype(BF16)
    b_hi = b.astype(BF16)
    a_lo = (a - a_hi.astype(F32)).astype(BF16)
    b_lo = (b - b_hi.astype(F32)).astype(BF16)
    return (jnp.dot(a_hi, b_hi, preferred_element_type=F32)
            + jnp.dot(a_lo, b_hi, preferred_element_type=F32)
            + jnp.dot(a_hi, b_lo, preferred_element_type=F32))


def _silu(x):
    return x * jax.nn.sigmoid(x)


def _mod_kernel(c_ref, w_ref, b_ref, o_ref):
    o_ref[...] = _dot(_silu(c_ref[...]), w_ref[...]) + b_ref[...]


def _mod_call(c8, w_mod, b_mod):
    d, n = w_mod.shape
    tn = 512
    return pl.pallas_call(
        _mod_kernel,
        grid=(n // tn,),
        in_specs=[pl.BlockSpec((8, d), lambda j: (0, 0)),
                  pl.BlockSpec((d, tn), lambda j: (0, j)),
                  pl.BlockSpec((1, tn), lambda j: (0, j))],
        out_specs=pl.BlockSpec((8, tn), lambda j: (0, j)),
        out_shape=jax.ShapeDtypeStruct((8, n), F32),
        compiler_params=_cparams(("parallel",)),
        name="mod",
    )(c8, w_mod, b_mod.reshape(1, n))


def _inproj_kernel(x_ref, mods_ref, g_ref, w_ref, wab_ref, o_ref, ab_ref, hn_ref, *, mod_row):
    b = pl.program_id(0)
    j = pl.program_id(2)

    @pl.when(j == 0)
    def _():
        x = x_ref[0]
        y = x * lax.rsqrt(jnp.mean(x * x, axis=-1, keepdims=True) + EPS) * g_ref[...]
        row = b if mod_row is None else mod_row
        sh = mods_ref[0, pl.ds(row, 1), :]
        sc = mods_ref[1, pl.ds(row, 1), :]
        hb = (y * (1 + sc) + sh).astype(BF16)
        hn_ref[...] = hb
        ab_ref[0] = jnp.dot(hb, wab_ref[...], preferred_element_type=F32)

    o_ref[0] = jnp.dot(hn_ref[...], w_ref[...], preferred_element_type=F32)


def _inproj_call(x, mods, g, w_in, wab_bf, n_main, mod_row):
    bsz, t, d = x.shape
    tm = min(t, 1024)
    tn = 1024
    kern = functools.partial(_inproj_kernel, mod_row=mod_row)
    return pl.pallas_call(
        kern,
        grid=(bsz, t // tm, n_main // tn),
        in_specs=[pl.BlockSpec((1, tm, d), lambda b, i, j: (b, i, 0)),
                  pl.BlockSpec((2, 8, d), lambda b, i, j: (0, 0, 0)),
                  pl.BlockSpec((1, d), lambda b, i, j: (0, 0)),
                  pl.BlockSpec((d, tn), lambda b, i, j: (0, j)),
                  pl.BlockSpec((d, LANES), lambda b, i, j: (0, 0))],
        out_specs=[pl.BlockSpec((1, tm, tn), lambda b, i, j: (b, i, j)),
                   pl.BlockSpec((1, tm, LANES), lambda b, i, j: (b, i, 0))],
        out_shape=[jax.ShapeDtypeStruct((bsz, t, n_main), F32),
                   jax.ShapeDtypeStruct((bsz, t, LANES), F32)],
        scratch_shapes=[pltpu.VMEM((tm, d), BF16)],
        compiler_params=_cparams(("parallel", "parallel", "arbitrary")),
        name="inproj",
    )(x, mods, g, w_in, wab_bf)


def _gates_kernel(ab_ref, prm_ref, g_ref, gt_ref, *, heads):
    ab = ab_ref[0]
    t = ab.shape[0]
    h2, h3, h4, h6 = 2 * heads, 3 * heads, 4 * heads, 6 * heads
    beta = jax.nn.sigmoid(ab)
    xx = ab + prm_ref[1:2, :]
    softplus = jnp.maximum(xx, 0.0) + jnp.log1p(jnp.exp(-jnp.abs(xx)))
    g = -jnp.exp(prm_ref[0:1, :]) * softplus
    pos = lax.broadcasted_iota(jnp.int32, ab.shape, 0) & (CHUNK - 1)
    cs = g
    ss = g
    s = 1
    while s < CHUNK:
        cs = cs + jnp.where(pos >= s, pltpu.roll(cs, s, 0), 0.0)
        ss = ss + jnp.where(pos < CHUNK - s, pltpu.roll(ss, t - s, 0), 0.0)
        s *= 2
    tot = pltpu.roll(cs + ss - g, h2, 1)
    lane = lax.broadcasted_iota(jnp.int32, ab.shape, 1)
    out = jnp.where(lane < h2, beta,
                    jnp.where(lane < h3, cs,
                              jnp.where(lane < h4, ss,
                                        jnp.where(lane < h6, tot, 0.0))))
    g_ref[0] = out
    gt_ref[0] = out.T


def _gates_call(ab, prm, heads):
    bsz, t, _ = ab.shape
    return pl.pallas_call(
        functools.partial(_gates_kernel, heads=heads),
        grid=(bsz,),
        in_specs=[pl.BlockSpec((1, t, LANES), lambda b: (b, 0, 0)),
                  pl.BlockSpec((8, LANES), lambda b: (0, 0))],
        out_specs=[pl.BlockSpec((1, t, LANES), lambda b: (b, 0, 0)),
                   pl.BlockSpec((1, LANES, t), lambda b: (b, 0, 0))],
        out_shape=[jax.ShapeDtypeStruct((bsz, t, LANES), F32),
                   jax.ShapeDtypeStruct((bsz, LANES, t), F32)],
        compiler_params=_cparams(("parallel",)),
        name="gates",
    )(ab, prm)


def _conv_silu(x, w):
    n = x.shape[0]
    row = lax.broadcasted_iota(jnp.int32, x.shape, 0)
    acc = x * w[CONV_WIDTH // 2:CONV_WIDTH // 2 + 1, :]
    for j in range(CONV_WIDTH):
        d = j - CONV_WIDTH // 2
        if d == 0:
            continue
        xs = pltpu.roll(x, (-d) % n, 0)
        valid = (row + d >= 0) & (row + d < n)
        acc = acc + jnp.where(valid, xs, 0.0) * w[j:j + 1, :]
    return _silu(acc)


def _l2norm(a):
    return a * lax.rsqrt(jnp.sum(a * a, axis=-1, keepdims=True) + EPS)


def _lane_col(g, lane_idx):
    lane = lax.broadcasted_iota(jnp.int32, g.shape, 1)
    return jnp.sum(jnp.where(lane == lane_idx, g, 0.0), axis=-1, keepdims=True)


def _chunk_terms(chains, between=()):
    pending = list(between)

    def stage_done():
        if pending:
            pending.pop(0)()

    c, hd = chains[0][0].shape
    ri = lax.broadcasted_iota(jnp.int32, (c, c), 0)
    ci = lax.broadcasted_iota(jnp.int32, (c, c), 1)
    eye = jnp.where(ri == ci, 1.0, 0.0)
    right = lax.broadcasted_iota(jnp.int32, (c, 2 * c), 1) >= c
    nt = (((1,), (1,)), ((), ()))
    tn = (((0,), (0,)), ((), ()))

    decs, kn_bs, zs = [], [], []
    for kn_c, kb_c, _, _, _, _, gcc, gcr, upper in chains:
        incl = (ri <= ci) if upper else (ri >= ci)
        strict = (ri < ci) if upper else (ri > ci)
        dec = jnp.where(incl, jnp.exp(jnp.where(incl, gcc - gcr, 0.0)), 0.0)
        kn_b = kn_c.astype(BF16)
        kk = lax.dot_general(kb_c.astype(BF16), kn_b, nt, preferred_element_type=F32)
        decs.append(dec)
        kn_bs.append(kn_b)
        zs.append(jnp.concatenate([-jnp.where(strict, kk * dec, 0.0), eye], axis=1))
    stage_done()
    n = 1
    while n < c:
        zs = [_dot(z[:, :c], z) + jnp.where(right, z, 0.0) for z in zs]
        stage_done()
        n *= 2
    uw_bs = [_dot(z[:, c:], ch[3]).astype(BF16) for z, ch in zip(zs, chains)]
    stage_done()
    wns = [lax.dot_general(ch[4].astype(BF16), uw_b, tn, preferred_element_type=F32)
           for uw_b, ch in zip(uw_bs, chains)]
    stage_done()
    qks = [None if ch[2] is None else
           lax.dot_general(ch[2].astype(BF16), kn_b, nt, preferred_element_type=F32) * dec
           for ch, kn_b, dec in zip(chains, kn_bs, decs)]
    stage_done()
    qws = [None if qk is None else jnp.dot(qk.astype(BF16), uw_b, preferred_element_type=F32)
           for qk, uw_b in zip(qks, uw_bs)]
    while pending:
        stage_done()
    out = []
    for ch, wn, qw in zip(chains, wns, qws):
        nc, w2 = wn[:, :hd], wn[:, hd:]
        if qw is None:
            out.append((w2, nc, None, None))
        else:
            out.append((w2, nc, ch[5] - qw[:, hd:], qw[:, :hd]))
    return out


def _delta_kernel(q_ref, k_ref, v_ref, z_ref, kc_ref, vc_ref, cwq_ref, cwk_ref, cwv_ref,
                  g_ref, gc_ref, grf_ref, grb_ref, grcf_ref, grcb_ref, ong_ref,
                  out_ref,
                  qn_s, kn_s, kb_s, rhs_s, kd_s, qd_s, col_s,
                  knc_s, kbc_s, rhsc_s, kdc_s, colc_s,
                  w2_s, nc_s, qp_s, o0_s, w2c_s, ncc_s, o_s, *, heads):
    h = pl.program_id(1)
    t = q_ref.shape[1]
    tc = kc_ref.shape[1]
    hd = HEAD_DIM
    n_lat = t // CHUNK
    n_ctx = tc // CHUNK

    qn = _l2norm(_conv_silu(q_ref[0], cwq_ref[...])) * (hd ** -0.5)
    kn = _l2norm(_conv_silu(k_ref[0], cwk_ref[...]))
    vv = _conv_silu(v_ref[0], cwv_ref[...])
    knc = _l2norm(_conv_silu(kc_ref[0], cwk_ref[...]))
    vvc = _conv_silu(vc_ref[0], cwv_ref[...])
    qn_s[...] = qn
    kn_s[...] = kn
    knc_s[...] = knc
    g_lat = g_ref[0]
    g_ctx = gc_ref[0]

    for d in range(2):
        for (gt_, kn_, vv_, kb_r, rhs_r, kd_r, col_r, qn_, qd_r) in (
                (g_lat, kn, vv, kb_s, rhs_s, kd_s, col_s, qn, qd_s),
                (g_ctx, knc, vvc, kbc_s, rhsc_s, kdc_s, colc_s, None, None)):
            beta = _lane_col(gt_, d * heads + h)
            gcum = _lane_col(gt_, (2 + d) * heads + h)
            gtot = _lane_col(gt_, (4 + d) * heads + h)
            e = jnp.exp(gcum)
            kb = kn_ * beta
            kb_r[d] = kb
            rhs_r[d, :, 0:hd] = vv_ * beta
            rhs_r[d, :, hd:2 * hd] = kb * e
            kd_r[d] = kn_ * jnp.exp(gtot - gcum)
            lane = lax.broadcasted_iota(jnp.int32, (gcum.shape[0], LANES), 1)
            col_r[d] = jnp.where(lane == 0, gcum, jnp.where(lane == 1, jnp.exp(gtot), 0.0))
            if qn_ is not None:
                qd_r[d] = qn_ * e

    gr_lat = (grf_ref, grb_ref)
    gr_ctx = (grcf_ref, grcb_ref)

    un_ctx = min(PRE_UNROLL, n_ctx)
    un_lat = min(PRE_UNROLL, n_lat)

    def pre_ctx(i, carry):
        ids = [(i * un_ctx + u, d) for u in range(un_ctx) for d in range(2)]
        chains = []
        for c, d in ids:
            rows = pl.ds(pl.multiple_of(c * CHUNK, CHUNK), CHUNK)
            chains.append((knc_s[rows, :], kbc_s[d, rows, :], None, rhsc_s[d, rows, :], kdc_s[d, rows, :],
                           None, colc_s[d, rows, 0:1], gr_ctx[d][0, 0, pl.ds(c, 1), :], d == 1))
        for (c, d), (w2, nc, _, _) in zip(ids, _chunk_terms(chains)):
            m0 = pl.multiple_of(c * hd, hd)
            w2c_s[d, pl.ds(m0, hd), :] = w2
            ncc_s[d, pl.ds(m0, hd), :] = nc
        return carry

    lax.fori_loop(0, n_ctx // un_ctx, pre_ctx, 0)

    def lat_chunk(i, u, d):
        k = i * un_lat + u
        return k if d == 0 else n_lat - 1 - k

    def pre_lat(i, between=()):
        ids = [(lat_chunk(i, u, d), d) for u in range(un_lat) for d in range(2)]
        chains = []
        for c, d in ids:
            rows = pl.ds(pl.multiple_of(c * CHUNK, CHUNK), CHUNK)
            chains.append((kn_s[rows, :], kb_s[d, rows, :], qn_s[rows, :], rhs_s[d, rows, :], kd_s[d, rows, :],
                           qd_s[d, rows, :], col_s[d, rows, 0:1], gr_lat[d][0, 0, pl.ds(c, 1), :], d == 1))
        for (c, d), (w2, nc, qp, o0) in zip(ids, _chunk_terms(chains, between)):
            rows = pl.ds(pl.multiple_of(c * CHUNK, CHUNK), CHUNK)
            m0 = pl.multiple_of(c * hd, hd)
            w2_s[d, pl.ds(m0, hd), :] = w2
            nc_s[d, pl.ds(m0, hd), :] = nc
            qp_s[d, rows, :] = qp
            o0_s[d, rows, :] = o0

    def scan_ctx(i, states):
        new = []
        for d in range(2):
            s = states[d]
            c = i if d == 0 else n_ctx - 1 - i
            m0 = pl.multiple_of(c * hd, hd)
            gt = colc_s[d, pl.ds(pl.multiple_of(c * CHUNK, CHUNK), 1), 1:2]
            s = gt * s + ncc_s[d, pl.ds(m0, hd), :] - _dot(w2c_s[d, pl.ds(m0, hd), :], s)
            new.append(s)
        return tuple(new)

    zero = jnp.zeros((hd, hd), F32)
    states = lax.fori_loop(0, n_ctx, scan_ctx, (zero, zero))

    def scan_lat_steps(i, box):
        def step(u):
            def run():
                for d in range(2):
                    s = box[d]
                    c = lat_chunk(i, u, d)
                    r0 = pl.multiple_of(c * CHUNK, CHUNK)
                    rows = pl.ds(r0, CHUNK)
                    m0 = pl.multiple_of(c * hd, hd)
                    s_b = s.astype(BF16)
                    o_s[d, rows, :] = jnp.dot(qp_s[d, rows, :].astype(BF16), s_b,
                                              preferred_element_type=F32) + o0_s[d, rows, :]
                    gt = col_s[d, pl.ds(r0, 1), 1:2]
                    box[d] = gt * s + nc_s[d, pl.ds(m0, hd), :] - jnp.dot(
                        w2_s[d, pl.ds(m0, hd), :].astype(BF16), s_b, preferred_element_type=F32)
            return run

        return [step(u) for u in range(un_lat)]

    pre_lat(0)

    def lat_body(i, states):
        box = list(states)
        pre_lat(i, scan_lat_steps(i - 1, box))
        return tuple(box)

    states = lax.fori_loop(1, n_lat // un_lat, lat_body, states)
    box = list(states)
    for run in scan_lat_steps(n_lat // un_lat - 1, box):
        run()

    o = o_s[0] + o_s[1]
    o = o * lax.rsqrt(jnp.mean(o * o, axis=-1, keepdims=True) + EPS) * ong_ref[...]
    out_ref[0] = (o * _silu(z_ref[0])).astype(out_ref.dtype)


def _delta_call(proj, proj_c, conv_w, g, gc, gr, grc, ong, heads, q_blk0, z_blk0):
    bsz, t, _ = proj.shape
    tc = proj_c.shape[1]
    hd = HEAD_DIM
    n_lat, n_ctx = t // CHUNK, tc // CHUNK

    def col(off):
        return lambda b, h: (b, 0, off + h)

    def cw(off):
        return lambda b, h: (0, off + h)

    def grow(off):
        return lambda b, h: (b, off + h, 0, 0)

    in_specs = [
        pl.BlockSpec((1, t, hd), col(q_blk0)),
        pl.BlockSpec((1, t, hd), col(q_blk0 + heads)),
        pl.BlockSpec((1, t, hd), col(q_blk0 + 2 * heads)),
        pl.BlockSpec((1, t, hd), col(z_blk0)),
        pl.BlockSpec((1, tc, hd), col(q_blk0 + heads)),
        pl.BlockSpec((1, tc, hd), col(q_blk0 + 2 * heads)),
        pl.BlockSpec((CONV_WIDTH, hd), cw(0)),
        pl.BlockSpec((CONV_WIDTH, hd), cw(heads)),
        pl.BlockSpec((CONV_WIDTH, hd), cw(2 * heads)),
        pl.BlockSpec((1, t, LANES), lambda b, h: (b, 0, 0)),
        pl.BlockSpec((1, tc, LANES), lambda b, h: (b, 0, 0)),
        pl.BlockSpec((1, 1, n_lat, CHUNK), grow(2 * heads)),
        pl.BlockSpec((1, 1, n_lat, CHUNK), grow(3 * heads)),
        pl.BlockSpec((1, 1, n_ctx, CHUNK), grow(2 * heads)),
        pl.BlockSpec((1, 1, n_ctx, CHUNK), grow(3 * heads)),
        pl.BlockSpec((1, hd), lambda b, h: (0, 0)),
    ]
    scratch = [
        pltpu.VMEM((t, hd), F32), pltpu.VMEM((t, hd), F32),
        pltpu.VMEM((2, t, hd), F32), pltpu.VMEM((2, t, 2 * hd), F32),
        pltpu.VMEM((2, t, hd), F32), pltpu.VMEM((2, t, hd), F32),
        pltpu.VMEM((2, t, LANES), F32),
        pltpu.VMEM((tc, hd), F32), pltpu.VMEM((2, tc, hd), F32),
        pltpu.VMEM((2, tc, 2 * hd), F32), pltpu.VMEM((2, tc, hd), F32),
        pltpu.VMEM((2, tc, LANES), F32),
        pltpu.VMEM((2, n_lat * hd, hd), F32), pltpu.VMEM((2, n_lat * hd, hd), F32),
        pltpu.VMEM((2, t, hd), F32), pltpu.VMEM((2, t, hd), F32),
        pltpu.VMEM((2, n_ctx * hd, hd), F32), pltpu.VMEM((2, n_ctx * hd, hd), F32),
        pltpu.VMEM((2, t, hd), F32),
    ]
    return pl.pallas_call(
        functools.partial(_delta_kernel, heads=heads),
        grid=(bsz, heads),
        in_specs=in_specs,
        out_specs=pl.BlockSpec((1, t, hd), lambda b, h: (b, 0, h)),
        out_shape=jax.ShapeDtypeStruct((bsz, t, heads * hd), BF16),
        scratch_shapes=scratch,
        compiler_params=_cparams(("parallel", "parallel")),
        name="delta",
    )(proj, proj, proj, proj, proj_c, proj_c, conv_w, conv_w, conv_w, g, gc, gr, gr, grc, grc, ong)


def _shift_rows(x, d, idx, size, stride):
    n = x.shape[0]
    xs = pltpu.roll(x, (-d * stride) % n, 0)
    return jnp.where((idx + d >= 0) & (idx + d < size), xs, 0.0)


def _box_sum_1d(x, win, idx, size, stride):
    m = win // 2
    lead = x
    trail = x
    k = 1
    while k < m:
        lead = lead + _shift_rows(lead, k, idx, size, stride)
        trail = trail + _shift_rows(trail, -k, idx, size, stride)
        k *= 2
    return lead + _shift_rows(trail, -1, idx, size, stride)


def _pool_kernel(u_ref, pw_ref, ps_ref, o_ref, *, rows, cols):
    t = u_ref.shape[1]
    gc = pw_ref.shape[1]
    tok = lax.broadcasted_iota(jnp.int32, (t, gc), 0)
    ci = tok % cols
    ri = tok // cols
    for gi, win in enumerate(POOL_WINDOWS):
        lo = win // 2
        hi = win - lo
        u = u_ref[0, :, gi * gc:(gi + 1) * gc]
        s = _box_sum_1d(u, win, ci, cols, 1)
        s = _box_sum_1d(s, win, ri, rows, cols)
        cnt_c = jnp.minimum(ci + hi, cols) - jnp.maximum(ci - lo, 0)
        cnt_r = jnp.minimum(ri + hi, rows) - jnp.maximum(ri - lo, 0)
        mean = s / (cnt_c * cnt_r).astype(F32)
        y = _dot(mean - u, pw_ref[gi]) * ps_ref[:, gi * gc:(gi + 1) * gc]
        o_ref[0, :, gi * gc:(gi + 1) * gc] = y.astype(o_ref.dtype)


def _pool_call(proj, pool_w, pool_scale, rows, cols):
    bsz, t, _ = proj.shape
    ng, gc, _ = pool_w.shape
    pwid = ng * gc
    return pl.pallas_call(
        functools.partial(_pool_kernel, rows=rows, cols=cols),
        grid=(bsz,),
        in_specs=[pl.BlockSpec((1, t, pwid), lambda b: (b, 0, 0)),
                  pl.BlockSpec((ng, gc, gc), lambda b: (0, 0, 0)),
                  pl.BlockSpec((1, pwid), lambda b: (0, 0))],
        out_specs=pl.BlockSpec((1, t, pwid), lambda b: (b, 0, 0)),
        out_shape=jax.ShapeDtypeStruct((bsz, t, pwid), BF16),
        compiler_params=_cparams(("parallel",)),
        name="pool",
    )(proj, pool_w, pool_scale)


def _outproj_kernel(pool_ref, dn_ref, wa_ref, wb_ref, x_ref, mods_ref, g2_ref, wr_ref, br_ref,
                    x1_ref, h2_ref, lg_ref):
    b = pl.program_id(0)
    mix = (jnp.dot(pool_ref[0], wa_ref[...], preferred_element_type=F32)
           + jnp.dot(dn_ref[0], wb_ref[...], preferred_element_type=F32))
    gt1 = mods_ref[2, pl.ds(b, 1), :]
    sh2 = mods_ref[3, pl.ds(b, 1), :]
    sc2 = mods_ref[4, pl.ds(b, 1), :]
    x1 = x_ref[0] + gt1 * mix
    x1_ref[0] = x1
    y = x1 * lax.rsqrt(jnp.mean(x1 * x1, axis=-1, keepdims=True) + EPS) * g2_ref[...]
    h2 = y * (1 + sc2) + sh2
    h2_ref[0] = h2
    lg_ref[0] = _dot_split(h2, wr_ref[...]) + br_ref[...]


def _outproj_call(pool, dn, w_out_bf, x, mods, g2, wr, br):
    bsz, t, d = x.shape
    half = pool.shape[-1]
    tm = min(t, 512)
    return pl.pallas_call(
        _outproj_kernel,
        grid=(bsz, t // tm),
        in_specs=[pl.BlockSpec((1, tm, half), lambda b, i: (b, i, 0)),
                  pl.BlockSpec((1, tm, half), lambda b, i: (b, i, 0)),
                  pl.BlockSpec((half, d), lambda b, i: (0, 0)),
                  pl.BlockSpec((half, d), lambda b, i: (1, 0)),
                  pl.BlockSpec((1, tm, d), lambda b, i: (b, i, 0)),
                  pl.BlockSpec((6, 8, d), lambda b, i: (0, 0, 0)),
                  pl.BlockSpec((1, d), lambda b, i: (0, 0)),
                  pl.BlockSpec((d, LANES), lambda b, i: (0, 0)),
                  pl.BlockSpec((1, LANES), lambda b, i: (0, 0))],
        out_specs=[pl.BlockSpec((1, tm, d), lambda b, i: (b, i, 0)),
                   pl.BlockSpec((1, tm, d), lambda b, i: (b, i, 0)),
                   pl.BlockSpec((1, tm, LANES), lambda b, i: (b, i, 0))],
        out_shape=[jax.ShapeDtypeStruct((bsz, t, d), F32),
                   jax.ShapeDtypeStruct((bsz, t, d), F32),
                   jax.ShapeDtypeStruct((bsz, t, LANES), F32)],
        compiler_params=_cparams(("parallel", "parallel")),
        name="outproj",
    )(pool, dn, w_out_bf, w_out_bf, x, mods, g2, wr, br)


def _router_kernel(lg_ref, dest_ref, gate_ref, be_ref, cnt_s, run_s, off_s):
    p = pl.program_id(0)
    i = pl.program_id(1)
    lg = lg_ref[...]
    tm = lg.shape[0]
    lane = lax.broadcasted_iota(jnp.int32, lg.shape, 1)
    lane_f = lane.astype(F32)
    neg = -jnp.inf
    big = float(LANES)

    grp = jnp.where(lane < N_GROUPS, lg, neg)
    gmax = jnp.max(grp, axis=-1, keepdims=True)
    gidx = jnp.min(jnp.where(grp == gmax, lane_f, big), axis=-1, keepdims=True)
    p_grp = 1.0 / jnp.sum(jnp.where(lane < N_GROUPS, jnp.exp(lg - gmax), 0.0), axis=-1, keepdims=True)
    lo = EXP_LANE0 + EXPERTS_PER_GROUP * gidx
    ev = jnp.where((lane_f >= lo) & (lane_f < lo + EXPERTS_PER_GROUP), lg, neg)
    t1 = jnp.max(ev, axis=-1, keepdims=True)
    i1 = jnp.min(jnp.where(ev == t1, lane_f, big), axis=-1, keepdims=True)
    ev2 = jnp.where(lane_f == i1, neg, ev)
    t2 = jnp.max(ev2, axis=-1, keepdims=True)
    i2 = jnp.min(jnp.where(ev2 == t2, lane_f, big), axis=-1, keepdims=True)
    oh1 = lane_f == i1
    oh2 = lane_f == i2
    cnt = oh1.astype(F32) + oh2.astype(F32)
    colsum = jnp.sum(cnt, axis=0, keepdims=True)

    @pl.when(p == 0)
    def _():
        @pl.when(i == 0)
        def _():
            cnt_s[...] = jnp.zeros_like(cnt_s)

        cnt_s[...] += colsum

    @pl.when(p == 1)
    def _():
        @pl.when(i == 0)
        def _():
            nblk = jnp.floor((cnt_s[...] + (MOE_BLOCK - 1)) * (1.0 / MOE_BLOCK))
            r = lax.broadcasted_iota(jnp.int32, (LANES, LANES), 0)
            c = lax.broadcasted_iota(jnp.int32, (LANES, LANES), 1)
            tri = (r < c).astype(BF16)
            nb8 = jnp.broadcast_to(nblk, (8, LANES))
            start_blk = jnp.dot(nb8.astype(BF16), tri, preferred_element_type=F32)[0:1, :]
            off_s[...] = start_blk * MOE_BLOCK
            run_s[...] = jnp.zeros_like(run_s)
            end_blk = start_blk + nblk
            nb = be_ref.shape[0]
            blk = lax.broadcasted_iota(jnp.int32, (nb, LANES), 0).astype(F32)
            ln = lax.broadcasted_iota(jnp.int32, (nb, LANES), 1)
            is_exp = (ln >= EXP_LANE0) & (ln < EXP_LANE0 + N_EXPERTS)
            done = jnp.sum(jnp.where(is_exp & (end_blk <= blk), 1.0, 0.0), axis=-1, keepdims=True)
            bexp = jnp.minimum(done, N_EXPERTS - 1.0)
            used = jnp.max(jnp.where(is_exp, end_blk, 0.0), axis=-1, keepdims=True)
            be_ref[...] = jnp.where(ln == 0, bexp, jnp.where(ln == 1, used, 0.0)).astype(jnp.int32)

        rr = lax.broadcasted_iota(jnp.int32, (tm, tm), 0)
        cc = lax.broadcasted_iota(jnp.int32, (tm, tm), 1)
        before = (cc < rr).astype(BF16)
        prefix = jnp.dot(before, cnt.astype(BF16), preferred_element_type=F32)
        base = off_s[...] + run_s[...] + prefix
        d1 = jnp.sum(jnp.where(oh1, base, 0.0), axis=-1, keepdims=True)
        d2 = jnp.sum(jnp.where(oh2, base, 0.0), axis=-1, keepdims=True)
        run_s[...] += colsum
        dd = jnp.exp(t2 - t1)
        g1 = p_grp / (1.0 + dd)
        g2 = p_grp * dd / (1.0 + dd)
        dest_ref[...] = jnp.where(lane == 0, d1, jnp.where(lane == 1, d2, 0.0)).astype(jnp.int32)
        gate_ref[...] = jnp.where(lane == 0, g1, jnp.where(lane == 1, g2, 0.0))


def _router_call(logits, n_blocks):
    n = logits.shape[0]
    tm = min(n, 512)
    nb_pad = ((n_blocks + 7) // 8) * 8
    return pl.pallas_call(
        _router_kernel,
        grid=(2, n // tm),
        in_specs=[pl.BlockSpec((tm, LANES), lambda p, i: (i, 0))],
        out_specs=[pl.BlockSpec((tm, LANES), lambda p, i: (i * p, 0)),
                   pl.BlockSpec((tm, LANES), lambda p, i: (i * p, 0)),
                   pl.BlockSpec((nb_pad, LANES), lambda p, i: (0, 0))],
        out_shape=[jax.ShapeDtypeStruct((n, LANES), jnp.int32),
                   jax.ShapeDtypeStruct((n, LANES), F32),
                   jax.ShapeDtypeStruct((nb_pad, LANES), jnp.int32)],
        scratch_shapes=[pltpu.VMEM((1, LANES), F32), pltpu.VMEM((1, LANES), F32),
                        pltpu.VMEM((1, LANES), F32)],
        compiler_params=_cparams(("arbitrary", "arbitrary")),
        name="router",
    )(logits)


def _slots_kernel(dest_ref, zero_ref, inv_ref, sem):
    fill = pltpu.make_async_copy(zero_ref, inv_ref, sem)
    fill.start()
    fill.wait()

    def put(tok, carry):
        inv_ref[dest_ref[2 * tok]] = tok
        inv_ref[dest_ref[2 * tok + 1]] = tok
        return carry

    lax.fori_loop(0, dest_ref.shape[0] // 2, put, 0, unroll=8)


def _slots_call(dest_flat, n_rows):
    return pl.pallas_call(
        _slots_kernel,
        in_specs=[pl.BlockSpec(memory_space=pltpu.SMEM), pl.BlockSpec(memory_space=pl.ANY)],
        out_specs=pl.BlockSpec(memory_space=pltpu.SMEM),
        out_shape=jax.ShapeDtypeStruct((n_rows,), jnp.int32),
        scratch_shapes=[pltpu.SemaphoreType.DMA(())],
        name="slots",
    )(dest_flat, jnp.zeros((n_rows,), jnp.int32))


GMM_GROUP = 6
GMM_TILES = 4


def _unit_tables(block_expert, used, n_blocks):
    n_units_max = N_EXPERTS + n_blocks // GMM_GROUP
    valid = jnp.arange(n_blocks) < used
    nblk_e = jnp.sum((block_expert[None, :] == jnp.arange(N_EXPERTS)[:, None]) & valid[None, :],
                     axis=1).astype(jnp.int32)
    first_e = jnp.cumsum(nblk_e) - nblk_e
    units_e = (nblk_e + GMM_GROUP - 1) // GMM_GROUP
    uend = jnp.cumsum(units_e)
    n_units = uend[-1]
    u = jnp.arange(n_units_max, dtype=jnp.int32)
    ue = jnp.minimum(jnp.sum(uend[None, :] <= u[:, None], axis=1), N_EXPERTS - 1).astype(jnp.int32)
    k = u - (uend - units_e)[ue]
    live = u < n_units
    ub = jnp.where(live, first_e[ue] + k * GMM_GROUP, 0).astype(jnp.int32)
    un = jnp.where(live, jnp.clip(nblk_e[ue] - k * GMM_GROUP, 0, GMM_GROUP), 0).astype(jnp.int32)
    return ue, ub, un, n_units.reshape(1).astype(jnp.int32)


def _gmm_kernel(ue_ref, ub_ref, un_ref, nu_ref, used_ref, inv_ref, h_ref, w1_ref, w3_ref, w2_ref, y_ref,
                xbuf, acc, w1f, w3f, w2f, w1b_s, w3b_s, w2b_s, gsem, osem, wsem, *, n_blocks):
    u = pl.program_id(0)
    n_units = nu_ref[0]
    slot = u % 2
    tj = w1f.shape[2]

    def gather(unit, s):
        base = ub_ref[unit] * MOE_BLOCK

        def body(g, carry):
            for k in range(8):
                r = g * 8 + k
                pltpu.make_async_copy(h_ref.at[pl.ds(inv_ref[base + r], 1)], xbuf.at[s, pl.ds(r, 1)],
                                      gsem.at[s]).start()
            return carry

        lax.fori_loop(0, un_ref[unit] * (MOE_BLOCK // 8), body, 0)

    def wait_gather(unit, s):
        rows = pl.ds(0, un_ref[unit] * MOE_BLOCK)
        pltpu.make_async_copy(xbuf.at[s, rows], xbuf.at[s, rows], gsem.at[s]).wait()

    def blk_rows(i):
        return pl.ds(pl.multiple_of(i * MOE_BLOCK, MOE_BLOCK), MOE_BLOCK)

    def out_copy(unit, s, i):
        dst = pl.ds(pl.multiple_of((ub_ref[unit] + i) * MOE_BLOCK, MOE_BLOCK), MOE_BLOCK)
        return pltpu.make_async_copy(acc.at[s, blk_rows(i)], y_ref.at[dst], osem.at[s])

    def wait_out(unit, s):
        def body(i, carry):
            out_copy(unit, s, i).wait()
            return carry

        lax.fori_loop(0, un_ref[unit], body, 0)

    def weight_copies(unit, j, ws):
        e = ue_ref[unit]
        cols = pl.ds(j * tj, tj)
        return (pltpu.make_async_copy(w1_ref.at[e, :, cols], w1f.at[ws], wsem.at[ws]),
                pltpu.make_async_copy(w3_ref.at[e, :, cols], w3f.at[ws], wsem.at[ws]),
                pltpu.make_async_copy(w2_ref.at[e, cols, :], w2f.at[ws], wsem.at[ws]))

    @pl.when(u == 0)
    def _():
        gather(0, 0)
        for c in weight_copies(0, 0, 0):
            c.start()

    def tile_pass(j, ws):
        first = j == 0
        w1b = w1f[ws].astype(BF16)
        w3b = w3f[ws].astype(BF16)
        w2b = w2f[ws].astype(BF16)
        w1b_s[...] = w1b
        w3b_s[...] = w3b
        w2b_s[...] = w2b

        def rows_pass(r0, m, a1, a3, a2):
            rows = pl.ds(r0, m)
            xb = xbuf[slot, rows, :].astype(BF16)
            h1 = jnp.dot(xb, a1, preferred_element_type=F32)
            h3 = jnp.dot(xb, a3, preferred_element_type=F32)
            part = jnp.dot((_silu(h1) * h3).astype(BF16), a2, preferred_element_type=F32)
            if first:
                acc[slot, rows, :] = part
            else:
                acc[slot, rows, :] += part

        rows_pass(0, MOE_BLOCK, w1b, w3b, w2b)
        rest = un_ref[u] - 1

        def pair(i, carry):
            r0 = pl.multiple_of(MOE_BLOCK + i * 2 * MOE_BLOCK, MOE_BLOCK)
            rows_pass(r0, 2 * MOE_BLOCK, w1b_s[...], w3b_s[...], w2b_s[...])
            return carry

        lax.fori_loop(0, rest // 2, pair, 0)

        @pl.when(rest % 2 == 1)
        def _():
            rows_pass(pl.multiple_of(rest * MOE_BLOCK, MOE_BLOCK), MOE_BLOCK,
                      w1b_s[...], w3b_s[...], w2b_s[...])

    @pl.when(u < n_units)
    def _():
        @pl.when(u >= 2)
        def _():
            wait_out(u - 2, slot)

        @pl.when(u + 1 < n_units)
        def _():
            gather(u + 1, 1 - slot)

        wait_gather(u, slot)
        nxt = jnp.minimum(u + 1, n_units - 1)
        for j in range(GMM_TILES):
            ws = j % 2
            following = weight_copies(u, j + 1, 1 - ws) if j + 1 < GMM_TILES else weight_copies(nxt, 0, 1 - ws)
            for c in following:
                c.start()
            for c in weight_copies(u, j, ws):
                c.wait()
            tile_pass(j, ws)

        def start(i, carry):
            out_copy(u, slot, i).start()
            return carry

        lax.fori_loop(0, un_ref[u], start, 0)

    @pl.when(u == pl.num_programs(0) - 1)
    def _():
        last = n_units - 1
        for c in weight_copies(last, 0, 0):
            c.wait()
        wait_out(last, last % 2)

        @pl.when(n_units >= 2)
        def _():
            wait_out(last - 1, (last - 1) % 2)

        acc[0, blk_rows(0), :] = jnp.zeros((MOE_BLOCK, acc.shape[2]), F32)

        def zero_copy(b):
            dst = pl.ds(pl.multiple_of(b * MOE_BLOCK, MOE_BLOCK), MOE_BLOCK)
            return pltpu.make_async_copy(acc.at[0, blk_rows(0)], y_ref.at[dst], osem.at[0])

        def start(b, carry):
            zero_copy(b).start()
            return carry

        def wait(b, carry):
            zero_copy(b).wait()
            return carry

        lax.fori_loop(used_ref[0], n_blocks, start, 0)
        lax.fori_loop(used_ref[0], n_blocks, wait, 0)


def _gmm_call(block_expert, used, inv, h2, w1, w3, w2):
    n_rows = inv.shape[0]
    d = h2.shape[1]
    de = w1.shape[-1]
    n_blocks = n_rows // MOE_BLOCK
    tj = de // GMM_TILES
    ue, ub, un, nu = _unit_tables(block_expert, used[0], n_blocks)

    assert GMM_TILES % 2 == 0, "weight buffers alternate per tile; an even count keeps tile j in buffer j % 2"
    rows = GMM_GROUP * MOE_BLOCK
    grid_spec = pltpu.PrefetchScalarGridSpec(
        num_scalar_prefetch=6,
        grid=(ue.shape[0],),
        in_specs=[pl.BlockSpec(memory_space=pl.ANY)] * 4,
        out_specs=pl.BlockSpec(memory_space=pl.ANY),
        scratch_shapes=[pltpu.VMEM((2, rows, d), F32), pltpu.VMEM((2, rows, d), F32),
                        pltpu.VMEM((2, d, tj), F32), pltpu.VMEM((2, d, tj), F32), pltpu.VMEM((2, tj, d), F32),
                        pltpu.VMEM((d, tj), BF16), pltpu.VMEM((d, tj), BF16), pltpu.VMEM((tj, d), BF16),
                        pltpu.SemaphoreType.DMA((2,)), pltpu.SemaphoreType.DMA((2,)),
                        pltpu.SemaphoreType.DMA((2,))],
    )
    return pl.pallas_call(
        functools.partial(_gmm_kernel, n_blocks=n_blocks),
        grid_spec=grid_spec,
        out_shape=jax.ShapeDtypeStruct((n_rows, d), F32),
        compiler_params=_cparams(("arbitrary",)),
        name="gmm",
    )(ue, ub, un, nu, used, inv, h2, w1, w3, w2)


def _combine_kernel(dest_ref, y_ref, x1_ref, gate_ref, mods_ref, fg_ref, o_ref, ybuf, sem, *, seq):
    i = pl.program_id(0)
    tm = x1_ref.shape[0]
    slot = i % 2

    last = pl.num_programs(0) - 1

    def row_copy(step, r, k, s):
        return pltpu.make_async_copy(y_ref.at[pl.ds(dest_ref[2 * (step * tm + r) + k], 1)],
                                     ybuf.at[s, k, pl.ds(r, 1)], sem.at[s])

    def wait_gather(s):
        pltpu.make_async_copy(ybuf.at[s], ybuf.at[s], sem.at[s]).wait()

    @pl.when(i == 0)
    def _():
        def body(r, carry):
            row_copy(0, r, 0, 0).start()
            row_copy(0, r, 1, 0).start()
            return carry

        lax.fori_loop(0, tm, body, 0, unroll=4)

    wait_gather(slot)

    nxt = jnp.minimum(i + 1, last)
    for r in range(tm):
        row_copy(nxt, r, 0, 1 - slot).start()
        row_copy(nxt, r, 1, 1 - slot).start()

    b = (i * tm) // seq
    gt2 = mods_ref[5, pl.ds(b, 1), :]
    gate = gate_ref[...]
    moe = gate[:, 0:1] * ybuf[slot, 0] + gate[:, 1:2] * ybuf[slot, 1]
    x = x1_ref[...] + gt2 * moe
    o_ref[...] = x * lax.rsqrt(jnp.mean(x * x, axis=-1, keepdims=True) + EPS) * fg_ref[...]

    @pl.when(i == last)
    def _():
        wait_gather(1 - slot)


def _combine_call(dest_flat, y_pad, x1, gate, mods, final_g, seq):
    n, d = x1.shape
    tm = min(seq, 256)
    grid_spec = pltpu.PrefetchScalarGridSpec(
        num_scalar_prefetch=1,
        grid=(n // tm,),
        in_specs=[pl.BlockSpec(memory_space=pl.ANY),
                  pl.BlockSpec((tm, d), lambda i, dest: (i, 0)),
                  pl.BlockSpec((tm, LANES), lambda i, dest: (i, 0)),
                  pl.BlockSpec((6, 8, d), lambda i, dest: (0, 0, 0)),
                  pl.BlockSpec((1, d), lambda i, dest: (0, 0))],
        out_specs=pl.BlockSpec((tm, d), lambda i, dest: (i, 0)),
        scratch_shapes=[pltpu.VMEM((2, 2, tm, d), F32), pltpu.SemaphoreType.DMA((2,))],
    )
    return pl.pallas_call(
        functools.partial(_combine_kernel, seq=seq),
        grid_spec=grid_spec,
        out_shape=jax.ShapeDtypeStruct((n, d), F32),
        compiler_params=_cparams(("arbitrary",)),
        name="combine",
    )(dest_flat, y_pad, x1, gate, mods, final_g)


def _pad_lanes(a, lane0):
    return jnp.zeros((LANES,), F32).at[lane0:lane0 + a.shape[0]].set(a.astype(F32))


def kernel(x, c, ctx, c_ctx, w_mod, b_mod, norm1_g, w_in, pool_w, pool_scale, conv_w,
           a_log_f, dt_bias_f, a_log_b, dt_bias_b, out_norm_g, w_out, norm2_g,
           w_grp, b_grp, w_rt, b_rt, w1, w3, w2, final_g):
    bsz, t, d = x.shape
    depth = w_mod.shape[0]
    assert depth == 1, "single-layer problem: the context stream is read but never updated"
    heads = a_log_f.shape[1]
    pool_width = pool_w.shape[1] * pool_w.shape[2]
    dn_width = heads * HEAD_DIM
    q0 = pool_width
    z0 = q0 + 3 * dn_width
    ab0 = z0 + dn_width
    n_tok = bsz * t
    l = 0

    c8 = jnp.zeros((8, d), F32).at[:bsz].set(c).at[bsz].set(c_ctx)
    mod = _mod_call(c8, w_mod[l], b_mod[l])
    mods = mod.reshape(8, 6, d).transpose(1, 0, 2)

    w_in_bf = w_in[l].astype(BF16)
    wab_bf = jnp.zeros((d, LANES), BF16).at[:, :4 * heads].set(w_in_bf[:, ab0:])
    g1 = norm1_g[l].reshape(1, d)
    proj, ab = _inproj_call(x, mods, g1, w_in_bf, wab_bf, ab0, None)
    proj_c, ab_c = _inproj_call(ctx, mods, g1, w_in_bf, wab_bf, ab0, bsz)

    prm = jnp.zeros((8, LANES), F32)
    prm = prm.at[0].set(_pad_lanes(jnp.concatenate([a_log_f[l], a_log_b[l]]), 2 * heads))
    prm = prm.at[1].set(_pad_lanes(jnp.concatenate([dt_bias_f[l], dt_bias_b[l]]), 2 * heads))
    g, g_t = _gates_call(ab, prm, heads)
    gc, gc_t = _gates_call(ab_c, prm, heads)
    gr = g_t.reshape(bsz, LANES, t // CHUNK, CHUNK)
    grc = gc_t.reshape(bsz, LANES, ctx.shape[1] // CHUNK, CHUNK)

    dn = _delta_call(proj, proj_c, conv_w[l], g, gc, gr, grc, out_norm_g[l].reshape(1, HEAD_DIM),
                     heads, q0 // HEAD_DIM, z0 // HEAD_DIM)
    pool = _pool_call(proj, pool_w[l], pool_scale[l].reshape(1, pool_width), t // GRID_W, GRID_W)

    wr = jnp.zeros((d, LANES), F32).at[:, :N_GROUPS].set(w_grp[l]).at[:, EXP_LANE0:EXP_LANE0 + N_EXPERTS].set(w_rt[l])
    br = jnp.zeros((LANES,), F32).at[:N_GROUPS].set(b_grp[l]).at[EXP_LANE0:EXP_LANE0 + N_EXPERTS].set(b_rt[l])
    x1, h2, logits = _outproj_call(pool, dn, w_out[l].astype(BF16), x, mods, norm2_g[l].reshape(1, d),
                                   wr, br.reshape(1, LANES))

    n_blocks = (n_tok * 2 + N_EXPERTS * (MOE_BLOCK - 1) + MOE_BLOCK - 1) // MOE_BLOCK
    dest, gate, be = _router_call(logits.reshape(n_tok, LANES), n_blocks)
    dest_flat = dest[:, :2].reshape(-1)
    block_expert = be[:n_blocks, 0]
    used = be[0:1, 1]
    inv = _slots_call(dest_flat, n_blocks * MOE_BLOCK)
    y_pad = _gmm_call(block_expert, used, inv, h2.reshape(n_tok, d), w1[l], w3[l], w2[l])
    out = _combine_call(dest_flat, y_pad, x1.reshape(n_tok, d), gate, mods, final_g.reshape(1, d), t)
    return out.reshape(bsz, t, d)
```

```python
import functools

import jax
import jax.numpy as jnp
from jax import lax
from jax.experimental import pallas as pl
from jax.experimental.pallas import tpu as pltpu

F32 = jnp.float32
BF16 = jnp.bfloat16

GRID_W = 64
POOL_WINDOWS = (2, 4, 8, 16)
HEAD_DIM = 128
CONV_WIDTH = 5
CHUNK = 64
N_GROUPS = 4
EXPERTS_PER_GROUP = 8
N_EXPERTS = N_GROUPS * EXPERTS_PER_GROUP
MOE_BLOCK = 128
EPS = 1e-6
LANES = 128
PRE_UNROLL = 8
EXP_LANE0 = N_GROUPS

VMEM_LIMIT = 56 * 1024 * 1024


def _cparams(sem):
    return pltpu.CompilerParams(dimension_semantics=sem, vmem_limit_bytes=VMEM_LIMIT)


def _dot(a, b):
    return jnp.dot(a.astype(BF16), b.astype(BF16), preferred_element_type=F32)


def _dot_split(a, b):
    a_hi = a.astype(BF16)
    b_hi = b.astype(BF16)
    a_lo = (a - a_hi.astype(F32)).astype(BF16)
    b_lo = (b - b_hi.astype(F32)).astype(BF16)
    return (jnp.dot(a_hi, b_hi, preferred_element_type=F32)
            + jnp.dot(a_lo, b_hi, preferred_element_type=F32)
            + jnp.dot(a_hi, b_lo, preferred_element_type=F32))


def _silu(x):
    return x * jax.nn.sigmoid(x)


def _mod_kernel(c_ref, w_ref, b_ref, o_ref):
    o_ref[...] = _dot(_silu(c_ref[...]), w_ref[...]) + b_ref[...]


def _mod_call(c8, w_mod, b_mod):
    d, n = w_mod.shape
    tn = 512
    return pl.pallas_call(
        _mod_kernel,
        grid=(n // tn,),
        in_specs=[pl.BlockSpec((8, d), lambda j: (0, 0)),
                  pl.BlockSpec((d, tn), lambda j: (0, j)),
                  pl.BlockSpec((1, tn), lambda j: (0, j))],
        out_specs=pl.BlockSpec((8, tn), lambda j: (0, j)),
        out_shape=jax.ShapeDtypeStruct((8, n), F32),
        compiler_params=_cparams(("parallel",)),
        name="mod",
    )(c8, w_mod, b_mod.reshape(1, n))


def _inproj_kernel(x_ref, mods_ref, g_ref, w_ref, wab_ref, o_ref, ab_ref, hn_ref, *, mod_row):
    b = pl.program_id(0)
    j = pl.program_id(2)

    @pl.when(j == 0)
    def _():
        x = x_ref[0]
        y = x * lax.rsqrt(jnp.mean(x * x, axis=-1, keepdims=True) + EPS) * g_ref[...]
        row = b if mod_row is None else mod_row
        sh = mods_ref[0, pl.ds(row, 1), :]
        sc = mods_ref[1, pl.ds(row, 1), :]
        hb = (y * (1 + sc) + sh).astype(BF16)
        hn_ref[...] = hb
        ab_ref[0] = jnp.dot(hb, wab_ref[...], preferred_element_type=F32)

    o_ref[0] = jnp.dot(hn_ref[...], w_ref[...], preferred_element_type=F32)


def _inproj_call(x, mods, g, w_in, wab_bf, n_main, mod_row):
    bsz, t, d = x.shape
    tm = min(t, 1024)
    tn = 1024
    kern = functools.partial(_inproj_kernel, mod_row=mod_row)
    return pl.pallas_call(
        kern,
        grid=(bsz, t // tm, n_main // tn),
        in_specs=[pl.BlockSpec((1, tm, d), lambda b, i, j: (b, i, 0)),
                  pl.BlockSpec((2, 8, d), lambda b, i, j: (0, 0, 0)),
                  pl.BlockSpec((1, d), lambda b, i, j: (0, 0)),
                  pl.BlockSpec((d, tn), lambda b, i, j: (0, j)),
                  pl.BlockSpec((d, LANES), lambda b, i, j: (0, 0))],
        out_specs=[pl.BlockSpec((1, tm, tn), lambda b, i, j: (b, i, j)),
                   pl.BlockSpec((1, tm, LANES), lambda b, i, j: (b, i, 0))],
        out_shape=[jax.ShapeDtypeStruct((bsz, t, n_main), F32),
                   jax.ShapeDtypeStruct((bsz, t, LANES), F32)],
        scratch_shapes=[pltpu.VMEM((tm, d), BF16)],
        compiler_params=_cparams(("parallel", "parallel", "arbitrary")),
        name="inproj",
    )(x, mods, g, w_in, wab_bf)


def _gates_kernel(ab_ref, prm_ref, g_ref, gt_ref, *, heads):
    ab = ab_ref[0]
    t = ab.shape[0]
    h2, h3, h4, h6 = 2 * heads, 3 * heads, 4 * heads, 6 * heads
    beta = jax.nn.sigmoid(ab)
    xx = ab + prm_ref[1:2, :]
    softplus = jnp.maximum(xx, 0.0) + jnp.log1p(jnp.exp(-jnp.abs(xx)))
    g = -jnp.exp(prm_ref[0:1, :]) * softplus
    pos = lax.broadcasted_iota(jnp.int32, ab.shape, 0) & (CHUNK - 1)
    cs = g
    ss = g
    s = 1
    while s < CHUNK:
        cs = cs + jnp.where(pos >= s, pltpu.roll(cs, s, 0), 0.0)
        ss = ss + jnp.where(pos < CHUNK - s, pltpu.roll(ss, t - s, 0), 0.0)
        s *= 2
    tot = pltpu.roll(cs + ss - g, h2, 1)
    lane = lax.broadcasted_iota(jnp.int32, ab.shape, 1)
    out = jnp.where(lane < h2, beta,
                    jnp.where(lane < h3, cs,
                              jnp.where(lane < h4, ss,
                                        jnp.where(lane < h6, tot, 0.0))))
    g_ref[0] = out
    gt_ref[0] = out.T


def _gates_call(ab, prm, heads):
    bsz, t, _ = ab.shape
    return pl.pallas_call(
        functools.partial(_gates_kernel, heads=heads),
        grid=(bsz,),
        in_specs=[pl.BlockSpec((1, t, LANES), lambda b: (b, 0, 0)),
                  pl.BlockSpec((8, LANES), lambda b: (0, 0))],
        out_specs=[pl.BlockSpec((1, t, LANES), lambda b: (b, 0, 0)),
                   pl.BlockSpec((1, LANES, t), lambda b: (b, 0, 0))],
        out_shape=[jax.ShapeDtypeStruct((bsz, t, LANES), F32),
                   jax.ShapeDtypeStruct((bsz, LANES, t), F32)],
        compiler_params=_cparams(("parallel",)),
        name="gates",
    )(ab, prm)


def _conv_silu(x, w):
    n = x.shape[0]
    row = lax.broadcasted_iota(jnp.int32, x.shape, 0)
    acc = x * w[CONV_WIDTH // 2:CONV_WIDTH // 2 + 1, :]
    for j in range(CONV_WIDTH):
        d = j - CONV_WIDTH // 2
        if d == 0:
            continue
        xs = pltpu.roll(x, (-d) % n, 0)
        valid = (row + d >= 0) & (row + d < n)
        acc = acc + jnp.where(valid, xs, 0.0) * w[j:j + 1, :]
    return _silu(acc)


def _l2norm(a):
    return a * lax.rsqrt(jnp.sum(a * a, axis=-1, keepdims=True) + EPS)


def _lane_col(g, lane_idx):
    lane = lax.broadcasted_iota(jnp.int32, g.shape, 1)
    return jnp.sum(jnp.where(lane == lane_idx, g, 0.0), axis=-1, keepdims=True)


def _chunk_terms(chains, between=()):
    pending = list(between)

    def stage_done():
        if pending:
            pending.pop(0)()

    c, hd = chains[0][0].shape
    ri = lax.broadcasted_iota(jnp.int32, (c, c), 0)
    ci = lax.broadcasted_iota(jnp.int32, (c, c), 1)
    eye = jnp.where(ri == ci, 1.0, 0.0)
    right = lax.broadcasted_iota(jnp.int32, (c, 2 * c), 1) >= c
    nt = (((1,), (1,)), ((), ()))
    tn = (((0,), (0,)), ((), ()))

    decs, kn_bs, zs = [], [], []
    for kn_c, kb_c, _, _, _, _, gcc, gcr, upper in chains:
        incl = (ri <= ci) if upper else (ri >= ci)
        strict = (ri < ci) if upper else (ri > ci)
        dec = jnp.where(incl, jnp.exp(jnp.where(incl, gcc - gcr, 0.0)), 0.0)
        kn_b = kn_c.astype(BF16)
        kk = lax.dot_general(kb_c.astype(BF16), kn_b, nt, preferred_element_type=F32)
        decs.append(dec)
        kn_bs.append(kn_b)
        zs.append(jnp.concatenate([-jnp.where(strict, kk * dec, 0.0), eye], axis=1))
    stage_done()
    n = 1
    while n < c:
        zs = [_dot(z[:, :c], z) + jnp.where(right, z, 0.0) for z in zs]
        stage_done()
        n *= 2
    uw_bs = [_dot(z[:, c:], ch[3]).astype(BF16) for z, ch in zip(zs, chains)]
    stage_done()
    wns = [lax.dot_general(ch[4].astype(BF16), uw_b, tn, preferred_element_type=F32)
           for uw_b, ch in zip(uw_bs, chains)]
    stage_done()
    qks = [None if ch[2] is None else
           lax.dot_general(ch[2].astype(BF16), kn_b, nt, preferred_element_type=F32) * dec
           for ch, kn_b, dec in zip(chains, kn_bs, decs)]
    stage_done()
    qws = [None if qk is None else jnp.dot(qk.astype(BF16), uw_b, preferred_element_type=F32)
           for qk, uw_b in zip(qks, uw_bs)]
    while pending:
        stage_done()
    out = []
    for ch, wn, qw in zip(chains, wns, qws):
        nc, w2 = wn[:, :hd], wn[:, hd:]
        if qw is None:
            out.append((w2, nc, None, None))
        else:
            out.append((w2, nc, ch[5] - qw[:, hd:], qw[:, :hd]))
    return out


def _delta_kernel(q_ref, k_ref, v_ref, z_ref, kc_ref, vc_ref, cwq_ref, cwk_ref, cwv_ref,
                  g_ref, gc_ref, grf_ref, grb_ref, grcf_ref, grcb_ref, ong_ref,
                  out_ref,
                  qn_s, kn_s, kb_s, rhs_s, kd_s, qd_s, col_s,
                  knc_s, kbc_s, rhsc_s, kdc_s, colc_s,
                  w2_s, nc_s, qp_s, o0_s, w2c_s, ncc_s, o_s, *, heads):
    h = pl.program_id(1)
    t = q_ref.shape[1]
    tc = kc_ref.shape[1]
    hd = HEAD_DIM
    n_lat = t // CHUNK
    n_ctx = tc // CHUNK

    qn = _l2norm(_conv_silu(q_ref[0], cwq_ref[...])) * (hd ** -0.5)
    kn = _l2norm(_conv_silu(k_ref[0], cwk_ref[...]))
    vv = _conv_silu(v_ref[0], cwv_ref[...])
    knc = _l2norm(_conv_silu(kc_ref[0], cwk_ref[...]))
    vvc = _conv_silu(vc_ref[0], cwv_ref[...])
    qn_s[...] = qn
    kn_s[...] = kn
    knc_s[...] = knc
    g_lat = g_ref[0]
    g_ctx = gc_ref[0]

    for d in range(2):
        for (gt_, kn_, vv_, kb_r, rhs_r, kd_r, col_r, qn_, qd_r) in (
                (g_lat, kn, vv, kb_s, rhs_s, kd_s, col_s, qn, qd_s),
                (g_ctx, knc, vvc, kbc_s, rhsc_s, kdc_s, colc_s, None, None)):
            beta = _lane_col(gt_, d * heads + h)
            gcum = _lane_col(gt_, (2 + d) * heads + h)
            gtot = _lane_col(gt_, (4 + d) * heads + h)
            e = jnp.exp(gcum)
            kb = kn_ * beta
            kb_r[d] = kb
            rhs_r[d, :, 0:hd] = vv_ * beta
            rhs_r[d, :, hd:2 * hd] = kb * e
            kd_r[d] = kn_ * jnp.exp(gtot - gcum)
            lane = lax.broadcasted_iota(jnp.int32, (gcum.shape[0], LANES), 1)
            col_r[d] = jnp.where(lane == 0, gcum, jnp.where(lane == 1, jnp.exp(gtot), 0.0))
            if qn_ is not None:
                qd_r[d] = qn_ * e

    gr_lat = (grf_ref, grb_ref)
    gr_ctx = (grcf_ref, grcb_ref)

    un_ctx = min(PRE_UNROLL, n_ctx)
    un_lat = min(PRE_UNROLL, n_lat)

    def pre_ctx(i, carry):
        ids = [(i * un_ctx + u, d) for u in range(un_ctx) for d in range(2)]
        chains = []
        for c, d in ids:
            rows = pl.ds(pl.multiple_of(c * CHUNK, CHUNK), CHUNK)
            chains.append((knc_s[rows, :], kbc_s[d, rows, :], None, rhsc_s[d, rows, :], kdc_s[d, rows, :],
                           None, colc_s[d, rows, 0:1], gr_ctx[d][0, 0, pl.ds(c, 1), :], d == 1))
        for (c, d), (w2, nc, _, _) in zip(ids, _chunk_terms(chains)):
            m0 = pl.multiple_of(c * hd, hd)
            w2c_s[d, pl.ds(m0, hd), :] = w2
            ncc_s[d, pl.ds(m0, hd), :] = nc
        return carry

    lax.fori_loop(0, n_ctx // un_ctx, pre_ctx, 0)

    def lat_chunk(i, u, d):
        k = i * un_lat + u
        return k if d == 0 else n_lat - 1 - k

    def pre_lat(i, between=()):
        ids = [(lat_chunk(i, u, d), d) for u in range(un_lat) for d in range(2)]
        chains = []
        for c, d in ids:
            rows = pl.ds(pl.multiple_of(c * CHUNK, CHUNK), CHUNK)
            chains.append((kn_s[rows, :], kb_s[d, rows, :], qn_s[rows, :], rhs_s[d, rows, :], kd_s[d, rows, :],
                           qd_s[d, rows, :], col_s[d, rows, 0:1], gr_lat[d][0, 0, pl.ds(c, 1), :], d == 1))
        for (c, d), (w2, nc, qp, o0) in zip(ids, _chunk_terms(chains, between)):
            rows = pl.ds(pl.multiple_of(c * CHUNK, CHUNK), CHUNK)
            m0 = pl.multiple_of(c * hd, hd)
            w2_s[d, pl.ds(m0, hd), :] = w2
            nc_s[d, pl.ds(m0, hd), :] = nc
            qp_s[d, rows, :] = qp
            o0_s[d, rows, :] = o0

    def scan_ctx_steps(box):
        def step(i):
            def run():
                for d in range(2):
                    c = i if d == 0 else n_ctx - 1 - i
                    gt = colc_s[d, c * CHUNK:c * CHUNK + 1, 1:2]
                    box[d] = (gt * box[d] + ncc_s[d, c * hd:(c + 1) * hd, :]
                              - _dot(w2c_s[d, c * hd:(c + 1) * hd, :], box[d]))
            return run

        return [step(i) for i in range(n_ctx)]

    def scan_lat_steps(i, box):
        def step(u):
            def run():
                for d in range(2):
                    s = box[d]
                    c = lat_chunk(i, u, d)
                    r0 = pl.multiple_of(c * CHUNK, CHUNK)
                    rows = pl.ds(r0, CHUNK)
                    m0 = pl.multiple_of(c * hd, hd)
                    s_b = s.astype(BF16)
                    o_s[d, rows, :] = jnp.dot(qp_s[d, rows, :].astype(BF16), s_b,
                                              preferred_element_type=F32) + o0_s[d, rows, :]
                    gt = col_s[d, pl.ds(r0, 1), 1:2]
                    box[d] = gt * s + nc_s[d, pl.ds(m0, hd), :] - jnp.dot(
                        w2_s[d, pl.ds(m0, hd), :].astype(BF16), s_b, preferred_element_type=F32)
            return run

        return [step(u) for u in range(un_lat)]

    zero = jnp.zeros((hd, hd), F32)
    box = [zero, zero]
    pre_lat(0, scan_ctx_steps(box))
    states = tuple(box)

    def lat_body(i, states):
        box = list(states)
        pre_lat(i, scan_lat_steps(i - 1, box))
        return tuple(box)

    states = lax.fori_loop(1, n_lat // un_lat, lat_body, states)
    box = list(states)
    for run in scan_lat_steps(n_lat // un_lat - 1, box):
        run()

    o = o_s[0] + o_s[1]
    o = o * lax.rsqrt(jnp.mean(o * o, axis=-1, keepdims=True) + EPS) * ong_ref[...]
    out_ref[0] = (o * _silu(z_ref[0])).astype(out_ref.dtype)


def _delta_call(proj, proj_c, conv_w, g, gc, gr, grc, ong, heads, q_blk0, z_blk0):
    bsz, t, _ = proj.shape
    tc = proj_c.shape[1]
    hd = HEAD_DIM
    n_lat, n_ctx = t // CHUNK, tc // CHUNK

    def col(off):
        return lambda b, h: (b, 0, off + h)

    def cw(off):
        return lambda b, h: (0, off + h)

    def grow(off):
        return lambda b, h: (b, off + h, 0, 0)

    in_specs = [
        pl.BlockSpec((1, t, hd), col(q_blk0)),
        pl.BlockSpec((1, t, hd), col(q_blk0 + heads)),
        pl.BlockSpec((1, t, hd), col(q_blk0 + 2 * heads)),
        pl.BlockSpec((1, t, hd), col(z_blk0)),
        pl.BlockSpec((1, tc, hd), col(q_blk0 + heads)),
        pl.BlockSpec((1, tc, hd), col(q_blk0 + 2 * heads)),
        pl.BlockSpec((CONV_WIDTH, hd), cw(0)),
        pl.BlockSpec((CONV_WIDTH, hd), cw(heads)),
        pl.BlockSpec((CONV_WIDTH, hd), cw(2 * heads)),
        pl.BlockSpec((1, t, LANES), lambda b, h: (b, 0, 0)),
        pl.BlockSpec((1, tc, LANES), lambda b, h: (b, 0, 0)),
        pl.BlockSpec((1, 1, n_lat, CHUNK), grow(2 * heads)),
        pl.BlockSpec((1, 1, n_lat, CHUNK), grow(3 * heads)),
        pl.BlockSpec((1, 1, n_ctx, CHUNK), grow(2 * heads)),
        pl.BlockSpec((1, 1, n_ctx, CHUNK), grow(3 * heads)),
        pl.BlockSpec((1, hd), lambda b, h: (0, 0)),
    ]
    scratch = [
        pltpu.VMEM((t, hd), F32), pltpu.VMEM((t, hd), F32),
        pltpu.VMEM((2, t, hd), F32), pltpu.VMEM((2, t, 2 * hd), F32),
        pltpu.VMEM((2, t, hd), F32), pltpu.VMEM((2, t, hd), F32),
        pltpu.VMEM((2, t, LANES), F32),
        pltpu.VMEM((tc, hd), F32), pltpu.VMEM((2, tc, hd), F32),
        pltpu.VMEM((2, tc, 2 * hd), F32), pltpu.VMEM((2, tc, hd), F32),
        pltpu.VMEM((2, tc, LANES), F32),
        pltpu.VMEM((2, n_lat * hd, hd), F32), pltpu.VMEM((2, n_lat * hd, hd), F32),
        pltpu.VMEM((2, t, hd), F32), pltpu.VMEM((2, t, hd), F32),
        pltpu.VMEM((2, n_ctx * hd, hd), F32), pltpu.VMEM((2, n_ctx * hd, hd), F32),
        pltpu.VMEM((2, t, hd), F32),
    ]
    return pl.pallas_call(
        functools.partial(_delta_kernel, heads=heads),
        grid=(bsz, heads),
        in_specs=in_specs,
        out_specs=pl.BlockSpec((1, t, hd), lambda b, h: (b, 0, h)),
        out_shape=jax.ShapeDtypeStruct((bsz, t, heads * hd), BF16),
        scratch_shapes=scratch,
        compiler_params=_cparams(("parallel", "parallel")),
        name="delta",
    )(proj, proj, proj, proj, proj_c, proj_c, conv_w, conv_w, conv_w, g, gc, gr, gr, grc, grc, ong)


def _shift_rows(x, d, idx, size, stride):
    n = x.shape[0]
    xs = pltpu.roll(x, (-d * stride) % n, 0)
    return jnp.where((idx + d >= 0) & (idx + d < size), xs, 0.0)


def _box_sum_1d(x, win, idx, size, stride):
    m = win // 2
    lead = x
    trail = x
    k = 1
    while k < m:
        lead = lead + _shift_rows(lead, k, idx, size, stride)
        trail = trail + _shift_rows(trail, -k, idx, size, stride)
        k *= 2
    return lead + _shift_rows(trail, -1, idx, size, stride)


def _pool_kernel(u_ref, pw_ref, ps_ref, o_ref, *, rows, cols):
    t = u_ref.shape[1]
    gc = pw_ref.shape[1]
    tok = lax.broadcasted_iota(jnp.int32, (t, gc), 0)
    ci = tok % cols
    ri = tok // cols
    for gi, win in enumerate(POOL_WINDOWS):
        lo = win // 2
        hi = win - lo
        u = u_ref[0, :, gi * gc:(gi + 1) * gc]
        s = _box_sum_1d(u, win, ci, cols, 1)
        s = _box_sum_1d(s, win, ri, rows, cols)
        cnt_c = jnp.minimum(ci + hi, cols) - jnp.maximum(ci - lo, 0)
        cnt_r = jnp.minimum(ri + hi, rows) - jnp.maximum(ri - lo, 0)
        mean = s / (cnt_c * cnt_r).astype(F32)
        y = _dot(mean - u, pw_ref[gi]) * ps_ref[:, gi * gc:(gi + 1) * gc]
        o_ref[0, :, gi * gc:(gi + 1) * gc] = y.astype(o_ref.dtype)


def _pool_call(proj, pool_w, pool_scale, rows, cols):
    bsz, t, _ = proj.shape
    ng, gc, _ = pool_w.shape
    pwid = ng * gc
    return pl.pallas_call(
        functools.partial(_pool_kernel, rows=rows, cols=cols),
        grid=(bsz,),
        in_specs=[pl.BlockSpec((1, t, pwid), lambda b: (b, 0, 0)),
                  pl.BlockSpec((ng, gc, gc), lambda b: (0, 0, 0)),
                  pl.BlockSpec((1, pwid), lambda b: (0, 0))],
        out_specs=pl.BlockSpec((1, t, pwid), lambda b: (b, 0, 0)),
        out_shape=jax.ShapeDtypeStruct((bsz, t, pwid), BF16),
        compiler_params=_cparams(("parallel",)),
        name="pool",
    )(proj, pool_w, pool_scale)


def _outproj_kernel(pool_ref, dn_ref, wa_ref, wb_ref, x_ref, mods_ref, g2_ref, wr_ref, br_ref,
                    x1_ref, h2_ref, lg_ref):
    b = pl.program_id(0)
    mix = (jnp.dot(pool_ref[0], wa_ref[...], preferred_element_type=F32)
           + jnp.dot(dn_ref[0], wb_ref[...], preferred_element_type=F32))
    gt1 = mods_ref[2, pl.ds(b, 1), :]
    sh2 = mods_ref[3, pl.ds(b, 1), :]
    sc2 = mods_ref[4, pl.ds(b, 1), :]
    x1 = x_ref[0] + gt1 * mix
    x1_ref[0] = x1
    y = x1 * lax.rsqrt(jnp.mean(x1 * x1, axis=-1, keepdims=True) + EPS) * g2_ref[...]
    h2 = y * (1 + sc2) + sh2
    h2_ref[0] = h2
    lg_ref[0] = _dot_split(h2, wr_ref[...]) + br_ref[...]


def _outproj_call(pool, dn, w_out_bf, x, mods, g2, wr, br):
    bsz, t, d = x.shape
    half = pool.shape[-1]
    tm = min(t, 512)
    return pl.pallas_call(
        _outproj_kernel,
        grid=(bsz, t // tm),
        in_specs=[pl.BlockSpec((1, tm, half), lambda b, i: (b, i, 0)),
                  pl.BlockSpec((1, tm, half), lambda b, i: (b, i, 0)),
                  pl.BlockSpec((half, d), lambda b, i: (0, 0)),
                  pl.BlockSpec((half, d), lambda b, i: (1, 0)),
                  pl.BlockSpec((1, tm, d), lambda b, i: (b, i, 0)),
                  pl.BlockSpec((6, 8, d), lambda b, i: (0, 0, 0)),
                  pl.BlockSpec((1, d), lambda b, i: (0, 0)),
                  pl.BlockSpec((d, LANES), lambda b, i: (0, 0)),
                  pl.BlockSpec((1, LANES), lambda b, i: (0, 0))],
        out_specs=[pl.BlockSpec((1, tm, d), lambda b, i: (b, i, 0)),
                   pl.BlockSpec((1, tm, d), lambda b, i: (b, i, 0)),
                   pl.BlockSpec((1, tm, LANES), lambda b, i: (b, i, 0))],
        out_shape=[jax.ShapeDtypeStruct((bsz, t, d), F32),
                   jax.ShapeDtypeStruct((bsz, t, d), F32),
                   jax.ShapeDtypeStruct((bsz, t, LANES), F32)],
        compiler_params=_cparams(("parallel", "parallel")),
        name="outproj",
    )(pool, dn, w_out_bf, w_out_bf, x, mods, g2, wr, br)


def _router_kernel(lg_ref, dest_ref, gate_ref, be_ref, cnt_s, run_s, off_s):
    p = pl.program_id(0)
    i = pl.program_id(1)
    lg = lg_ref[...]
    tm = lg.shape[0]
    lane = lax.broadcasted_iota(jnp.int32, lg.shape, 1)
    lane_f = lane.astype(F32)
    neg = -jnp.inf
    big = float(LANES)

    grp = jnp.where(lane < N_GROUPS, lg, neg)
    gmax = jnp.max(grp, axis=-1, keepdims=True)
    gidx = jnp.min(jnp.where(grp == gmax, lane_f, big), axis=-1, keepdims=True)
    p_grp = 1.0 / jnp.sum(jnp.where(lane < N_GROUPS, jnp.exp(lg - gmax), 0.0), axis=-1, keepdims=True)
    lo = EXP_LANE0 + EXPERTS_PER_GROUP * gidx
    ev = jnp.where((lane_f >= lo) & (lane_f < lo + EXPERTS_PER_GROUP), lg, neg)
    t1 = jnp.max(ev, axis=-1, keepdims=True)
    i1 = jnp.min(jnp.where(ev == t1, lane_f, big), axis=-1, keepdims=True)
    ev2 = jnp.where(lane_f == i1, neg, ev)
    t2 = jnp.max(ev2, axis=-1, keepdims=True)
    i2 = jnp.min(jnp.where(ev2 == t2, lane_f, big), axis=-1, keepdims=True)
    oh1 = lane_f == i1
    oh2 = lane_f == i2
    cnt = oh1.astype(F32) + oh2.astype(F32)
    colsum = jnp.sum(cnt, axis=0, keepdims=True)

    @pl.when(p == 0)
    def _():
        @pl.when(i == 0)
        def _():
            cnt_s[...] = jnp.zeros_like(cnt_s)

        cnt_s[...] += colsum

    @pl.when(p == 1)
    def _():
        @pl.when(i == 0)
        def _():
            nblk = jnp.floor((cnt_s[...] + (MOE_BLOCK - 1)) * (1.0 / MOE_BLOCK))
            r = lax.broadcasted_iota(jnp.int32, (LANES, LANES), 0)
            c = lax.broadcasted_iota(jnp.int32, (LANES, LANES), 1)
            tri = (r < c).astype(BF16)
            nb8 = jnp.broadcast_to(nblk, (8, LANES))
            start_blk = jnp.dot(nb8.astype(BF16), tri, preferred_element_type=F32)[0:1, :]
            off_s[...] = start_blk * MOE_BLOCK
            run_s[...] = jnp.zeros_like(run_s)
            end_blk = start_blk + nblk
            nb = be_ref.shape[0]
            blk = lax.broadcasted_iota(jnp.int32, (nb, LANES), 0).astype(F32)
            ln = lax.broadcasted_iota(jnp.int32, (nb, LANES), 1)
            is_exp = (ln >= EXP_LANE0) & (ln < EXP_LANE0 + N_EXPERTS)
            done = jnp.sum(jnp.where(is_exp & (end_blk <= blk), 1.0, 0.0), axis=-1, keepdims=True)
            bexp = jnp.minimum(done, N_EXPERTS - 1.0)
            used = jnp.max(jnp.where(is_exp, end_blk, 0.0), axis=-1, keepdims=True)
            be_ref[...] = jnp.where(ln == 0, bexp, jnp.where(ln == 1, used, 0.0)).astype(jnp.int32)

        rr = lax.broadcasted_iota(jnp.int32, (tm, tm), 0)
        cc = lax.broadcasted_iota(jnp.int32, (tm, tm), 1)
        before = (cc < rr).astype(BF16)
        prefix = jnp.dot(before, cnt.astype(BF16), preferred_element_type=F32)
        base = off_s[...] + run_s[...] + prefix
        d1 = jnp.sum(jnp.where(oh1, base, 0.0), axis=-1, keepdims=True)
        d2 = jnp.sum(jnp.where(oh2, base, 0.0), axis=-1, keepdims=True)
        run_s[...] += colsum
        dd = jnp.exp(t2 - t1)
        g1 = p_grp / (1.0 + dd)
        g2 = p_grp * dd / (1.0 + dd)
        dest_ref[...] = jnp.where(lane == 0, d1, jnp.where(lane == 1, d2, 0.0)).astype(jnp.int32)
        gate_ref[...] = jnp.where(lane == 0, g1, jnp.where(lane == 1, g2, 0.0))


def _router_call(logits, n_blocks):
    n = logits.shape[0]
    tm = min(n, 512)
    nb_pad = ((n_blocks + 7) // 8) * 8
    return pl.pallas_call(
        _router_kernel,
        grid=(2, n // tm),
        in_specs=[pl.BlockSpec((tm, LANES), lambda p, i: (i, 0))],
        out_specs=[pl.BlockSpec((tm, LANES), lambda p, i: (i * p, 0)),
                   pl.BlockSpec((tm, LANES), lambda p, i: (i * p, 0)),
                   pl.BlockSpec((nb_pad, LANES), lambda p, i: (0, 0))],
        out_shape=[jax.ShapeDtypeStruct((n, LANES), jnp.int32),
                   jax.ShapeDtypeStruct((n, LANES), F32),
                   jax.ShapeDtypeStruct((nb_pad, LANES), jnp.int32)],
        scratch_shapes=[pltpu.VMEM((1, LANES), F32), pltpu.VMEM((1, LANES), F32),
                        pltpu.VMEM((1, LANES), F32)],
        compiler_params=_cparams(("arbitrary", "arbitrary")),
        name="router",
    )(logits)


def _slots_kernel(dest_ref, zero_ref, inv_ref, sem):
    fill = pltpu.make_async_copy(zero_ref, inv_ref, sem)
    fill.start()
    fill.wait()

    def put(tok, carry):
        inv_ref[dest_ref[2 * tok]] = tok
        inv_ref[dest_ref[2 * tok + 1]] = tok
        return carry

    lax.fori_loop(0, dest_ref.shape[0] // 2, put, 0, unroll=8)


def _slots_call(dest_flat, n_rows):
    return pl.pallas_call(
        _slots_kernel,
        in_specs=[pl.BlockSpec(memory_space=pltpu.SMEM), pl.BlockSpec(memory_space=pl.ANY)],
        out_specs=pl.BlockSpec(memory_space=pltpu.SMEM),
        out_shape=jax.ShapeDtypeStruct((n_rows,), jnp.int32),
        scratch_shapes=[pltpu.SemaphoreType.DMA(())],
        name="slots",
    )(dest_flat, jnp.zeros((n_rows,), jnp.int32))


GMM_GROUP = 6
GMM_TILES = 4
WEIGHT_DMA_SPLIT = 4


def _unit_tables(block_expert, used, n_blocks):
    n_units_max = N_EXPERTS + n_blocks // GMM_GROUP
    valid = jnp.arange(n_blocks) < used
    nblk_e = jnp.sum((block_expert[None, :] == jnp.arange(N_EXPERTS)[:, None]) & valid[None, :],
                     axis=1).astype(jnp.int32)
    first_e = jnp.cumsum(nblk_e) - nblk_e
    units_e = (nblk_e + GMM_GROUP - 1) // GMM_GROUP
    uend = jnp.cumsum(units_e)
    n_units = uend[-1]
    u = jnp.arange(n_units_max, dtype=jnp.int32)
    ue = jnp.minimum(jnp.sum(uend[None, :] <= u[:, None], axis=1), N_EXPERTS - 1).astype(jnp.int32)
    k = u - (uend - units_e)[ue]
    live = u < n_units
    ub = jnp.where(live, first_e[ue] + k * GMM_GROUP, 0).astype(jnp.int32)
    un = jnp.where(live, jnp.clip(nblk_e[ue] - k * GMM_GROUP, 0, GMM_GROUP), 0).astype(jnp.int32)
    return ue, ub, un, n_units.reshape(1).astype(jnp.int32)


def _gmm_kernel(ue_ref, ub_ref, un_ref, nu_ref, used_ref, inv_ref, h_ref, w1_ref, w3_ref, w2_ref, y_ref,
                xbuf, xs, acc, w1f, w3f, w2f, w1b_s, w3b_s, w2b_s, gsem, osem, wsem, *, n_blocks):
    u = pl.program_id(0)
    n_units = nu_ref[0]
    slot = u % 2
    d, tj = w1f.shape[1], w1f.shape[2]

    def gather(unit, s):
        base = ub_ref[unit] * MOE_BLOCK

        def body(g, carry):
            for k in range(8):
                r = g * 8 + k
                pltpu.make_async_copy(h_ref.at[pl.ds(inv_ref[base + r], 1)], xbuf.at[s, pl.ds(r, 1)],
                                      gsem.at[s]).start()
            return carry

        lax.fori_loop(0, un_ref[unit] * (MOE_BLOCK // 8), body, 0)

    def wait_gather(unit, s):
        rows = pl.ds(0, un_ref[unit] * MOE_BLOCK)
        pltpu.make_async_copy(xbuf.at[s, rows], xbuf.at[s, rows], gsem.at[s]).wait()

    def blk_rows(i):
        return pl.ds(pl.multiple_of(i * MOE_BLOCK, MOE_BLOCK), MOE_BLOCK)

    def out_copy(unit, s, i):
        dst = pl.ds(pl.multiple_of((ub_ref[unit] + i) * MOE_BLOCK, MOE_BLOCK), MOE_BLOCK)
        return pltpu.make_async_copy(acc.at[s, blk_rows(i)], y_ref.at[dst], osem.at[s])

    def wait_out(unit, s):
        def body(i, carry):
            out_copy(unit, s, i).wait()
            return carry

        lax.fori_loop(0, un_ref[unit], body, 0)

    def weight_copies(unit, j, ws):
        e = ue_ref[unit]
        cols = pl.ds(j * tj, tj)
        copies = []
        for p in range(WEIGHT_DMA_SPLIT):
            rk = pl.ds(p * (d // WEIGHT_DMA_SPLIT), d // WEIGHT_DMA_SPLIT)
            rj = pl.ds(p * (tj // WEIGHT_DMA_SPLIT), tj // WEIGHT_DMA_SPLIT)
            rj_src = pl.ds(j * tj + p * (tj // WEIGHT_DMA_SPLIT), tj // WEIGHT_DMA_SPLIT)
            copies += [pltpu.make_async_copy(w1_ref.at[e, rk, cols], w1f.at[ws, rk], wsem.at[ws]),
                       pltpu.make_async_copy(w3_ref.at[e, rk, cols], w3f.at[ws, rk], wsem.at[ws]),
                       pltpu.make_async_copy(w2_ref.at[e, rj_src, :], w2f.at[ws, rj], wsem.at[ws])]
        return copies

    @pl.when(u == 0)
    def _():
        gather(0, 0)
        for c in weight_copies(0, 0, 0):
            c.start()

    def tile_pass(j, ws):
        first = j == 0
        w1b = w1f[ws].astype(BF16)
        w3b = w3f[ws].astype(BF16)
        w2b = w2f[ws].astype(BF16)
        w1b_s[...] = w1b
        w3b_s[...] = w3b
        w2b_s[...] = w2b

        def rows_pass(r0, m, a1, a3, a2):
            rows = pl.ds(r0, m)
            xb = xs[rows, :]
            h1 = jnp.dot(xb, a1, preferred_element_type=F32)
            h3 = jnp.dot(xb, a3, preferred_element_type=F32)
            part = jnp.dot((_silu(h1) * h3).astype(BF16), a2, preferred_element_type=F32)
            if first:
                acc[slot, rows, :] = part
            else:
                acc[slot, rows, :] += part

        rows_pass(0, MOE_BLOCK, w1b, w3b, w2b)
        rest = un_ref[u] - 1

        def pair(i, carry):
            r0 = pl.multiple_of(MOE_BLOCK + i * 2 * MOE_BLOCK, MOE_BLOCK)
            rows_pass(r0, 2 * MOE_BLOCK, w1b_s[...], w3b_s[...], w2b_s[...])
            return carry

        lax.fori_loop(0, rest // 2, pair, 0)

        @pl.when(rest % 2 == 1)
        def _():
            rows_pass(pl.multiple_of(rest * MOE_BLOCK, MOE_BLOCK), MOE_BLOCK,
                      w1b_s[...], w3b_s[...], w2b_s[...])

    @pl.when(u < n_units)
    def _():
        @pl.when(u >= 2)
        def _():
            wait_out(u - 2, slot)

        @pl.when(u + 1 < n_units)
        def _():
            gather(u + 1, 1 - slot)

        wait_gather(u, slot)

        def to_bf16(i, carry):
            xs[blk_rows(i), :] = xbuf[slot, blk_rows(i), :].astype(BF16)
            return carry

        lax.fori_loop(0, un_ref[u], to_bf16, 0)
        nxt = jnp.minimum(u + 1, n_units - 1)
        for j in range(GMM_TILES):
            ws = j % 2
            following = weight_copies(u, j + 1, 1 - ws) if j + 1 < GMM_TILES else weight_copies(nxt, 0, 1 - ws)
            for c in following:
                c.start()
            for c in weight_copies(u, j, ws):
                c.wait()
            tile_pass(j, ws)

        def start(i, carry):
            out_copy(u, slot, i).start()
            return carry

        lax.fori_loop(0, un_ref[u], start, 0)

    @pl.when(u == pl.num_programs(0) - 1)
    def _():
        last = n_units - 1
        for c in weight_copies(last, 0, 0):
            c.wait()
        wait_out(last, last % 2)

        @pl.when(n_units >= 2)
        def _():
            wait_out(last - 1, (last - 1) % 2)

        acc[0, blk_rows(0), :] = jnp.zeros((MOE_BLOCK, acc.shape[2]), F32)

        def zero_copy(b):
            dst = pl.ds(pl.multiple_of(b * MOE_BLOCK, MOE_BLOCK), MOE_BLOCK)
            return pltpu.make_async_copy(acc.at[0, blk_rows(0)], y_ref.at[dst], osem.at[0])

        def start(b, carry):
            zero_copy(b).start()
            return carry

        def wait(b, carry):
            zero_copy(b).wait()
            return carry

        lax.fori_loop(used_ref[0], n_blocks, start, 0)
        lax.fori_loop(used_ref[0], n_blocks, wait, 0)


def _gmm_call(block_expert, used, inv, h2, w1, w3, w2):
    n_rows = inv.shape[0]
    d = h2.shape[1]
    de = w1.shape[-1]
    n_blocks = n_rows // MOE_BLOCK
    tj = de // GMM_TILES
    ue, ub, un, nu = _unit_tables(block_expert, used[0], n_blocks)

    assert GMM_TILES % 2 == 0, "weight buffers alternate per tile; an even count keeps tile j in buffer j % 2"
    rows = GMM_GROUP * MOE_BLOCK
    grid_spec = pltpu.PrefetchScalarGridSpec(
        num_scalar_prefetch=6,
        grid=(ue.shape[0],),
        in_specs=[pl.BlockSpec(memory_space=pl.ANY)] * 4,
        out_specs=pl.BlockSpec(memory_space=pl.ANY),
        scratch_shapes=[pltpu.VMEM((2, rows, d), F32), pltpu.VMEM((rows, d), BF16),
                        pltpu.VMEM((2, rows, d), F32),
                        pltpu.VMEM((2, d, tj), F32), pltpu.VMEM((2, d, tj), F32), pltpu.VMEM((2, tj, d), F32),
                        pltpu.VMEM((d, tj), BF16), pltpu.VMEM((d, tj), BF16), pltpu.VMEM((tj, d), BF16),
                        pltpu.SemaphoreType.DMA((2,)), pltpu.SemaphoreType.DMA((2,)),
                        pltpu.SemaphoreType.DMA((2,))],
    )
    return pl.pallas_call(
        functools.partial(_gmm_kernel, n_blocks=n_blocks),
        grid_spec=grid_spec,
        out_shape=jax.ShapeDtypeStruct((n_rows, d), F32),
        compiler_params=_cparams(("arbitrary",)),
        name="gmm",
    )(ue, ub, un, nu, used, inv, h2, w1, w3, w2)


def _combine_kernel(dest_ref, y_ref, x1_ref, gate_ref, mods_ref, fg_ref, o_ref, ybuf, sem, *, seq):
    i = pl.program_id(0)
    tm = x1_ref.shape[0]
    slot = i % 2

    last = pl.num_programs(0) - 1

    def row_copy(step, r, k, s):
        return pltpu.make_async_copy(y_ref.at[pl.ds(dest_ref[2 * (step * tm + r) + k], 1)],
                                     ybuf.at[s, k, pl.ds(r, 1)], sem.at[s])

    def wait_gather(s):
        pltpu.make_async_copy(ybuf.at[s], ybuf.at[s], sem.at[s]).wait()

    @pl.when(i == 0)
    def _():
        def body(r, carry):
            row_copy(0, r, 0, 0).start()
            row_copy(0, r, 1, 0).start()
            return carry

        lax.fori_loop(0, tm, body, 0, unroll=4)

    wait_gather(slot)

    nxt = jnp.minimum(i + 1, last)
    for r in range(tm):
        row_copy(nxt, r, 0, 1 - slot).start()
        row_copy(nxt, r, 1, 1 - slot).start()

    b = (i * tm) // seq
    gt2 = mods_ref[5, pl.ds(b, 1), :]
    gate = gate_ref[...]
    moe = gate[:, 0:1] * ybuf[slot, 0] + gate[:, 1:2] * ybuf[slot, 1]
    x = x1_ref[...] + gt2 * moe
    o_ref[...] = x * lax.rsqrt(jnp.mean(x * x, axis=-1, keepdims=True) + EPS) * fg_ref[...]

    @pl.when(i == last)
    def _():
        wait_gather(1 - slot)


def _combine_call(dest_flat, y_pad, x1, gate, mods, final_g, seq):
    n, d = x1.shape
    tm = min(seq, 256)
    grid_spec = pltpu.PrefetchScalarGridSpec(
        num_scalar_prefetch=1,
        grid=(n // tm,),
        in_specs=[pl.BlockSpec(memory_space=pl.ANY),
                  pl.BlockSpec((tm, d), lambda i, dest: (i, 0)),
                  pl.BlockSpec((tm, LANES), lambda i, dest: (i, 0)),
                  pl.BlockSpec((6, 8, d), lambda i, dest: (0, 0, 0)),
                  pl.BlockSpec((1, d), lambda i, dest: (0, 0))],
        out_specs=pl.BlockSpec((tm, d), lambda i, dest: (i, 0)),
        scratch_shapes=[pltpu.VMEM((2, 2, tm, d), F32), pltpu.SemaphoreType.DMA((2,))],
    )
    return pl.pallas_call(
        functools.partial(_combine_kernel, seq=seq),
        grid_spec=grid_spec,
        out_shape=jax.ShapeDtypeStruct((n, d), F32),
        compiler_params=_cparams(("arbitrary",)),
        name="combine",
    )(dest_flat, y_pad, x1, gate, mods, final_g)


def _pad_lanes(a, lane0):
    return jnp.zeros((LANES,), F32).at[lane0:lane0 + a.shape[0]].set(a.astype(F32))


def kernel(x, c, ctx, c_ctx, w_mod, b_mod, norm1_g, w_in, pool_w, pool_scale, conv_w,
           a_log_f, dt_bias_f, a_log_b, dt_bias_b, out_norm_g, w_out, norm2_g,
           w_grp, b_grp, w_rt, b_rt, w1, w3, w2, final_g):
    bsz, t, d = x.shape
    depth = w_mod.shape[0]
    assert depth == 1, "single-layer problem: the context stream is read but never updated"
    heads = a_log_f.shape[1]
    pool_width = pool_w.shape[1] * pool_w.shape[2]
    dn_width = heads * HEAD_DIM
    q0 = pool_width
    z0 = q0 + 3 * dn_width
    ab0 = z0 + dn_width
    n_tok = bsz * t
    l = 0

    c8 = jnp.zeros((8, d), F32).at[:bsz].set(c).at[bsz].set(c_ctx)
    mod = _mod_call(c8, w_mod[l], b_mod[l])
    mods = mod.reshape(8, 6, d).transpose(1, 0, 2)

    w_in_bf = w_in[l].astype(BF16)
    wab_bf = jnp.zeros((d, LANES), BF16).at[:, :4 * heads].set(w_in_bf[:, ab0:])
    g1 = norm1_g[l].reshape(1, d)
    proj, ab = _inproj_call(x, mods, g1, w_in_bf, wab_bf, ab0, None)
    proj_c, ab_c = _inproj_call(ctx, mods, g1, w_in_bf, wab_bf, ab0, bsz)

    prm = jnp.zeros((8, LANES), F32)
    prm = prm.at[0].set(_pad_lanes(jnp.concatenate([a_log_f[l], a_log_b[l]]), 2 * heads))
    prm = prm.at[1].set(_pad_lanes(jnp.concatenate([dt_bias_f[l], dt_bias_b[l]]), 2 * heads))
    g, g_t = _gates_call(ab, prm, heads)
    gc, gc_t = _gates_call(ab_c, prm, heads)
    gr = g_t.reshape(bsz, LANES, t // CHUNK, CHUNK)
    grc = gc_t.reshape(bsz, LANES, ctx.shape[1] // CHUNK, CHUNK)

    dn = _delta_call(proj, proj_c, conv_w[l], g, gc, gr, grc, out_norm_g[l].reshape(1, HEAD_DIM),
                     heads, q0 // HEAD_DIM, z0 // HEAD_DIM)
    pool = _pool_call(proj, pool_w[l], pool_scale[l].reshape(1, pool_width), t // GRID_W, GRID_W)

    wr = jnp.zeros((d, LANES), F32).at[:, :N_GROUPS].set(w_grp[l]).at[:, EXP_LANE0:EXP_LANE0 + N_EXPERTS].set(w_rt[l])
    br = jnp.zeros((LANES,), F32).at[:N_GROUPS].set(b_grp[l]).at[EXP_LANE0:EXP_LANE0 + N_EXPERTS].set(b_rt[l])
    x1, h2, logits = _outproj_call(pool, dn, w_out[l].astype(BF16), x, mods, norm2_g[l].reshape(1, d),
                                   wr, br.reshape(1, LANES))

    n_blocks = (n_tok * 2 + N_EXPERTS * (MOE_BLOCK - 1) + MOE_BLOCK - 1) // MOE_BLOCK
    dest, gate, be = _router_call(logits.reshape(n_tok, LANES), n_blocks)
    dest_flat = dest[:, :2].reshape(-1)
    block_expert = be[:n_blocks, 0]
    used = be[0:1, 1]
    inv = _slots_call(dest_flat, n_blocks * MOE_BLOCK)
    y_pad = _gmm_call(block_expert, used, inv, h2.reshape(n_tok, d), w1[l], w3[l], w2[l])
    out = _combine_call(dest_flat, y_pad, x1.reshape(n_tok, d), gate, mods, final_g.reshape(1, d), t)
    return out.reshape(bsz, t, d)
```

```python
import functools

import jax
import jax.numpy as jnp
from jax import lax
from jax.experimental import pallas as pl
from jax.experimental.pallas import tpu as pltpu

F32 = jnp.float32
BF16 = jnp.bfloat16

GRID_W = 64
POOL_WINDOWS = (2, 4, 8, 16)
HEAD_DIM = 128
CONV_WIDTH = 5
CHUNK = 64
N_GROUPS = 4
EXPERTS_PER_GROUP = 8
N_EXPERTS = N_GROUPS * EXPERTS_PER_GROUP
MOE_BLOCK = 128
EPS = 1e-6
LANES = 128
PRE_UNROLL = 8
EXP_LANE0 = N_GROUPS

VMEM_LIMIT = 56 * 1024 * 1024


def _cparams(sem):
    return pltpu.CompilerParams(dimension_semantics=sem, vmem_limit_bytes=VMEM_LIMIT)


def _dot(a, b):
    return jnp.dot(a.astype(BF16), b.astype(BF16), preferred_element_type=F32)


def _dot_split(a, b):
    a_hi = a.astype(BF16)
    b_hi = b.astype(BF16)
    a_lo = (a - a_hi.astype(F32)).astype(BF16)
    b_lo = (b - b_hi.astype(F32)).astype(BF16)
    return (jnp.dot(a_hi, b_hi, preferred_element_type=F32)
            + jnp.dot(a_lo, b_hi, preferred_element_type=F32)
            + jnp.dot(a_hi, b_lo, preferred_element_type=F32))


def _silu(x):
    return x * jax.nn.sigmoid(x)


def _mod_kernel(c_ref, w_ref, b_ref, o_ref):
    o_ref[...] = _dot(_silu(c_ref[...]), w_ref[...]) + b_ref[...]


def _mod_call(c8, w_mod, b_mod):
    d, n = w_mod.shape
    tn = 512
    return pl.pallas_call(
        _mod_kernel,
        grid=(n // tn,),
        in_specs=[pl.BlockSpec((8, d), lambda j: (0, 0)),
                  pl.BlockSpec((d, tn), lambda j: (0, j)),
                  pl.BlockSpec((1, tn), lambda j: (0, j))],
        out_specs=pl.BlockSpec((8, tn), lambda j: (0, j)),
        out_shape=jax.ShapeDtypeStruct((8, n), F32),
        compiler_params=_cparams(("parallel",)),
        name="mod",
    )(c8, w_mod, b_mod.reshape(1, n))


def _inproj_kernel(x_ref, mods_ref, g_ref, w_ref, wab_ref, o_ref, ab_ref, hn_ref, *, mod_row):
    b = pl.program_id(0)
    j = pl.program_id(2)

    @pl.when(j == 0)
    def _():
        x = x_ref[0]
        y = x * lax.rsqrt(jnp.mean(x * x, axis=-1, keepdims=True) + EPS) * g_ref[...]
        row = b if mod_row is None else mod_row
        sh = mods_ref[0, pl.ds(row, 1), :]
        sc = mods_ref[1, pl.ds(row, 1), :]
        hb = (y * (1 + sc) + sh).astype(BF16)
        hn_ref[...] = hb
        ab_ref[0] = jnp.dot(hb, wab_ref[...], preferred_element_type=F32)

    o_ref[0] = jnp.dot(hn_ref[...], w_ref[...], preferred_element_type=F32)


def _inproj_call(x, mods, g, w_in, wab_bf, n_main, mod_row):
    bsz, t, d = x.shape
    tm = min(t, 1024)
    tn = 1024
    kern = functools.partial(_inproj_kernel, mod_row=mod_row)
    return pl.pallas_call(
        kern,
        grid=(bsz, t // tm, n_main // tn),
        in_specs=[pl.BlockSpec((1, tm, d), lambda b, i, j: (b, i, 0)),
                  pl.BlockSpec((2, 8, d), lambda b, i, j: (0, 0, 0)),
                  pl.BlockSpec((1, d), lambda b, i, j: (0, 0)),
                  pl.BlockSpec((d, tn), lambda b, i, j: (0, j)),
                  pl.BlockSpec((d, LANES), lambda b, i, j: (0, 0))],
        out_specs=[pl.BlockSpec((1, tm, tn), lambda b, i, j: (b, i, j)),
                   pl.BlockSpec((1, tm, LANES), lambda b, i, j: (b, i, 0))],
        out_shape=[jax.ShapeDtypeStruct((bsz, t, n_main), F32),
                   jax.ShapeDtypeStruct((bsz, t, LANES), F32)],
        scratch_shapes=[pltpu.VMEM((tm, d), BF16)],
        compiler_params=_cparams(("parallel", "parallel", "arbitrary")),
        name="inproj",
    )(x, mods, g, w_in, wab_bf)


def _gates_kernel(ab_ref, prm_ref, g_ref, gt_ref, *, heads):
    ab = ab_ref[0]
    t = ab.shape[0]
    h2, h3, h4, h6 = 2 * heads, 3 * heads, 4 * heads, 6 * heads
    beta = jax.nn.sigmoid(ab)
    xx = ab + prm_ref[1:2, :]
    softplus = jnp.maximum(xx, 0.0) + jnp.log1p(jnp.exp(-jnp.abs(xx)))
    g = -jnp.exp(prm_ref[0:1, :]) * softplus
    pos = lax.broadcasted_iota(jnp.int32, ab.shape, 0) & (CHUNK - 1)
    cs = g
    ss = g
    s = 1
    while s < CHUNK:
        cs = cs + jnp.where(pos >= s, pltpu.roll(cs, s, 0), 0.0)
        ss = ss + jnp.where(pos < CHUNK - s, pltpu.roll(ss, t - s, 0), 0.0)
        s *= 2
    tot = pltpu.roll(cs + ss - g, h2, 1)
    lane = lax.broadcasted_iota(jnp.int32, ab.shape, 1)
    out = jnp.where(lane < h2, beta,
                    jnp.where(lane < h3, cs,
                              jnp.where(lane < h4, ss,
                                        jnp.where(lane < h6, tot, 0.0))))
    g_ref[0] = out
    gt_ref[0] = out.T


def _gates_call(ab, prm, heads):
    bsz, t, _ = ab.shape
    return pl.pallas_call(
        functools.partial(_gates_kernel, heads=heads),
        grid=(bsz,),
        in_specs=[pl.BlockSpec((1, t, LANES), lambda b: (b, 0, 0)),
                  pl.BlockSpec((8, LANES), lambda b: (0, 0))],
        out_specs=[pl.BlockSpec((1, t, LANES), lambda b: (b, 0, 0)),
                   pl.BlockSpec((1, LANES, t), lambda b: (b, 0, 0))],
        out_shape=[jax.ShapeDtypeStruct((bsz, t, LANES), F32),
                   jax.ShapeDtypeStruct((bsz, LANES, t), F32)],
        compiler_params=_cparams(("parallel",)),
        name="gates",
    )(ab, prm)


def _conv_silu(x, w):
    n = x.shape[0]
    row = lax.broadcasted_iota(jnp.int32, x.shape, 0)
    acc = x * w[CONV_WIDTH // 2:CONV_WIDTH // 2 + 1, :]
    for j in range(CONV_WIDTH):
        d = j - CONV_WIDTH // 2
        if d == 0:
            continue
        xs = pltpu.roll(x, (-d) % n, 0)
        valid = (row + d >= 0) & (row + d < n)
        acc = acc + jnp.where(valid, xs, 0.0) * w[j:j + 1, :]
    return _silu(acc)


def _l2norm(a):
    return a * lax.rsqrt(jnp.sum(a * a, axis=-1, keepdims=True) + EPS)


def _lane_col(g, lane_idx):
    lane = lax.broadcasted_iota(jnp.int32, g.shape, 1)
    return jnp.sum(jnp.where(lane == lane_idx, g, 0.0), axis=-1, keepdims=True)


def _chunk_terms(chains, between=()):
    pending = list(between)

    def stage_done():
        if pending:
            pending.pop(0)()

    c, hd = chains[0][0].shape
    ri = lax.broadcasted_iota(jnp.int32, (c, c), 0)
    ci = lax.broadcasted_iota(jnp.int32, (c, c), 1)
    eye = jnp.where(ri == ci, 1.0, 0.0)
    right = lax.broadcasted_iota(jnp.int32, (c, 2 * c), 1) >= c
    nt = (((1,), (1,)), ((), ()))
    tn = (((0,), (0,)), ((), ()))

    decs, kn_bs, zs = [], [], []
    for kn_c, kb_c, _, _, _, _, gcc, gcr, upper in chains:
        incl = (ri <= ci) if upper else (ri >= ci)
        strict = (ri < ci) if upper else (ri > ci)
        dec = jnp.where(incl, jnp.exp(jnp.where(incl, gcc - gcr, 0.0)), 0.0)
        kn_b = kn_c.astype(BF16)
        kk = lax.dot_general(kb_c.astype(BF16), kn_b, nt, preferred_element_type=F32)
        decs.append(dec)
        kn_bs.append(kn_b)
        zs.append(jnp.concatenate([-jnp.where(strict, kk * dec, 0.0), eye], axis=1))
    stage_done()
    n = 1
    while n < c:
        zs = [_dot(z[:, :c], z) + jnp.where(right, z, 0.0) for z in zs]
        stage_done()
        n *= 2
    uw_bs = [_dot(z[:, c:], ch[3]).astype(BF16) for z, ch in zip(zs, chains)]
    stage_done()
    wns = [lax.dot_general(ch[4].astype(BF16), uw_b, tn, preferred_element_type=F32)
           for uw_b, ch in zip(uw_bs, chains)]
    stage_done()
    qks = [None if ch[2] is None else
           lax.dot_general(ch[2].astype(BF16), kn_b, nt, preferred_element_type=F32) * dec
           for ch, kn_b, dec in zip(chains, kn_bs, decs)]
    stage_done()
    qws = [None if qk is None else jnp.dot(qk.astype(BF16), uw_b, preferred_element_type=F32)
           for qk, uw_b in zip(qks, uw_bs)]
    while pending:
        stage_done()
    out = []
    for ch, wn, qw in zip(chains, wns, qws):
        nc, w2 = wn[:, :hd], wn[:, hd:]
        if qw is None:
            out.append((w2, nc, None, None))
        else:
            out.append((w2, nc, ch[5] - qw[:, hd:], qw[:, :hd]))
    return out


def _delta_kernel(q_ref, k_ref, v_ref, z_ref, kc_ref, vc_ref, cwq_ref, cwk_ref, cwv_ref,
                  g_ref, gc_ref, grf_ref, grb_ref, grcf_ref, grcb_ref, ong_ref,
                  out_ref,
                  qn_s, kn_s, kb_s, rhs_s, kd_s, qd_s, col_s,
                  knc_s, kbc_s, rhsc_s, kdc_s, colc_s,
                  w2_s, nc_s, qp_s, o0_s, w2c_s, ncc_s, o_s, *, heads):
    h = pl.program_id(1)
    t = q_ref.shape[1]
    tc = kc_ref.shape[1]
    hd = HEAD_DIM
    n_lat = t // CHUNK
    n_ctx = tc // CHUNK

    qn = _l2norm(_conv_silu(q_ref[0], cwq_ref[...])) * (hd ** -0.5)
    kn = _l2norm(_conv_silu(k_ref[0], cwk_ref[...]))
    vv = _conv_silu(v_ref[0], cwv_ref[...])
    knc = _l2norm(_conv_silu(kc_ref[0], cwk_ref[...]))
    vvc = _conv_silu(vc_ref[0], cwv_ref[...])
    qn_s[...] = qn
    kn_s[...] = kn
    knc_s[...] = knc
    g_lat = g_ref[0]
    g_ctx = gc_ref[0]

    for d in range(2):
        for (gt_, kn_, vv_, kb_r, rhs_r, kd_r, col_r, qn_, qd_r) in (
                (g_lat, kn, vv, kb_s, rhs_s, kd_s, col_s, qn, qd_s),
                (g_ctx, knc, vvc, kbc_s, rhsc_s, kdc_s, colc_s, None, None)):
            beta = _lane_col(gt_, d * heads + h)
            gcum = _lane_col(gt_, (2 + d) * heads + h)
            gtot = _lane_col(gt_, (4 + d) * heads + h)
            e = jnp.exp(gcum)
            kb = kn_ * beta
            kb_r[d] = kb
            rhs_r[d, :, 0:hd] = vv_ * beta
            rhs_r[d, :, hd:2 * hd] = kb * e
            kd_r[d] = kn_ * jnp.exp(gtot - gcum)
            lane = lax.broadcasted_iota(jnp.int32, (gcum.shape[0], LANES), 1)
            col_r[d] = jnp.where(lane == 0, gcum, jnp.where(lane == 1, jnp.exp(gtot), 0.0))
            if qn_ is not None:
                qd_r[d] = qn_ * e

    gr_lat = (grf_ref, grb_ref)
    gr_ctx = (grcf_ref, grcb_ref)

    un_ctx = min(PRE_UNROLL, n_ctx)
    un_lat = min(PRE_UNROLL, n_lat)

    def pre_ctx(i, carry):
        ids = [(i * un_ctx + u, d) for u in range(un_ctx) for d in range(2)]
        chains = []
        for c, d in ids:
            rows = pl.ds(pl.multiple_of(c * CHUNK, CHUNK), CHUNK)
            chains.append((knc_s[rows, :], kbc_s[d, rows, :], None, rhsc_s[d, rows, :], kdc_s[d, rows, :],
                           None, colc_s[d, rows, 0:1], gr_ctx[d][0, 0, pl.ds(c, 1), :], d == 1))
        for (c, d), (w2, nc, _, _) in zip(ids, _chunk_terms(chains)):
            m0 = pl.multiple_of(c * hd, hd)
            w2c_s[d, pl.ds(m0, hd), :] = w2
            ncc_s[d, pl.ds(m0, hd), :] = nc
        return carry

    lax.fori_loop(0, n_ctx // un_ctx, pre_ctx, 0)

    def lat_chunk(i, u, d):
        k = i * un_lat + u
        return k if d == 0 else n_lat - 1 - k

    def pre_lat(i, between=()):
        ids = [(lat_chunk(i, u, d), d) for u in range(un_lat) for d in range(2)]
        chains = []
        for c, d in ids:
            rows = pl.ds(pl.multiple_of(c * CHUNK, CHUNK), CHUNK)
            chains.append((kn_s[rows, :], kb_s[d, rows, :], qn_s[rows, :], rhs_s[d, rows, :], kd_s[d, rows, :],
                           qd_s[d, rows, :], col_s[d, rows, 0:1], gr_lat[d][0, 0, pl.ds(c, 1), :], d == 1))
        for (c, d), (w2, nc, qp, o0) in zip(ids, _chunk_terms(chains, between)):
            rows = pl.ds(pl.multiple_of(c * CHUNK, CHUNK), CHUNK)
            m0 = pl.multiple_of(c * hd, hd)
            w2_s[d, pl.ds(m0, hd), :] = w2
            nc_s[d, pl.ds(m0, hd), :] = nc
            qp_s[d, rows, :] = qp
            o0_s[d, rows, :] = o0

    def scan_ctx_steps(box):
        def step(i):
            def run():
                for d in range(2):
                    c = i if d == 0 else n_ctx - 1 - i
                    gt = colc_s[d, c * CHUNK:c * CHUNK + 1, 1:2]
                    box[d] = (gt * box[d] + ncc_s[d, c * hd:(c + 1) * hd, :]
                              - _dot(w2c_s[d, c * hd:(c + 1) * hd, :], box[d]))
            return run

        return [step(i) for i in range(n_ctx)]

    def scan_lat_steps(i, box):
        def step(u):
            def run():
                for d in range(2):
                    s = box[d]
                    c = lat_chunk(i, u, d)
                    r0 = pl.multiple_of(c * CHUNK, CHUNK)
                    rows = pl.ds(r0, CHUNK)
                    m0 = pl.multiple_of(c * hd, hd)
                    s_b = s.astype(BF16)
                    o_s[d, rows, :] = jnp.dot(qp_s[d, rows, :].astype(BF16), s_b,
                                              preferred_element_type=F32) + o0_s[d, rows, :]
                    gt = col_s[d, pl.ds(r0, 1), 1:2]
                    box[d] = gt * s + nc_s[d, pl.ds(m0, hd), :] - jnp.dot(
                        w2_s[d, pl.ds(m0, hd), :].astype(BF16), s_b, preferred_element_type=F32)
            return run

        return [step(u) for u in range(un_lat)]

    zero = jnp.zeros((hd, hd), F32)
    box = [zero, zero]
    pre_lat(0, scan_ctx_steps(box))
    states = tuple(box)

    def lat_body(i, states):
        box = list(states)
        pre_lat(i, scan_lat_steps(i - 1, box))
        return tuple(box)

    states = lax.fori_loop(1, n_lat // un_lat, lat_body, states)
    box = list(states)
    for run in scan_lat_steps(n_lat // un_lat - 1, box):
        run()

    o = o_s[0] + o_s[1]
    o = o * lax.rsqrt(jnp.mean(o * o, axis=-1, keepdims=True) + EPS) * ong_ref[...]
    out_ref[0] = (o * _silu(z_ref[0])).astype(out_ref.dtype)


def _delta_call(proj, proj_c, conv_w, g, gc, gr, grc, ong, heads, q_blk0, z_blk0):
    bsz, t, _ = proj.shape
    tc = proj_c.shape[1]
    hd = HEAD_DIM
    n_lat, n_ctx = t // CHUNK, tc // CHUNK

    def col(off):
        return lambda b, h: (b, 0, off + h)

    def cw(off):
        return lambda b, h: (0, off + h)

    def grow(off):
        return lambda b, h: (b, off + h, 0, 0)

    in_specs = [
        pl.BlockSpec((1, t, hd), col(q_blk0)),
        pl.BlockSpec((1, t, hd), col(q_blk0 + heads)),
        pl.BlockSpec((1, t, hd), col(q_blk0 + 2 * heads)),
        pl.BlockSpec((1, t, hd), col(z_blk0)),
        pl.BlockSpec((1, tc, hd), col(q_blk0 + heads)),
        pl.BlockSpec((1, tc, hd), col(q_blk0 + 2 * heads)),
        pl.BlockSpec((CONV_WIDTH, hd), cw(0)),
        pl.BlockSpec((CONV_WIDTH, hd), cw(heads)),
        pl.BlockSpec((CONV_WIDTH, hd), cw(2 * heads)),
        pl.BlockSpec((1, t, LANES), lambda b, h: (b, 0, 0)),
        pl.BlockSpec((1, tc, LANES), lambda b, h: (b, 0, 0)),
        pl.BlockSpec((1, 1, n_lat, CHUNK), grow(2 * heads)),
        pl.BlockSpec((1, 1, n_lat, CHUNK), grow(3 * heads)),
        pl.BlockSpec((1, 1, n_ctx, CHUNK), grow(2 * heads)),
        pl.BlockSpec((1, 1, n_ctx, CHUNK), grow(3 * heads)),
        pl.BlockSpec((1, hd), lambda b, h: (0, 0)),
    ]
    scratch = [
        pltpu.VMEM((t, hd), F32), pltpu.VMEM((t, hd), F32),
        pltpu.VMEM((2, t, hd), F32), pltpu.VMEM((2, t, 2 * hd), F32),
        pltpu.VMEM((2, t, hd), F32), pltpu.VMEM((2, t, hd), F32),
        pltpu.VMEM((2, t, LANES), F32),
        pltpu.VMEM((tc, hd), F32), pltpu.VMEM((2, tc, hd), F32),
        pltpu.VMEM((2, tc, 2 * hd), F32), pltpu.VMEM((2, tc, hd), F32),
        pltpu.VMEM((2, tc, LANES), F32),
        pltpu.VMEM((2, n_lat * hd, hd), F32), pltpu.VMEM((2, n_lat * hd, hd), F32),
        pltpu.VMEM((2, t, hd), F32), pltpu.VMEM((2, t, hd), F32),
        pltpu.VMEM((2, n_ctx * hd, hd), F32), pltpu.VMEM((2, n_ctx * hd, hd), F32),
        pltpu.VMEM((2, t, hd), F32),
    ]
    return pl.pallas_call(
        functools.partial(_delta_kernel, heads=heads),
        grid=(bsz, heads),
        in_specs=in_specs,
        out_specs=pl.BlockSpec((1, t, hd), lambda b, h: (b, 0, h)),
        out_shape=jax.ShapeDtypeStruct((bsz, t, heads * hd), BF16),
        scratch_shapes=scratch,
        compiler_params=_cparams(("parallel", "parallel")),
        name="delta",
    )(proj, proj, proj, proj, proj_c, proj_c, conv_w, conv_w, conv_w, g, gc, gr, gr, grc, grc, ong)


def _shift_rows(x, d, idx, size, stride):
    n = x.shape[0]
    xs = pltpu.roll(x, (-d * stride) % n, 0)
    return jnp.where((idx + d >= 0) & (idx + d < size), xs, 0.0)


def _box_sum_1d(x, win, idx, size, stride):
    m = win // 2
    lead = x
    trail = x
    k = 1
    while k < m:
        lead = lead + _shift_rows(lead, k, idx, size, stride)
        trail = trail + _shift_rows(trail, -k, idx, size, stride)
        k *= 2
    return lead + _shift_rows(trail, -1, idx, size, stride)


def _pool_kernel(u_ref, pw_ref, ps_ref, o_ref, *, rows, cols):
    t = u_ref.shape[1]
    gc = pw_ref.shape[1]
    tok = lax.broadcasted_iota(jnp.int32, (t, gc), 0)
    ci = tok % cols
    ri = tok // cols
    for gi, win in enumerate(POOL_WINDOWS):
        lo = win // 2
        hi = win - lo
        u = u_ref[0, :, gi * gc:(gi + 1) * gc]
        s = _box_sum_1d(u, win, ci, cols, 1)
        s = _box_sum_1d(s, win, ri, rows, cols)
        cnt_c = jnp.minimum(ci + hi, cols) - jnp.maximum(ci - lo, 0)
        cnt_r = jnp.minimum(ri + hi, rows) - jnp.maximum(ri - lo, 0)
        mean = s / (cnt_c * cnt_r).astype(F32)
        y = _dot(mean - u, pw_ref[gi]) * ps_ref[:, gi * gc:(gi + 1) * gc]
        o_ref[0, :, gi * gc:(gi + 1) * gc] = y.astype(o_ref.dtype)


def _pool_call(proj, pool_w, pool_scale, rows, cols):
    bsz, t, _ = proj.shape
    ng, gc, _ = pool_w.shape
    pwid = ng * gc
    return pl.pallas_call(
        functools.partial(_pool_kernel, rows=rows, cols=cols),
        grid=(bsz,),
        in_specs=[pl.BlockSpec((1, t, pwid), lambda b: (b, 0, 0)),
                  pl.BlockSpec((ng, gc, gc), lambda b: (0, 0, 0)),
                  pl.BlockSpec((1, pwid), lambda b: (0, 0))],
        out_specs=pl.BlockSpec((1, t, pwid), lambda b: (b, 0, 0)),
        out_shape=jax.ShapeDtypeStruct((bsz, t, pwid), BF16),
        compiler_params=_cparams(("parallel",)),
        name="pool",
    )(proj, pool_w, pool_scale)


def _outproj_kernel(pool_ref, dn_ref, wa_ref, wb_ref, x_ref, mods_ref, g2_ref, wr_ref, br_ref,
                    x1_ref, h2_ref, lg_ref):
    b = pl.program_id(0)
    mix = (jnp.dot(pool_ref[0], wa_ref[...], preferred_element_type=F32)
           + jnp.dot(dn_ref[0], wb_ref[...], preferred_element_type=F32))
    gt1 = mods_ref[2, pl.ds(b, 1), :]
    sh2 = mods_ref[3, pl.ds(b, 1), :]
    sc2 = mods_ref[4, pl.ds(b, 1), :]
    x1 = x_ref[0] + gt1 * mix
    x1_ref[0] = x1
    y = x1 * lax.rsqrt(jnp.mean(x1 * x1, axis=-1, keepdims=True) + EPS) * g2_ref[...]
    h2 = y * (1 + sc2) + sh2
    h2_ref[0] = h2
    lg_ref[0] = _dot_split(h2, wr_ref[...]) + br_ref[...]


def _outproj_call(pool, dn, w_out_bf, x, mods, g2, wr, br):
    bsz, t, d = x.shape
    half = pool.shape[-1]
    tm = min(t, 512)
    return pl.pallas_call(
        _outproj_kernel,
        grid=(bsz, t // tm),
        in_specs=[pl.BlockSpec((1, tm, half), lambda b, i: (b, i, 0)),
                  pl.BlockSpec((1, tm, half), lambda b, i: (b, i, 0)),
                  pl.BlockSpec((half, d), lambda b, i: (0, 0)),
                  pl.BlockSpec((half, d), lambda b, i: (1, 0)),
                  pl.BlockSpec((1, tm, d), lambda b, i: (b, i, 0)),
                  pl.BlockSpec((6, 8, d), lambda b, i: (0, 0, 0)),
                  pl.BlockSpec((1, d), lambda b, i: (0, 0)),
                  pl.BlockSpec((d, LANES), lambda b, i: (0, 0)),
                  pl.BlockSpec((1, LANES), lambda b, i: (0, 0))],
        out_specs=[pl.BlockSpec((1, tm, d), lambda b, i: (b, i, 0)),
                   pl.BlockSpec((1, tm, d), lambda b, i: (b, i, 0)),
                   pl.BlockSpec((1, tm, LANES), lambda b, i: (b, i, 0))],
        out_shape=[jax.ShapeDtypeStruct((bsz, t, d), F32),
                   jax.ShapeDtypeStruct((bsz, t, d), F32),
                   jax.ShapeDtypeStruct((bsz, t, LANES), F32)],
        compiler_params=_cparams(("parallel", "parallel")),
        name="outproj",
    )(pool, dn, w_out_bf, w_out_bf, x, mods, g2, wr, br)


def _router_kernel(lg_ref, dest_ref, gate_ref, be_ref, cnt_s, run_s, off_s):
    p = pl.program_id(0)
    i = pl.program_id(1)
    lg = lg_ref[...]
    tm = lg.shape[0]
    lane = lax.broadcasted_iota(jnp.int32, lg.shape, 1)
    lane_f = lane.astype(F32)
    neg = -jnp.inf
    big = float(LANES)

    grp = jnp.where(lane < N_GROUPS, lg, neg)
    gmax = jnp.max(grp, axis=-1, keepdims=True)
    gidx = jnp.min(jnp.where(grp == gmax, lane_f, big), axis=-1, keepdims=True)
    p_grp = 1.0 / jnp.sum(jnp.where(lane < N_GROUPS, jnp.exp(lg - gmax), 0.0), axis=-1, keepdims=True)
    lo = EXP_LANE0 + EXPERTS_PER_GROUP * gidx
    ev = jnp.where((lane_f >= lo) & (lane_f < lo + EXPERTS_PER_GROUP), lg, neg)
    t1 = jnp.max(ev, axis=-1, keepdims=True)
    i1 = jnp.min(jnp.where(ev == t1, lane_f, big), axis=-1, keepdims=True)
    ev2 = jnp.where(lane_f == i1, neg, ev)
    t2 = jnp.max(ev2, axis=-1, keepdims=True)
    i2 = jnp.min(jnp.where(ev2 == t2, lane_f, big), axis=-1, keepdims=True)
    oh1 = lane_f == i1
    oh2 = lane_f == i2
    cnt = oh1.astype(F32) + oh2.astype(F32)
    colsum = jnp.sum(cnt, axis=0, keepdims=True)

    @pl.when(p == 0)
    def _():
        @pl.when(i == 0)
        def _():
            cnt_s[...] = jnp.zeros_like(cnt_s)

        cnt_s[...] += colsum

    @pl.when(p == 1)
    def _():
        @pl.when(i == 0)
        def _():
            nblk = jnp.floor((cnt_s[...] + (MOE_BLOCK - 1)) * (1.0 / MOE_BLOCK))
            r = lax.broadcasted_iota(jnp.int32, (LANES, LANES), 0)
            c = lax.broadcasted_iota(jnp.int32, (LANES, LANES), 1)
            tri = (r < c).astype(BF16)
            nb8 = jnp.broadcast_to(nblk, (8, LANES))
            start_blk = jnp.dot(nb8.astype(BF16), tri, preferred_element_type=F32)[0:1, :]
            off_s[...] = start_blk * MOE_BLOCK
            run_s[...] = jnp.zeros_like(run_s)
            end_blk = start_blk + nblk
            nb = be_ref.shape[0]
            blk = lax.broadcasted_iota(jnp.int32, (nb, LANES), 0).astype(F32)
            ln = lax.broadcasted_iota(jnp.int32, (nb, LANES), 1)
            is_exp = (ln >= EXP_LANE0) & (ln < EXP_LANE0 + N_EXPERTS)
            done = jnp.sum(jnp.where(is_exp & (end_blk <= blk), 1.0, 0.0), axis=-1, keepdims=True)
            bexp = jnp.minimum(done, N_EXPERTS - 1.0)
            used = jnp.max(jnp.where(is_exp, end_blk, 0.0), axis=-1, keepdims=True)
            be_ref[...] = jnp.where(ln == 0, bexp, jnp.where(ln == 1, used, 0.0)).astype(jnp.int32)

        rr = lax.broadcasted_iota(jnp.int32, (tm, tm), 0)
        cc = lax.broadcasted_iota(jnp.int32, (tm, tm), 1)
        before = (cc < rr).astype(BF16)
        prefix = jnp.dot(before, cnt.astype(BF16), preferred_element_type=F32)
        base = off_s[...] + run_s[...] + prefix
        d1 = jnp.sum(jnp.where(oh1, base, 0.0), axis=-1, keepdims=True)
        d2 = jnp.sum(jnp.where(oh2, base, 0.0), axis=-1, keepdims=True)
        run_s[...] += colsum
        dd = jnp.exp(t2 - t1)
        g1 = p_grp / (1.0 + dd)
        g2 = p_grp * dd / (1.0 + dd)
        dest_ref[...] = jnp.where(lane == 0, d1, jnp.where(lane == 1, d2, 0.0)).astype(jnp.int32)
        gate_ref[...] = jnp.where(lane == 0, g1, jnp.where(lane == 1, g2, 0.0))


def _router_call(logits, n_blocks):
    n = logits.shape[0]
    tm = min(n, 512)
    nb_pad = ((n_blocks + 7) // 8) * 8
    return pl.pallas_call(
        _router_kernel,
        grid=(2, n // tm),
        in_specs=[pl.BlockSpec((tm, LANES), lambda p, i: (i, 0))],
        out_specs=[pl.BlockSpec((tm, LANES), lambda p, i: (i * p, 0)),
                   pl.BlockSpec((tm, LANES), lambda p, i: (i * p, 0)),
                   pl.BlockSpec((nb_pad, LANES), lambda p, i: (0, 0))],
        out_shape=[jax.ShapeDtypeStruct((n, LANES), jnp.int32),
                   jax.ShapeDtypeStruct((n, LANES), F32),
                   jax.ShapeDtypeStruct((nb_pad, LANES), jnp.int32)],
        scratch_shapes=[pltpu.VMEM((1, LANES), F32), pltpu.VMEM((1, LANES), F32),
                        pltpu.VMEM((1, LANES), F32)],
        compiler_params=_cparams(("arbitrary", "arbitrary")),
        name="router",
    )(logits)


def _slots_kernel(dest_ref, zero_ref, inv_ref, sem):
    fill = pltpu.make_async_copy(zero_ref, inv_ref, sem)
    fill.start()
    fill.wait()

    def put(tok, carry):
        inv_ref[dest_ref[2 * tok]] = tok
        inv_ref[dest_ref[2 * tok + 1]] = tok
        return carry

    lax.fori_loop(0, dest_ref.shape[0] // 2, put, 0, unroll=8)


def _slots_call(dest_flat, n_rows):
    return pl.pallas_call(
        _slots_kernel,
        in_specs=[pl.BlockSpec(memory_space=pltpu.SMEM), pl.BlockSpec(memory_space=pl.ANY)],
        out_specs=pl.BlockSpec(memory_space=pltpu.SMEM),
        out_shape=jax.ShapeDtypeStruct((n_rows,), jnp.int32),
        scratch_shapes=[pltpu.SemaphoreType.DMA(())],
        name="slots",
    )(dest_flat, jnp.zeros((n_rows,), jnp.int32))


GMM_GROUP = 6
GMM_TILES = 4
WEIGHT_DMA_SPLIT = 4
WEIGHT_BUFFERS = 3


def _unit_tables(block_expert, used, n_blocks):
    n_units_max = N_EXPERTS + n_blocks // GMM_GROUP
    valid = jnp.arange(n_blocks) < used
    nblk_e = jnp.sum((block_expert[None, :] == jnp.arange(N_EXPERTS)[:, None]) & valid[None, :],
                     axis=1).astype(jnp.int32)
    first_e = jnp.cumsum(nblk_e) - nblk_e
    units_e = (nblk_e + GMM_GROUP - 1) // GMM_GROUP
    uend = jnp.cumsum(units_e)
    n_units = uend[-1]
    u = jnp.arange(n_units_max, dtype=jnp.int32)
    ue = jnp.minimum(jnp.sum(uend[None, :] <= u[:, None], axis=1), N_EXPERTS - 1).astype(jnp.int32)
    k = u - (uend - units_e)[ue]
    live = u < n_units
    ub = jnp.where(live, first_e[ue] + k * GMM_GROUP, 0).astype(jnp.int32)
    un = jnp.where(live, jnp.clip(nblk_e[ue] - k * GMM_GROUP, 0, GMM_GROUP), 0).astype(jnp.int32)
    return ue, ub, un, n_units.reshape(1).astype(jnp.int32)


def _gmm_kernel(ue_ref, ub_ref, un_ref, nu_ref, used_ref, inv_ref, h_ref, w1_ref, w3_ref, w2_ref, y_ref,
                xbuf, xs, acc, w1f, w3f, w2f, w1b_s, w3b_s, w2b_s, gsem, osem, wsem, *, n_blocks):
    u = pl.program_id(0)
    n_units = nu_ref[0]
    slot = u % 2
    d, tj = w1f.shape[1], w1f.shape[2]

    def gather(unit, s):
        base = ub_ref[unit] * MOE_BLOCK

        def body(g, carry):
            for k in range(8):
                r = g * 8 + k
                pltpu.make_async_copy(h_ref.at[pl.ds(inv_ref[base + r], 1)], xbuf.at[s, pl.ds(r, 1)],
                                      gsem.at[s]).start()
            return carry

        lax.fori_loop(0, un_ref[unit] * (MOE_BLOCK // 8), body, 0)

    def wait_gather(unit, s):
        rows = pl.ds(0, un_ref[unit] * MOE_BLOCK)
        pltpu.make_async_copy(xbuf.at[s, rows], xbuf.at[s, rows], gsem.at[s]).wait()

    def blk_rows(i):
        return pl.ds(pl.multiple_of(i * MOE_BLOCK, MOE_BLOCK), MOE_BLOCK)

    def out_copy(unit, s, i):
        dst = pl.ds(pl.multiple_of((ub_ref[unit] + i) * MOE_BLOCK, MOE_BLOCK), MOE_BLOCK)
        return pltpu.make_async_copy(acc.at[s, blk_rows(i)], y_ref.at[dst], osem.at[s])

    def wait_out(unit, s):
        def body(i, carry):
            out_copy(unit, s, i).wait()
            return carry

        lax.fori_loop(0, un_ref[unit], body, 0)

    def weight_copies(unit, j, ws):
        e = ue_ref[unit]
        cols = pl.ds(j * tj, tj)
        copies = []
        for p in range(WEIGHT_DMA_SPLIT):
            rk = pl.ds(p * (d // WEIGHT_DMA_SPLIT), d // WEIGHT_DMA_SPLIT)
            rj = pl.ds(p * (tj // WEIGHT_DMA_SPLIT), tj // WEIGHT_DMA_SPLIT)
            rj_src = pl.ds(j * tj + p * (tj // WEIGHT_DMA_SPLIT), tj // WEIGHT_DMA_SPLIT)
            copies += [pltpu.make_async_copy(w1_ref.at[e, rk, cols], w1f.at[ws, rk], wsem.at[ws]),
                       pltpu.make_async_copy(w3_ref.at[e, rk, cols], w3f.at[ws, rk], wsem.at[ws]),
                       pltpu.make_async_copy(w2_ref.at[e, rj_src, :], w2f.at[ws, rj], wsem.at[ws])]
        return copies

    n_wbuf = w1f.shape[0]
    ahead = n_wbuf - 1

    def tile_buffer(unit, j):
        return (unit * GMM_TILES + j) % n_wbuf

    def tile_after(unit, j, k):
        jj = j + k
        if jj < GMM_TILES:
            return unit, jj
        return jnp.minimum(unit + 1, n_units - 1), jj - GMM_TILES

    @pl.when(u == 0)
    def _():
        gather(0, 0)
        for k in range(ahead):
            for c in weight_copies(0, k, tile_buffer(0, k)):
                c.start()

    def tile_pass(j, ws):
        first = j == 0
        w1b = w1f[ws].astype(BF16)
        w3b = w3f[ws].astype(BF16)
        w2b = w2f[ws].astype(BF16)
        w1b_s[...] = w1b
        w3b_s[...] = w3b
        w2b_s[...] = w2b

        def rows_pass(r0, m, a1, a3, a2):
            rows = pl.ds(r0, m)
            xb = xs[rows, :]
            h1 = jnp.dot(xb, a1, preferred_element_type=F32)
            h3 = jnp.dot(xb, a3, preferred_element_type=F32)
            part = jnp.dot((_silu(h1) * h3).astype(BF16), a2, preferred_element_type=F32)
            if first:
                acc[slot, rows, :] = part
            else:
                acc[slot, rows, :] += part

        rows_pass(0, MOE_BLOCK, w1b, w3b, w2b)
        rest = un_ref[u] - 1

        def pair(i, carry):
            r0 = pl.multiple_of(MOE_BLOCK + i * 2 * MOE_BLOCK, MOE_BLOCK)
            rows_pass(r0, 2 * MOE_BLOCK, w1b_s[...], w3b_s[...], w2b_s[...])
            return carry

        lax.fori_loop(0, rest // 2, pair, 0)

        @pl.when(rest % 2 == 1)
        def _():
            rows_pass(pl.multiple_of(rest * MOE_BLOCK, MOE_BLOCK), MOE_BLOCK,
                      w1b_s[...], w3b_s[...], w2b_s[...])

    @pl.when(u < n_units)
    def _():
        @pl.when(u >= 2)
        def _():
            wait_out(u - 2, slot)

        @pl.when(u + 1 < n_units)
        def _():
            gather(u + 1, 1 - slot)

        wait_gather(u, slot)

        def to_bf16(i, carry):
            xs[blk_rows(i), :] = xbuf[slot, blk_rows(i), :].astype(BF16)
            return carry

        lax.fori_loop(0, un_ref[u], to_bf16, 0)
        for j in range(GMM_TILES):
            nu_, nj = tile_after(u, j, ahead)
            for c in weight_copies(nu_, nj, (tile_buffer(u, j) + ahead) % n_wbuf):
                c.start()
            ws = tile_buffer(u, j)
            for c in weight_copies(u, j, ws):
                c.wait()
            tile_pass(j, ws)

        def start(i, carry):
            out_copy(u, slot, i).start()
            return carry

        lax.fori_loop(0, un_ref[u], start, 0)

    @pl.when(u == pl.num_programs(0) - 1)
    def _():
        last = n_units - 1
        for k in range(ahead):
            for c in weight_copies(last, k, (tile_buffer(last, GMM_TILES - 1) + 1 + k) % n_wbuf):
                c.wait()
        wait_out(last, last % 2)

        @pl.when(n_units >= 2)
        def _():
            wait_out(last - 1, (last - 1) % 2)

        acc[0, blk_rows(0), :] = jnp.zeros((MOE_BLOCK, acc.shape[2]), F32)

        def zero_copy(b):
            dst = pl.ds(pl.multiple_of(b * MOE_BLOCK, MOE_BLOCK), MOE_BLOCK)
            return pltpu.make_async_copy(acc.at[0, blk_rows(0)], y_ref.at[dst], osem.at[0])

        def start(b, carry):
            zero_copy(b).start()
            return carry

        def wait(b, carry):
            zero_copy(b).wait()
            return carry

        lax.fori_loop(used_ref[0], n_blocks, start, 0)
        lax.fori_loop(used_ref[0], n_blocks, wait, 0)


def _gmm_call(block_expert, used, inv, h2, w1, w3, w2):
    n_rows = inv.shape[0]
    d = h2.shape[1]
    de = w1.shape[-1]
    n_blocks = n_rows // MOE_BLOCK
    tj = de // GMM_TILES
    ue, ub, un, nu = _unit_tables(block_expert, used[0], n_blocks)

    rows = GMM_GROUP * MOE_BLOCK
    grid_spec = pltpu.PrefetchScalarGridSpec(
        num_scalar_prefetch=6,
        grid=(ue.shape[0],),
        in_specs=[pl.BlockSpec(memory_space=pl.ANY)] * 4,
        out_specs=pl.BlockSpec(memory_space=pl.ANY),
        scratch_shapes=[pltpu.VMEM((2, rows, d), F32), pltpu.VMEM((rows, d), BF16),
                        pltpu.VMEM((2, rows, d), F32),
                        pltpu.VMEM((WEIGHT_BUFFERS, d, tj), F32), pltpu.VMEM((WEIGHT_BUFFERS, d, tj), F32),
                        pltpu.VMEM((WEIGHT_BUFFERS, tj, d), F32),
                        pltpu.VMEM((d, tj), BF16), pltpu.VMEM((d, tj), BF16), pltpu.VMEM((tj, d), BF16),
                        pltpu.SemaphoreType.DMA((2,)), pltpu.SemaphoreType.DMA((2,)),
                        pltpu.SemaphoreType.DMA((WEIGHT_BUFFERS,))],
    )
    return pl.pallas_call(
        functools.partial(_gmm_kernel, n_blocks=n_blocks),
        grid_spec=grid_spec,
        out_shape=jax.ShapeDtypeStruct((n_rows, d), F32),
        compiler_params=_cparams(("arbitrary",)),
        name="gmm",
    )(ue, ub, un, nu, used, inv, h2, w1, w3, w2)


def _combine_kernel(dest_ref, y_ref, x1_ref, gate_ref, mods_ref, fg_ref, o_ref, ybuf, sem, *, seq):
    i = pl.program_id(0)
    tm = x1_ref.shape[0]
    slot = i % 2

    last = pl.num_programs(0) - 1

    def row_copy(step, r, k, s):
        return pltpu.make_async_copy(y_ref.at[pl.ds(dest_ref[2 * (step * tm + r) + k], 1)],
                                     ybuf.at[s, k, pl.ds(r, 1)], sem.at[s])

    def wait_gather(s):
        pltpu.make_async_copy(ybuf.at[s], ybuf.at[s], sem.at[s]).wait()

    @pl.when(i == 0)
    def _():
        def body(r, carry):
            row_copy(0, r, 0, 0).start()
            row_copy(0, r, 1, 0).start()
            return carry

        lax.fori_loop(0, tm, body, 0, unroll=4)

    wait_gather(slot)

    nxt = jnp.minimum(i + 1, last)
    for r in range(tm):
        row_copy(nxt, r, 0, 1 - slot).start()
        row_copy(nxt, r, 1, 1 - slot).start()

    b = (i * tm) // seq
    gt2 = mods_ref[5, pl.ds(b, 1), :]
    gate = gate_ref[...]
    moe = gate[:, 0:1] * ybuf[slot, 0] + gate[:, 1:2] * ybuf[slot, 1]
    x = x1_ref[...] + gt2 * moe
    o_ref[...] = x * lax.rsqrt(jnp.mean(x * x, axis=-1, keepdims=True) + EPS) * fg_ref[...]

    @pl.when(i == last)
    def _():
        wait_gather(1 - slot)


def _combine_call(dest_flat, y_pad, x1, gate, mods, final_g, seq):
    n, d = x1.shape
    tm = min(seq, 256)
    grid_spec = pltpu.PrefetchScalarGridSpec(
        num_scalar_prefetch=1,
        grid=(n // tm,),
        in_specs=[pl.BlockSpec(memory_space=pl.ANY),
                  pl.BlockSpec((tm, d), lambda i, dest: (i, 0)),
                  pl.BlockSpec((tm, LANES), lambda i, dest: (i, 0)),
                  pl.BlockSpec((6, 8, d), lambda i, dest: (0, 0, 0)),
                  pl.BlockSpec((1, d), lambda i, dest: (0, 0))],
        out_specs=pl.BlockSpec((tm, d), lambda i, dest: (i, 0)),
        scratch_shapes=[pltpu.VMEM((2, 2, tm, d), F32), pltpu.SemaphoreType.DMA((2,))],
    )
    return pl.pallas_call(
        functools.partial(_combine_kernel, seq=seq),
        grid_spec=grid_spec,
        out_shape=jax.ShapeDtypeStruct((n, d), F32),
        compiler_params=_cparams(("arbitrary",)),
        name="combine",
    )(dest_flat, y_pad, x1, gate, mods, final_g)


def _pad_lanes(a, lane0):
    return jnp.zeros((LANES,), F32).at[lane0:lane0 + a.shape[0]].set(a.astype(F32))


def kernel(x, c, ctx, c_ctx, w_mod, b_mod, norm1_g, w_in, pool_w, pool_scale, conv_w,
           a_log_f, dt_bias_f, a_log_b, dt_bias_b, out_norm_g, w_out, norm2_g,
           w_grp, b_grp, w_rt, b_rt, w1, w3, w2, final_g):
    bsz, t, d = x.shape
    depth = w_mod.shape[0]
    assert depth == 1, "single-layer problem: the context stream is read but never updated"
    heads = a_log_f.shape[1]
    pool_width = pool_w.shape[1] * pool_w.shape[2]
    dn_width = heads * HEAD_DIM
    q0 = pool_width
    z0 = q0 + 3 * dn_width
    ab0 = z0 + dn_width
    n_tok = bsz * t
    l = 0

    c8 = jnp.zeros((8, d), F32).at[:bsz].set(c).at[bsz].set(c_ctx)
    mod = _mod_call(c8, w_mod[l], b_mod[l])
    mods = mod.reshape(8, 6, d).transpose(1, 0, 2)

    w_in_bf = w_in[l].astype(BF16)
    wab_bf = jnp.zeros((d, LANES), BF16).at[:, :4 * heads].set(w_in_bf[:, ab0:])
    g1 = norm1_g[l].reshape(1, d)
    proj, ab = _inproj_call(x, mods, g1, w_in_bf, wab_bf, ab0, None)
    proj_c, ab_c = _inproj_call(ctx, mods, g1, w_in_bf, wab_bf, ab0, bsz)

    prm = jnp.zeros((8, LANES), F32)
    prm = prm.at[0].set(_pad_lanes(jnp.concatenate([a_log_f[l], a_log_b[l]]), 2 * heads))
    prm = prm.at[1].set(_pad_lanes(jnp.concatenate([dt_bias_f[l], dt_bias_b[l]]), 2 * heads))
    g, g_t = _gates_call(ab, prm, heads)
    gc, gc_t = _gates_call(ab_c, prm, heads)
    gr = g_t.reshape(bsz, LANES, t // CHUNK, CHUNK)
    grc = gc_t.reshape(bsz, LANES, ctx.shape[1] // CHUNK, CHUNK)

    dn = _delta_call(proj, proj_c, conv_w[l], g, gc, gr, grc, out_norm_g[l].reshape(1, HEAD_DIM),
                     heads, q0 // HEAD_DIM, z0 // HEAD_DIM)
    pool = _pool_call(proj, pool_w[l], pool_scale[l].reshape(1, pool_width), t // GRID_W, GRID_W)

    wr = jnp.zeros((d, LANES), F32).at[:, :N_GROUPS].set(w_grp[l]).at[:, EXP_LANE0:EXP_LANE0 + N_EXPERTS].set(w_rt[l])
    br = jnp.zeros((LANES,), F32).at[:N_GROUPS].set(b_grp[l]).at[EXP_LANE0:EXP_LANE0 + N_EXPERTS].set(b_rt[l])
    x1, h2, logits = _outproj_call(pool, dn, w_out[l].astype(BF16), x, mods, norm2_g[l].reshape(1, d),
                                   wr, br.reshape(1, LANES))

    n_blocks = (n_tok * 2 + N_EXPERTS * (MOE_BLOCK - 1) + MOE_BLOCK - 1) // MOE_BLOCK
    dest, gate, be = _router_call(logits.reshape(n_tok, LANES), n_blocks)
    dest_flat = dest[:, :2].reshape(-1)
    block_expert = be[:n_blocks, 0]
    used = be[0:1, 1]
    inv = _slots_call(dest_flat, n_blocks * MOE_BLOCK)
    y_pad = _gmm_call(block_expert, used, inv, h2.reshape(n_tok, d), w1[l], w3[l], w2[l])
    out = _combine_call(dest_flat, y_pad, x1.reshape(n_tok, d), gate, mods, final_g.reshape(1, d), t)
    return out.reshape(bsz, t, d)
```

```python
import functools

import jax
import jax.numpy as jnp
from jax import lax
from jax.experimental import pallas as pl
from jax.experimental.pallas import tpu as pltpu

F32 = jnp.float32
BF16 = jnp.bfloat16

GRID_W = 64
POOL_WINDOWS = (2, 4, 8, 16)
HEAD_DIM = 128
CONV_WIDTH = 5
CHUNK = 64
N_GROUPS = 4
EXPERTS_PER_GROUP = 8
N_EXPERTS = N_GROUPS * EXPERTS_PER_GROUP
MOE_BLOCK = 128
EPS = 1e-6
LANES = 128
PRE_UNROLL = 8
EXP_LANE0 = N_GROUPS

VMEM_LIMIT = 56 * 1024 * 1024


def _cparams(sem):
    return pltpu.CompilerParams(dimension_semantics=sem, vmem_limit_bytes=VMEM_LIMIT)


def _dot(a, b):
    return jnp.dot(a.astype(BF16), b.astype(BF16), preferred_element_type=F32)


def _dot_split(a, b):
    a_hi = a.astype(BF16)
    b_hi = b.astype(BF16)
    a_lo = (a - a_hi.astype(F32)).astype(BF16)
    b_lo = (b - b_hi.astype(F32)).astype(BF16)
    return (jnp.dot(a_hi, b_hi, preferred_element_type=F32)
            + jnp.dot(a_lo, b_hi, preferred_element_type=F32)
            + jnp.dot(a_hi, b_lo, preferred_element_type=F32))


def _silu(x):
    return x * jax.nn.sigmoid(x)


def _mod_kernel(c_ref, w_ref, b_ref, o_ref):
    o_ref[...] = _dot(_silu(c_ref[...]), w_ref[...]) + b_ref[...]


def _mod_call(c8, w_mod, b_mod):
    d, n = w_mod.shape
    tn = 512
    return pl.pallas_call(
        _mod_kernel,
        grid=(n // tn,),
        in_specs=[pl.BlockSpec((8, d), lambda j: (0, 0)),
                  pl.BlockSpec((d, tn), lambda j: (0, j)),
                  pl.BlockSpec((1, tn), lambda j: (0, j))],
        out_specs=pl.BlockSpec((8, tn), lambda j: (0, j)),
        out_shape=jax.ShapeDtypeStruct((8, n), F32),
        compiler_params=_cparams(("parallel",)),
        name="mod",
    )(c8, w_mod, b_mod.reshape(1, n))


def _inproj_kernel(x_ref, mods_ref, g_ref, w_ref, wab_ref, o_ref, ab_ref, hn_ref, *, mod_row):
    b = pl.program_id(0)
    j = pl.program_id(2)

    @pl.when(j == 0)
    def _():
        x = x_ref[0]
        y = x * lax.rsqrt(jnp.mean(x * x, axis=-1, keepdims=True) + EPS) * g_ref[...]
        row = b if mod_row is None else mod_row
        sh = mods_ref[0, pl.ds(row, 1), :]
        sc = mods_ref[1, pl.ds(row, 1), :]
        hb = (y * (1 + sc) + sh).astype(BF16)
        hn_ref[...] = hb
        ab_ref[0] = jnp.dot(hb, wab_ref[...], preferred_element_type=F32)

    o_ref[0] = jnp.dot(hn_ref[...], w_ref[...], preferred_element_type=F32)


def _inproj_call(x, mods, g, w_in, wab_bf, n_main, mod_row):
    bsz, t, d = x.shape
    tm = min(t, 1024)
    tn = 1024
    kern = functools.partial(_inproj_kernel, mod_row=mod_row)
    return pl.pallas_call(
        kern,
        grid=(bsz, t // tm, n_main // tn),
        in_specs=[pl.BlockSpec((1, tm, d), lambda b, i, j: (b, i, 0)),
                  pl.BlockSpec((2, 8, d), lambda b, i, j: (0, 0, 0)),
                  pl.BlockSpec((1, d), lambda b, i, j: (0, 0)),
                  pl.BlockSpec((d, tn), lambda b, i, j: (0, j)),
                  pl.BlockSpec((d, LANES), lambda b, i, j: (0, 0))],
        out_specs=[pl.BlockSpec((1, tm, tn), lambda b, i, j: (b, i, j)),
                   pl.BlockSpec((1, tm, LANES), lambda b, i, j: (b, i, 0))],
        out_shape=[jax.ShapeDtypeStruct((bsz, t, n_main), F32),
                   jax.ShapeDtypeStruct((bsz, t, LANES), F32)],
        scratch_shapes=[pltpu.VMEM((tm, d), BF16)],
        compiler_params=_cparams(("parallel", "parallel", "arbitrary")),
        name="inproj",
    )(x, mods, g, w_in, wab_bf)


def _gates_kernel(ab_ref, prm_ref, g_ref, gt_ref, *, heads):
    ab = ab_ref[0]
    t = ab.shape[0]
    h2, h3, h4, h6 = 2 * heads, 3 * heads, 4 * heads, 6 * heads
    beta = jax.nn.sigmoid(ab)
    xx = ab + prm_ref[1:2, :]
    softplus = jnp.maximum(xx, 0.0) + jnp.log1p(jnp.exp(-jnp.abs(xx)))
    g = -jnp.exp(prm_ref[0:1, :]) * softplus
    pos = lax.broadcasted_iota(jnp.int32, ab.shape, 0) & (CHUNK - 1)
    cs = g
    ss = g
    s = 1
    while s < CHUNK:
        cs = cs + jnp.where(pos >= s, pltpu.roll(cs, s, 0), 0.0)
        ss = ss + jnp.where(pos < CHUNK - s, pltpu.roll(ss, t - s, 0), 0.0)
        s *= 2
    tot = pltpu.roll(cs + ss - g, h2, 1)
    lane = lax.broadcasted_iota(jnp.int32, ab.shape, 1)
    out = jnp.where(lane < h2, beta,
                    jnp.where(lane < h3, cs,
                              jnp.where(lane < h4, ss,
                                        jnp.where(lane < h6, tot, 0.0))))
    g_ref[0] = out
    gt_ref[0] = out.T


def _gates_call(ab, prm, heads):
    bsz, t, _ = ab.shape
    return pl.pallas_call(
        functools.partial(_gates_kernel, heads=heads),
        grid=(bsz,),
        in_specs=[pl.BlockSpec((1, t, LANES), lambda b: (b, 0, 0)),
                  pl.BlockSpec((8, LANES), lambda b: (0, 0))],
        out_specs=[pl.BlockSpec((1, t, LANES), lambda b: (b, 0, 0)),
                   pl.BlockSpec((1, LANES, t), lambda b: (b, 0, 0))],
        out_shape=[jax.ShapeDtypeStruct((bsz, t, LANES), F32),
                   jax.ShapeDtypeStruct((bsz, LANES, t), F32)],
        compiler_params=_cparams(("parallel",)),
        name="gates",
    )(ab, prm)


def _conv_silu(x, w):
    n = x.shape[0]
    row = lax.broadcasted_iota(jnp.int32, x.shape, 0)
    acc = x * w[CONV_WIDTH // 2:CONV_WIDTH // 2 + 1, :]
    for j in range(CONV_WIDTH):
        d = j - CONV_WIDTH // 2
        if d == 0:
            continue
        xs = pltpu.roll(x, (-d) % n, 0)
        valid = (row + d >= 0) & (row + d < n)
        acc = acc + jnp.where(valid, xs, 0.0) * w[j:j + 1, :]
    return _silu(acc)


def _l2norm(a):
    return a * lax.rsqrt(jnp.sum(a * a, axis=-1, keepdims=True) + EPS)


def _lane_col(g, lane_idx):
    lane = lax.broadcasted_iota(jnp.int32, g.shape, 1)
    return jnp.sum(jnp.where(lane == lane_idx, g, 0.0), axis=-1, keepdims=True)


def _chunk_terms(chains, between=()):
    pending = list(between)

    def stage_done():
        if pending:
            pending.pop(0)()

    c, hd = chains[0][0].shape
    ri = lax.broadcasted_iota(jnp.int32, (c, c), 0)
    ci = lax.broadcasted_iota(jnp.int32, (c, c), 1)
    eye = jnp.where(ri == ci, 1.0, 0.0)
    right = lax.broadcasted_iota(jnp.int32, (c, 2 * c), 1) >= c
    nt = (((1,), (1,)), ((), ()))
    tn = (((0,), (0,)), ((), ()))

    decs, kn_bs, zs = [], [], []
    for kn_c, kb_c, _, _, _, _, gcc, gcr, upper in chains:
        incl = (ri <= ci) if upper else (ri >= ci)
        strict = (ri < ci) if upper else (ri > ci)
        dec = jnp.where(incl, jnp.exp(jnp.where(incl, gcc - gcr, 0.0)), 0.0)
        kn_b = kn_c.astype(BF16)
        kk = lax.dot_general(kb_c.astype(BF16), kn_b, nt, preferred_element_type=F32)
        decs.append(dec)
        kn_bs.append(kn_b)
        zs.append(jnp.concatenate([-jnp.where(strict, kk * dec, 0.0), eye], axis=1))
    stage_done()
    n = 1
    while n < c:
        zs = [_dot(z[:, :c], z) + jnp.where(right, z, 0.0) for z in zs]
        stage_done()
        n *= 2
    uw_bs = [_dot(z[:, c:], ch[3]).astype(BF16) for z, ch in zip(zs, chains)]
    stage_done()
    wns = [lax.dot_general(ch[4].astype(BF16), uw_b, tn, preferred_element_type=F32)
           for uw_b, ch in zip(uw_bs, chains)]
    stage_done()
    qks = [None if ch[2] is None else
           lax.dot_general(ch[2].astype(BF16), kn_b, nt, preferred_element_type=F32) * dec
           for ch, kn_b, dec in zip(chains, kn_bs, decs)]
    stage_done()
    qws = [None if qk is None else jnp.dot(qk.astype(BF16), uw_b, preferred_element_type=F32)
           for qk, uw_b in zip(qks, uw_bs)]
    while pending:
        stage_done()
    out = []
    for ch, wn, qw in zip(chains, wns, qws):
        nc, w2 = wn[:, :hd], wn[:, hd:]
        if qw is None:
            out.append((w2, nc, None, None))
        else:
            out.append((w2, nc, ch[5] - qw[:, hd:], qw[:, :hd]))
    return out


def _delta_kernel(q_ref, k_ref, v_ref, z_ref, kc_ref, vc_ref, cwq_ref, cwk_ref, cwv_ref,
                  g_ref, gc_ref, grf_ref, grb_ref, grcf_ref, grcb_ref, ong_ref,
                  out_ref,
                  qn_s, kn_s, kb_s, rhs_s, kd_s, qd_s, col_s,
                  knc_s, kbc_s, rhsc_s, kdc_s, colc_s,
                  w2_s, nc_s, qp_s, o0_s, w2c_s, ncc_s, o_s, *, heads):
    h = pl.program_id(1)
    t = q_ref.shape[1]
    tc = kc_ref.shape[1]
    hd = HEAD_DIM
    n_lat = t // CHUNK
    n_ctx = tc // CHUNK

    qn = _l2norm(_conv_silu(q_ref[0], cwq_ref[...])) * (hd ** -0.5)
    kn = _l2norm(_conv_silu(k_ref[0], cwk_ref[...]))
    vv = _conv_silu(v_ref[0], cwv_ref[...])
    knc = _l2norm(_conv_silu(kc_ref[0], cwk_ref[...]))
    vvc = _conv_silu(vc_ref[0], cwv_ref[...])
    qn_s[...] = qn
    kn_s[...] = kn
    knc_s[...] = knc
    g_lat = g_ref[0]
    g_ctx = gc_ref[0]

    for d in range(2):
        for (gt_, kn_, vv_, kb_r, rhs_r, kd_r, col_r, qn_, qd_r) in (
                (g_lat, kn, vv, kb_s, rhs_s, kd_s, col_s, qn, qd_s),
                (g_ctx, knc, vvc, kbc_s, rhsc_s, kdc_s, colc_s, None, None)):
            beta = _lane_col(gt_, d * heads + h)
            gcum = _lane_col(gt_, (2 + d) * heads + h)
            gtot = _lane_col(gt_, (4 + d) * heads + h)
            e = jnp.exp(gcum)
            kb = kn_ * beta
            kb_r[d] = kb
            rhs_r[d, :, 0:hd] = vv_ * beta
            rhs_r[d, :, hd:2 * hd] = kb * e
            kd_r[d] = kn_ * jnp.exp(gtot - gcum)
            lane = lax.broadcasted_iota(jnp.int32, (gcum.shape[0], LANES), 1)
            col_r[d] = jnp.where(lane == 0, gcum, jnp.where(lane == 1, jnp.exp(gtot), 0.0))
            if qn_ is not None:
                qd_r[d] = qn_ * e

    gr_lat = (grf_ref, grb_ref)
    gr_ctx = (grcf_ref, grcb_ref)

    un_ctx = min(PRE_UNROLL, n_ctx)
    un_lat = min(PRE_UNROLL, n_lat)

    def pre_ctx(i, carry):
        ids = [(i * un_ctx + u, d) for u in range(un_ctx) for d in range(2)]
        chains = []
        for c, d in ids:
            rows = pl.ds(pl.multiple_of(c * CHUNK, CHUNK), CHUNK)
            chains.append((knc_s[rows, :], kbc_s[d, rows, :], None, rhsc_s[d, rows, :], kdc_s[d, rows, :],
                           None, colc_s[d, rows, 0:1], gr_ctx[d][0, 0, pl.ds(c, 1), :], d == 1))
        for (c, d), (w2, nc, _, _) in zip(ids, _chunk_terms(chains)):
            m0 = pl.multiple_of(c * hd, hd)
            w2c_s[d, pl.ds(m0, hd), :] = w2
            ncc_s[d, pl.ds(m0, hd), :] = nc
        return carry

    lax.fori_loop(0, n_ctx // un_ctx, pre_ctx, 0)

    def lat_chunk(i, u, d):
        k = i * un_lat + u
        return k if d == 0 else n_lat - 1 - k

    def pre_lat(i, between=()):
        ids = [(lat_chunk(i, u, d), d) for u in range(un_lat) for d in range(2)]
        chains = []
        for c, d in ids:
            rows = pl.ds(pl.multiple_of(c * CHUNK, CHUNK), CHUNK)
            chains.append((kn_s[rows, :], kb_s[d, rows, :], qn_s[rows, :], rhs_s[d, rows, :], kd_s[d, rows, :],
                           qd_s[d, rows, :], col_s[d, rows, 0:1], gr_lat[d][0, 0, pl.ds(c, 1), :], d == 1))
        for (c, d), (w2, nc, qp, o0) in zip(ids, _chunk_terms(chains, between)):
            rows = pl.ds(pl.multiple_of(c * CHUNK, CHUNK), CHUNK)
            m0 = pl.multiple_of(c * hd, hd)
            w2_s[d, pl.ds(m0, hd), :] = w2
            nc_s[d, pl.ds(m0, hd), :] = nc
            qp_s[d, rows, :] = qp
            o0_s[d, rows, :] = o0

    def scan_ctx_steps(box):
        def step(i):
            def run():
                for d in range(2):
                    c = i if d == 0 else n_ctx - 1 - i
                    gt = colc_s[d, c * CHUNK:c * CHUNK + 1, 1:2]
                    box[d] = (gt * box[d] + ncc_s[d, c * hd:(c + 1) * hd, :]
                              - _dot(w2c_s[d, c * hd:(c + 1) * hd, :], box[d]))
            return run

        return [step(i) for i in range(n_ctx)]

    def scan_lat_steps(i, box):
        def step(u):
            def run():
                for d in range(2):
                    s = box[d]
                    c = lat_chunk(i, u, d)
                    r0 = pl.multiple_of(c * CHUNK, CHUNK)
                    rows = pl.ds(r0, CHUNK)
                    m0 = pl.multiple_of(c * hd, hd)
                    s_b = s.astype(BF16)
                    o_s[d, rows, :] = jnp.dot(qp_s[d, rows, :].astype(BF16), s_b,
                                              preferred_element_type=F32) + o0_s[d, rows, :]
                    gt = col_s[d, pl.ds(r0, 1), 1:2]
                    box[d] = gt * s + nc_s[d, pl.ds(m0, hd), :] - jnp.dot(
                        w2_s[d, pl.ds(m0, hd), :].astype(BF16), s_b, preferred_element_type=F32)
            return run

        return [step(u) for u in range(un_lat)]

    zero = jnp.zeros((hd, hd), F32)
    box = [zero, zero]
    pre_lat(0, scan_ctx_steps(box))
    states = tuple(box)

    def lat_body(i, states):
        box = list(states)
        pre_lat(i, scan_lat_steps(i - 1, box))
        return tuple(box)

    states = lax.fori_loop(1, n_lat // un_lat, lat_body, states)
    box = list(states)
    for run in scan_lat_steps(n_lat // un_lat - 1, box):
        run()

    o = o_s[0] + o_s[1]
    o = o * lax.rsqrt(jnp.mean(o * o, axis=-1, keepdims=True) + EPS) * ong_ref[...]
    out_ref[0] = (o * _silu(z_ref[0])).astype(out_ref.dtype)


def _delta_call(proj, proj_c, conv_w, g, gc, gr, grc, ong, heads, q_blk0, z_blk0):
    bsz, t, _ = proj.shape
    tc = proj_c.shape[1]
    hd = HEAD_DIM
    n_lat, n_ctx = t // CHUNK, tc // CHUNK

    def col(off):
        return lambda b, h: (b, 0, off + h)

    def cw(off):
        return lambda b, h: (0, off + h)

    def grow(off):
        return lambda b, h: (b, off + h, 0, 0)

    in_specs = [
        pl.BlockSpec((1, t, hd), col(q_blk0)),
        pl.BlockSpec((1, t, hd), col(q_blk0 + heads)),
        pl.BlockSpec((1, t, hd), col(q_blk0 + 2 * heads)),
        pl.BlockSpec((1, t, hd), col(z_blk0)),
        pl.BlockSpec((1, tc, hd), col(q_blk0 + heads)),
        pl.BlockSpec((1, tc, hd), col(q_blk0 + 2 * heads)),
        pl.BlockSpec((CONV_WIDTH, hd), cw(0)),
        pl.BlockSpec((CONV_WIDTH, hd), cw(heads)),
        pl.BlockSpec((CONV_WIDTH, hd), cw(2 * heads)),
        pl.BlockSpec((1, t, LANES), lambda b, h: (b, 0, 0)),
        pl.BlockSpec((1, tc, LANES), lambda b, h: (b, 0, 0)),
        pl.BlockSpec((1, 1, n_lat, CHUNK), grow(2 * heads)),
        pl.BlockSpec((1, 1, n_lat, CHUNK), grow(3 * heads)),
        pl.BlockSpec((1, 1, n_ctx, CHUNK), grow(2 * heads)),
        pl.BlockSpec((1, 1, n_ctx, CHUNK), grow(3 * heads)),
        pl.BlockSpec((1, hd), lambda b, h: (0, 0)),
    ]
    scratch = [
        pltpu.VMEM((t, hd), F32), pltpu.VMEM((t, hd), F32),
        pltpu.VMEM((2, t, hd), F32), pltpu.VMEM((2, t, 2 * hd), F32),
        pltpu.VMEM((2, t, hd), F32), pltpu.VMEM((2, t, hd), F32),
        pltpu.VMEM((2, t, LANES), F32),
        pltpu.VMEM((tc, hd), F32), pltpu.VMEM((2, tc, hd), F32),
        pltpu.VMEM((2, tc, 2 * hd), F32), pltpu.VMEM((2, tc, hd), F32),
        pltpu.VMEM((2, tc, LANES), F32),
        pltpu.VMEM((2, n_lat * hd, hd), F32), pltpu.VMEM((2, n_lat * hd, hd), F32),
        pltpu.VMEM((2, t, hd), F32), pltpu.VMEM((2, t, hd), F32),
        pltpu.VMEM((2, n_ctx * hd, hd), F32), pltpu.VMEM((2, n_ctx * hd, hd), F32),
        pltpu.VMEM((2, t, hd), F32),
    ]
    return pl.pallas_call(
        functools.partial(_delta_kernel, heads=heads),
        grid=(bsz, heads),
        in_specs=in_specs,
        out_specs=pl.BlockSpec((1, t, hd), lambda b, h: (b, 0, h)),
        out_shape=jax.ShapeDtypeStruct((bsz, t, heads * hd), BF16),
        scratch_shapes=scratch,
        compiler_params=_cparams(("parallel", "parallel")),
        name="delta",
    )(proj, proj, proj, proj, proj_c, proj_c, conv_w, conv_w, conv_w, g, gc, gr, gr, grc, grc, ong)


def _shift_rows(x, d, idx, size, stride):
    n = x.shape[0]
    xs = pltpu.roll(x, (-d * stride) % n, 0)
    return jnp.where((idx + d >= 0) & (idx + d < size), xs, 0.0)


def _box_sum_1d(x, win, idx, size, stride):
    m = win // 2
    lead = x
    trail = x
    k = 1
    while k < m:
        lead = lead + _shift_rows(lead, k, idx, size, stride)
        trail = trail + _shift_rows(trail, -k, idx, size, stride)
        k *= 2
    return lead + _shift_rows(trail, -1, idx, size, stride)


def _pool_kernel(u_ref, pw_ref, ps_ref, o_ref, *, rows, cols):
    t = u_ref.shape[1]
    gc = pw_ref.shape[1]
    tok = lax.broadcasted_iota(jnp.int32, (t, gc), 0)
    ci = tok % cols
    ri = tok // cols
    for gi, win in enumerate(POOL_WINDOWS):
        lo = win // 2
        hi = win - lo
        u = u_ref[0, :, gi * gc:(gi + 1) * gc]
        s = _box_sum_1d(u, win, ci, cols, 1)
        s = _box_sum_1d(s, win, ri, rows, cols)
        cnt_c = jnp.minimum(ci + hi, cols) - jnp.maximum(ci - lo, 0)
        cnt_r = jnp.minimum(ri + hi, rows) - jnp.maximum(ri - lo, 0)
        mean = s / (cnt_c * cnt_r).astype(F32)
        y = _dot(mean - u, pw_ref[gi]) * ps_ref[:, gi * gc:(gi + 1) * gc]
        o_ref[0, :, gi * gc:(gi + 1) * gc] = y.astype(o_ref.dtype)


def _pool_call(proj, pool_w, pool_scale, rows, cols):
    bsz, t, _ = proj.shape
    ng, gc, _ = pool_w.shape
    pwid = ng * gc
    return pl.pallas_call(
        functools.partial(_pool_kernel, rows=rows, cols=cols),
        grid=(bsz,),
        in_specs=[pl.BlockSpec((1, t, pwid), lambda b: (b, 0, 0)),
                  pl.BlockSpec((ng, gc, gc), lambda b: (0, 0, 0)),
                  pl.BlockSpec((1, pwid), lambda b: (0, 0))],
        out_specs=pl.BlockSpec((1, t, pwid), lambda b: (b, 0, 0)),
        out_shape=jax.ShapeDtypeStruct((bsz, t, pwid), BF16),
        compiler_params=_cparams(("parallel",)),
        name="pool",
    )(proj, pool_w, pool_scale)


def _outproj_kernel(pool_ref, dn_ref, wa_ref, wb_ref, x_ref, mods_ref, g2_ref, wr_ref, br_ref,
                    x1_ref, h2_ref, lg_ref):
    b = pl.program_id(0)
    mix = (jnp.dot(pool_ref[0], wa_ref[...], preferred_element_type=F32)
           + jnp.dot(dn_ref[0], wb_ref[...], preferred_element_type=F32))
    gt1 = mods_ref[2, pl.ds(b, 1), :]
    sh2 = mods_ref[3, pl.ds(b, 1), :]
    sc2 = mods_ref[4, pl.ds(b, 1), :]
    x1 = x_ref[0] + gt1 * mix
    x1_ref[0] = x1
    y = x1 * lax.rsqrt(jnp.mean(x1 * x1, axis=-1, keepdims=True) + EPS) * g2_ref[...]
    h2 = y * (1 + sc2) + sh2
    h2_ref[0] = h2
    lg_ref[0] = _dot_split(h2, wr_ref[...]) + br_ref[...]


def _outproj_call(pool, dn, w_out_bf, x, mods, g2, wr, br):
    bsz, t, d = x.shape
    half = pool.shape[-1]
    tm = min(t, 512)
    return pl.pallas_call(
        _outproj_kernel,
        grid=(bsz, t // tm),
        in_specs=[pl.BlockSpec((1, tm, half), lambda b, i: (b, i, 0)),
                  pl.BlockSpec((1, tm, half), lambda b, i: (b, i, 0)),
                  pl.BlockSpec((half, d), lambda b, i: (0, 0)),
                  pl.BlockSpec((half, d), lambda b, i: (1, 0)),
                  pl.BlockSpec((1, tm, d), lambda b, i: (b, i, 0)),
                  pl.BlockSpec((6, 8, d), lambda b, i: (0, 0, 0)),
                  pl.BlockSpec((1, d), lambda b, i: (0, 0)),
                  pl.BlockSpec((d, LANES), lambda b, i: (0, 0)),
                  pl.BlockSpec((1, LANES), lambda b, i: (0, 0))],
        out_specs=[pl.BlockSpec((1, tm, d), lambda b, i: (b, i, 0)),
                   pl.BlockSpec((1, tm, d), lambda b, i: (b, i, 0)),
                   pl.BlockSpec((1, tm, LANES), lambda b, i: (b, i, 0))],
        out_shape=[jax.ShapeDtypeStruct((bsz, t, d), F32),
                   jax.ShapeDtypeStruct((bsz, t, d), F32),
                   jax.ShapeDtypeStruct((bsz, t, LANES), F32)],
        compiler_params=_cparams(("parallel", "parallel")),
        name="outproj",
    )(pool, dn, w_out_bf, w_out_bf, x, mods, g2, wr, br)


def _router_kernel(lg_ref, dest_ref, gate_ref, be_ref, cnt_s, run_s, off_s):
    p = pl.program_id(0)
    i = pl.program_id(1)
    lg = lg_ref[...]
    tm = lg.shape[0]
    lane = lax.broadcasted_iota(jnp.int32, lg.shape, 1)
    lane_f = lane.astype(F32)
    neg = -jnp.inf
    big = float(LANES)

    grp = jnp.where(lane < N_GROUPS, lg, neg)
    gmax = jnp.max(grp, axis=-1, keepdims=True)
    gidx = jnp.min(jnp.where(grp == gmax, lane_f, big), axis=-1, keepdims=True)
    p_grp = 1.0 / jnp.sum(jnp.where(lane < N_GROUPS, jnp.exp(lg - gmax), 0.0), axis=-1, keepdims=True)
    lo = EXP_LANE0 + EXPERTS_PER_GROUP * gidx
    ev = jnp.where((lane_f >= lo) & (lane_f < lo + EXPERTS_PER_GROUP), lg, neg)
    t1 = jnp.max(ev, axis=-1, keepdims=True)
    i1 = jnp.min(jnp.where(ev == t1, lane_f, big), axis=-1, keepdims=True)
    ev2 = jnp.where(lane_f == i1, neg, ev)
    t2 = jnp.max(ev2, axis=-1, keepdims=True)
    i2 = jnp.min(jnp.where(ev2 == t2, lane_f, big), axis=-1, keepdims=True)
    oh1 = lane_f == i1
    oh2 = lane_f == i2
    cnt = oh1.astype(F32) + oh2.astype(F32)
    colsum = jnp.sum(cnt, axis=0, keepdims=True)

    @pl.when(p == 0)
    def _():
        @pl.when(i == 0)
        def _():
            cnt_s[...] = jnp.zeros_like(cnt_s)

        cnt_s[...] += colsum

    @pl.when(p == 1)
    def _():
        @pl.when(i == 0)
        def _():
            nblk = jnp.floor((cnt_s[...] + (MOE_BLOCK - 1)) * (1.0 / MOE_BLOCK))
            r = lax.broadcasted_iota(jnp.int32, (LANES, LANES), 0)
            c = lax.broadcasted_iota(jnp.int32, (LANES, LANES), 1)
            tri = (r < c).astype(BF16)
            nb8 = jnp.broadcast_to(nblk, (8, LANES))
            start_blk = jnp.dot(nb8.astype(BF16), tri, preferred_element_type=F32)[0:1, :]
            off_s[...] = start_blk * MOE_BLOCK
            run_s[...] = jnp.zeros_like(run_s)
            end_blk = start_blk + nblk
            nb = be_ref.shape[0]
            blk = lax.broadcasted_iota(jnp.int32, (nb, LANES), 0).astype(F32)
            ln = lax.broadcasted_iota(jnp.int32, (nb, LANES), 1)
            is_exp = (ln >= EXP_LANE0) & (ln < EXP_LANE0 + N_EXPERTS)
            done = jnp.sum(jnp.where(is_exp & (end_blk <= blk), 1.0, 0.0), axis=-1, keepdims=True)
            bexp = jnp.minimum(done, N_EXPERTS - 1.0)
            used = jnp.max(jnp.where(is_exp, end_blk, 0.0), axis=-1, keepdims=True)
            be_ref[...] = jnp.where(ln == 0, bexp, jnp.where(ln == 1, used, 0.0)).astype(jnp.int32)

        rr = lax.broadcasted_iota(jnp.int32, (tm, tm), 0)
        cc = lax.broadcasted_iota(jnp.int32, (tm, tm), 1)
        before = (cc < rr).astype(BF16)
        prefix = jnp.dot(before, cnt.astype(BF16), preferred_element_type=F32)
        base = off_s[...] + run_s[...] + prefix
        d1 = jnp.sum(jnp.where(oh1, base, 0.0), axis=-1, keepdims=True)
        d2 = jnp.sum(jnp.where(oh2, base, 0.0), axis=-1, keepdims=True)
        run_s[...] += colsum
        dd = jnp.exp(t2 - t1)
        g1 = p_grp / (1.0 + dd)
        g2 = p_grp * dd / (1.0 + dd)
        dest_ref[...] = jnp.where(lane == 0, d1, jnp.where(lane == 1, d2, 0.0)).astype(jnp.int32)
        gate_ref[...] = jnp.where(lane == 0, g1, jnp.where(lane == 1, g2, 0.0))


def _router_call(logits, n_blocks):
    n = logits.shape[0]
    tm = min(n, 512)
    nb_pad = ((n_blocks + 7) // 8) * 8
    return pl.pallas_call(
        _router_kernel,
        grid=(2, n // tm),
        in_specs=[pl.BlockSpec((tm, LANES), lambda p, i: (i, 0))],
        out_specs=[pl.BlockSpec((tm, LANES), lambda p, i: (i * p, 0)),
                   pl.BlockSpec((tm, LANES), lambda p, i: (i * p, 0)),
                   pl.BlockSpec((nb_pad, LANES), lambda p, i: (0, 0))],
        out_shape=[jax.ShapeDtypeStruct((n, LANES), jnp.int32),
                   jax.ShapeDtypeStruct((n, LANES), F32),
                   jax.ShapeDtypeStruct((nb_pad, LANES), jnp.int32)],
        scratch_shapes=[pltpu.VMEM((1, LANES), F32), pltpu.VMEM((1, LANES), F32),
                        pltpu.VMEM((1, LANES), F32)],
        compiler_params=_cparams(("arbitrary", "arbitrary")),
        name="router",
    )(logits)


def _slots_kernel(dest_ref, zero_ref, inv_ref, sem):
    fill = pltpu.make_async_copy(zero_ref, inv_ref, sem)
    fill.start()
    fill.wait()

    def put(tok, carry):
        inv_ref[dest_ref[2 * tok]] = tok
        inv_ref[dest_ref[2 * tok + 1]] = tok
        return carry

    lax.fori_loop(0, dest_ref.shape[0] // 2, put, 0, unroll=8)


def _slots_call(dest_flat, n_rows):
    return pl.pallas_call(
        _slots_kernel,
        in_specs=[pl.BlockSpec(memory_space=pltpu.SMEM), pl.BlockSpec(memory_space=pl.ANY)],
        out_specs=pl.BlockSpec(memory_space=pltpu.SMEM),
        out_shape=jax.ShapeDtypeStruct((n_rows,), jnp.int32),
        scratch_shapes=[pltpu.SemaphoreType.DMA(())],
        name="slots",
    )(dest_flat, jnp.zeros((n_rows,), jnp.int32))


GMM_GROUP = 5
GMM_TILES = 4
WEIGHT_DMA_SPLIT = 4
WEIGHT_BUFFERS = 4


def _unit_tables(block_expert, used, n_blocks):
    n_units_max = N_EXPERTS + n_blocks // GMM_GROUP
    valid = jnp.arange(n_blocks) < used
    nblk_e = jnp.sum((block_expert[None, :] == jnp.arange(N_EXPERTS)[:, None]) & valid[None, :],
                     axis=1).astype(jnp.int32)
    first_e = jnp.cumsum(nblk_e) - nblk_e
    units_e = (nblk_e + GMM_GROUP - 1) // GMM_GROUP
    uend = jnp.cumsum(units_e)
    n_units = uend[-1]
    u = jnp.arange(n_units_max, dtype=jnp.int32)
    ue = jnp.minimum(jnp.sum(uend[None, :] <= u[:, None], axis=1), N_EXPERTS - 1).astype(jnp.int32)
    k = u - (uend - units_e)[ue]
    live = u < n_units
    ub = jnp.where(live, first_e[ue] + k * GMM_GROUP, 0).astype(jnp.int32)
    un = jnp.where(live, jnp.clip(nblk_e[ue] - k * GMM_GROUP, 0, GMM_GROUP), 0).astype(jnp.int32)
    return ue, ub, un, n_units.reshape(1).astype(jnp.int32)


def _gmm_kernel(ue_ref, ub_ref, un_ref, nu_ref, used_ref, inv_ref, h_ref, w1_ref, w3_ref, w2_ref, y_ref,
                xbuf, xs, acc, w1f, w3f, w2f, w1b_s, w3b_s, w2b_s, gsem, osem, wsem, *, n_blocks):
    u = pl.program_id(0)
    n_units = nu_ref[0]
    slot = u % 2
    d, tj = w1f.shape[1], w1f.shape[2]

    def gather(unit, s):
        base = ub_ref[unit] * MOE_BLOCK

        def body(g, carry):
            for k in range(8):
                r = g * 8 + k
                pltpu.make_async_copy(h_ref.at[pl.ds(inv_ref[base + r], 1)], xbuf.at[s, pl.ds(r, 1)],
                                      gsem.at[s]).start()
            return carry

        lax.fori_loop(0, un_ref[unit] * (MOE_BLOCK // 8), body, 0)

    def wait_gather(unit, s):
        rows = pl.ds(0, un_ref[unit] * MOE_BLOCK)
        pltpu.make_async_copy(xbuf.at[s, rows], xbuf.at[s, rows], gsem.at[s]).wait()

    def blk_rows(i):
        return pl.ds(pl.multiple_of(i * MOE_BLOCK, MOE_BLOCK), MOE_BLOCK)

    def out_copy(unit, s, i):
        dst = pl.ds(pl.multiple_of((ub_ref[unit] + i) * MOE_BLOCK, MOE_BLOCK), MOE_BLOCK)
        return pltpu.make_async_copy(acc.at[s, blk_rows(i)], y_ref.at[dst], osem.at[s])

    def wait_out(unit, s):
        def body(i, carry):
            out_copy(unit, s, i).wait()
            return carry

        lax.fori_loop(0, un_ref[unit], body, 0)

    def weight_copies(unit, j, ws):
        e = ue_ref[unit]
        cols = pl.ds(j * tj, tj)
        copies = []
        for p in range(WEIGHT_DMA_SPLIT):
            rk = pl.ds(p * (d // WEIGHT_DMA_SPLIT), d // WEIGHT_DMA_SPLIT)
            rj = pl.ds(p * (tj // WEIGHT_DMA_SPLIT), tj // WEIGHT_DMA_SPLIT)
            rj_src = pl.ds(j * tj + p * (tj // WEIGHT_DMA_SPLIT), tj // WEIGHT_DMA_SPLIT)
            copies += [pltpu.make_async_copy(w1_ref.at[e, rk, cols], w1f.at[ws, rk], wsem.at[ws]),
                       pltpu.make_async_copy(w3_ref.at[e, rk, cols], w3f.at[ws, rk], wsem.at[ws]),
                       pltpu.make_async_copy(w2_ref.at[e, rj_src, :], w2f.at[ws, rj], wsem.at[ws])]
        return copies

    n_wbuf = w1f.shape[0]
    ahead = n_wbuf - 1

    def tile_buffer(unit, j):
        return (unit * GMM_TILES + j) % n_wbuf

    def tile_after(unit, j, k):
        jj = j + k
        if jj < GMM_TILES:
            return unit, jj
        return jnp.minimum(unit + 1, n_units - 1), jj - GMM_TILES

    @pl.when(u == 0)
    def _():
        gather(0, 0)
        for k in range(ahead):
            for c in weight_copies(0, k, tile_buffer(0, k)):
                c.start()

    def tile_pass(j, ws):
        first = j == 0
        w1b = w1f[ws].astype(BF16)
        w3b = w3f[ws].astype(BF16)
        w2b = w2f[ws].astype(BF16)
        w1b_s[...] = w1b
        w3b_s[...] = w3b
        w2b_s[...] = w2b

        def rows_pass(r0, m, a1, a3, a2):
            rows = pl.ds(r0, m)
            xb = xs[rows, :]
            h1 = jnp.dot(xb, a1, preferred_element_type=F32)
            h3 = jnp.dot(xb, a3, preferred_element_type=F32)
            part = jnp.dot((_silu(h1) * h3).astype(BF16), a2, preferred_element_type=F32)
            if first:
                acc[slot, rows, :] = part
            else:
                acc[slot, rows, :] += part

        rows_pass(0, MOE_BLOCK, w1b, w3b, w2b)
        rest = un_ref[u] - 1

        def pair(i, carry):
            r0 = pl.multiple_of(MOE_BLOCK + i * 2 * MOE_BLOCK, MOE_BLOCK)
            rows_pass(r0, 2 * MOE_BLOCK, w1b_s[...], w3b_s[...], w2b_s[...])
            return carry

        lax.fori_loop(0, rest // 2, pair, 0)

        @pl.when(rest % 2 == 1)
        def _():
            rows_pass(pl.multiple_of(rest * MOE_BLOCK, MOE_BLOCK), MOE_BLOCK,
                      w1b_s[...], w3b_s[...], w2b_s[...])

    @pl.when(u < n_units)
    def _():
        @pl.when(u >= 2)
        def _():
            wait_out(u - 2, slot)

        @pl.when(u + 1 < n_units)
        def _():
            gather(u + 1, 1 - slot)

        wait_gather(u, slot)

        def to_bf16(i, carry):
            xs[blk_rows(i), :] = xbuf[slot, blk_rows(i), :].astype(BF16)
            return carry

        lax.fori_loop(0, un_ref[u], to_bf16, 0)
        for j in range(GMM_TILES):
            nu_, nj = tile_after(u, j, ahead)
            for c in weight_copies(nu_, nj, (tile_buffer(u, j) + ahead) % n_wbuf):
                c.start()
            ws = tile_buffer(u, j)
            for c in weight_copies(u, j, ws):
                c.wait()
            tile_pass(j, ws)

        def start(i, carry):
            out_copy(u, slot, i).start()
            return carry

        lax.fori_loop(0, un_ref[u], start, 0)

    @pl.when(u == pl.num_programs(0) - 1)
    def _():
        last = n_units - 1
        for k in range(ahead):
            for c in weight_copies(last, k, (tile_buffer(last, GMM_TILES - 1) + 1 + k) % n_wbuf):
                c.wait()
        wait_out(last, last % 2)

        @pl.when(n_units >= 2)
        def _():
            wait_out(last - 1, (last - 1) % 2)

        acc[0, blk_rows(0), :] = jnp.zeros((MOE_BLOCK, acc.shape[2]), F32)

        def zero_copy(b):
            dst = pl.ds(pl.multiple_of(b * MOE_BLOCK, MOE_BLOCK), MOE_BLOCK)
            return pltpu.make_async_copy(acc.at[0, blk_rows(0)], y_ref.at[dst], osem.at[0])

        def start(b, carry):
            zero_copy(b).start()
            return carry

        def wait(b, carry):
            zero_copy(b).wait()
            return carry

        lax.fori_loop(used_ref[0], n_blocks, start, 0)
        lax.fori_loop(used_ref[0], n_blocks, wait, 0)


def _gmm_call(block_expert, used, inv, h2, w1, w3, w2):
    n_rows = inv.shape[0]
    d = h2.shape[1]
    de = w1.shape[-1]
    n_blocks = n_rows // MOE_BLOCK
    tj = de // GMM_TILES
    ue, ub, un, nu = _unit_tables(block_expert, used[0], n_blocks)

    rows = GMM_GROUP * MOE_BLOCK
    grid_spec = pltpu.PrefetchScalarGridSpec(
        num_scalar_prefetch=6,
        grid=(ue.shape[0],),
        in_specs=[pl.BlockSpec(memory_space=pl.ANY)] * 4,
        out_specs=pl.BlockSpec(memory_space=pl.ANY),
        scratch_shapes=[pltpu.VMEM((2, rows, d), F32), pltpu.VMEM((rows, d), BF16),
                        pltpu.VMEM((2, rows, d), F32),
                        pltpu.VMEM((WEIGHT_BUFFERS, d, tj), F32), pltpu.VMEM((WEIGHT_BUFFERS, d, tj), F32),
                        pltpu.VMEM((WEIGHT_BUFFERS, tj, d), F32),
                        pltpu.VMEM((d, tj), BF16), pltpu.VMEM((d, tj), BF16), pltpu.VMEM((tj, d), BF16),
                        pltpu.SemaphoreType.DMA((2,)), pltpu.SemaphoreType.DMA((2,)),
                        pltpu.SemaphoreType.DMA((WEIGHT_BUFFERS,))],
    )
    return pl.pallas_call(
        functools.partial(_gmm_kernel, n_blocks=n_blocks),
        grid_spec=grid_spec,
        out_shape=jax.ShapeDtypeStruct((n_rows, d), F32),
        compiler_params=_cparams(("arbitrary",)),
        name="gmm",
    )(ue, ub, un, nu, used, inv, h2, w1, w3, w2)


def _combine_kernel(dest_ref, y_ref, x1_ref, gate_ref, mods_ref, fg_ref, o_ref, ybuf, sem, *, seq):
    i = pl.program_id(0)
    tm = x1_ref.shape[0]
    slot = i % 2

    last = pl.num_programs(0) - 1

    def row_copy(step, r, k, s):
        return pltpu.make_async_copy(y_ref.at[pl.ds(dest_ref[2 * (step * tm + r) + k], 1)],
                                     ybuf.at[s, k, pl.ds(r, 1)], sem.at[s])

    def wait_gather(s):
        pltpu.make_async_copy(ybuf.at[s], ybuf.at[s], sem.at[s]).wait()

    @pl.when(i == 0)
    def _():
        def body(r, carry):
            row_copy(0, r, 0, 0).start()
            row_copy(0, r, 1, 0).start()
            return carry

        lax.fori_loop(0, tm, body, 0, unroll=4)

    wait_gather(slot)

    nxt = jnp.minimum(i + 1, last)
    for r in range(tm):
        row_copy(nxt, r, 0, 1 - slot).start()
        row_copy(nxt, r, 1, 1 - slot).start()

    b = (i * tm) // seq
    gt2 = mods_ref[5, pl.ds(b, 1), :]
    gate = gate_ref[...]
    moe = gate[:, 0:1] * ybuf[slot, 0] + gate[:, 1:2] * ybuf[slot, 1]
    x = x1_ref[...] + gt2 * moe
    o_ref[...] = x * lax.rsqrt(jnp.mean(x * x, axis=-1, keepdims=True) + EPS) * fg_ref[...]

    @pl.when(i == last)
    def _():
        wait_gather(1 - slot)


def _combine_call(dest_flat, y_pad, x1, gate, mods, final_g, seq):
    n, d = x1.shape
    tm = min(seq, 256)
    grid_spec = pltpu.PrefetchScalarGridSpec(
        num_scalar_prefetch=1,
        grid=(n // tm,),
        in_specs=[pl.BlockSpec(memory_space=pl.ANY),
                  pl.BlockSpec((tm, d), lambda i, dest: (i, 0)),
                  pl.BlockSpec((tm, LANES), lambda i, dest: (i, 0)),
                  pl.BlockSpec((6, 8, d), lambda i, dest: (0, 0, 0)),
                  pl.BlockSpec((1, d), lambda i, dest: (0, 0))],
        out_specs=pl.BlockSpec((tm, d), lambda i, dest: (i, 0)),
        scratch_shapes=[pltpu.VMEM((2, 2, tm, d), F32), pltpu.SemaphoreType.DMA((2,))],
    )
    return pl.pallas_call(
        functools.partial(_combine_kernel, seq=seq),
        grid_spec=grid_spec,
        out_shape=jax.ShapeDtypeStruct((n, d), F32),
        compiler_params=_cparams(("arbitrary",)),
        name="combine",
    )(dest_flat, y_pad, x1, gate, mods, final_g)


def _pad_lanes(a, lane0):
    return jnp.zeros((LANES,), F32).at[lane0:lane0 + a.shape[0]].set(a.astype(F32))


def kernel(x, c, ctx, c_ctx, w_mod, b_mod, norm1_g, w_in, pool_w, pool_scale, conv_w,
           a_log_f, dt_bias_f, a_log_b, dt_bias_b, out_norm_g, w_out, norm2_g,
           w_grp, b_grp, w_rt, b_rt, w1, w3, w2, final_g):
    bsz, t, d = x.shape
    depth = w_mod.shape[0]
    assert depth == 1, "single-layer problem: the context stream is read but never updated"
    heads = a_log_f.shape[1]
    pool_width = pool_w.shape[1] * pool_w.shape[2]
    dn_width = heads * HEAD_DIM
    q0 = pool_width
    z0 = q0 + 3 * dn_width
    ab0 = z0 + dn_width
    n_tok = bsz * t
    l = 0

    c8 = jnp.zeros((8, d), F32).at[:bsz].set(c).at[bsz].set(c_ctx)
    mod = _mod_call(c8, w_mod[l], b_mod[l])
    mods = mod.reshape(8, 6, d).transpose(1, 0, 2)

    w_in_bf = w_in[l].astype(BF16)
    wab_bf = jnp.zeros((d, LANES), BF16).at[:, :4 * heads].set(w_in_bf[:, ab0:])
    g1 = norm1_g[l].reshape(1, d)
    proj, ab = _inproj_call(x, mods, g1, w_in_bf, wab_bf, ab0, None)
    proj_c, ab_c = _inproj_call(ctx, mods, g1, w_in_bf, wab_bf, ab0, bsz)

    prm = jnp.zeros((8, LANES), F32)
    prm = prm.at[0].set(_pad_lanes(jnp.concatenate([a_log_f[l], a_log_b[l]]), 2 * heads))
    prm = prm.at[1].set(_pad_lanes(jnp.concatenate([dt_bias_f[l], dt_bias_b[l]]), 2 * heads))
    g, g_t = _gates_call(ab, prm, heads)
    gc, gc_t = _gates_call(ab_c, prm, heads)
    gr = g_t.reshape(bsz, LANES, t // CHUNK, CHUNK)
    grc = gc_t.reshape(bsz, LANES, ctx.shape[1] // CHUNK, CHUNK)

    dn = _delta_call(proj, proj_c, conv_w[l], g, gc, gr, grc, out_norm_g[l].reshape(1, HEAD_DIM),
                     heads, q0 // HEAD_DIM, z0 // HEAD_DIM)
    pool = _pool_call(proj, pool_w[l], pool_scale[l].reshape(1, pool_width), t // GRID_W, GRID_W)

    wr = jnp.zeros((d, LANES), F32).at[:, :N_GROUPS].set(w_grp[l]).at[:, EXP_LANE0:EXP_LANE0 + N_EXPERTS].set(w_rt[l])
    br = jnp.zeros((LANES,), F32).at[:N_GROUPS].set(b_grp[l]).at[EXP_LANE0:EXP_LANE0 + N_EXPERTS].set(b_rt[l])
    x1, h2, logits = _outproj_call(pool, dn, w_out[l].astype(BF16), x, mods, norm2_g[l].reshape(1, d),
                                   wr, br.reshape(1, LANES))

    n_blocks = (n_tok * 2 + N_EXPERTS * (MOE_BLOCK - 1) + MOE_BLOCK - 1) // MOE_BLOCK
    dest, gate, be = _router_call(logits.reshape(n_tok, LANES), n_blocks)
    dest_flat = dest[:, :2].reshape(-1)
    block_expert = be[:n_blocks, 0]
    used = be[0:1, 1]
    inv = _slots_call(dest_flat, n_blocks * MOE_BLOCK)
    y_pad = _gmm_call(block_expert, used, inv, h2.reshape(n_tok, d), w1[l], w3[l], w2[l])
    out = _combine_call(dest_flat, y_pad, x1.reshape(n_tok, d), gate, mods, final_g.reshape(1, d), t)
    return out.reshape(bsz, t, d)
```

```python
import functools

import jax
import jax.numpy as jnp
from jax import lax
from jax.experimental import pallas as pl
from jax.experimental.pallas import tpu as pltpu

F32 = jnp.float32
BF16 = jnp.bfloat16

GRID_W = 64
POOL_WINDOWS = (2, 4, 8, 16)
HEAD_DIM = 128
CONV_WIDTH = 5
CHUNK = 64
N_GROUPS = 4
EXPERTS_PER_GROUP = 8
N_EXPERTS = N_GROUPS * EXPERTS_PER_GROUP
MOE_BLOCK = 128
EPS = 1e-6
LANES = 128
PRE_UNROLL = 8
EXP_LANE0 = N_GROUPS

VMEM_LIMIT = 56 * 1024 * 1024


def _cparams(sem):
    return pltpu.CompilerParams(dimension_semantics=sem, vmem_limit_bytes=VMEM_LIMIT)


def _dot(a, b):
    return jnp.dot(a.astype(BF16), b.astype(BF16), preferred_element_type=F32)


def _dot_split(a, b):
    a_hi = a.astype(BF16)
    b_hi = b.astype(BF16)
    a_lo = (a - a_hi.astype(F32)).astype(BF16)
    b_lo = (b - b_hi.astype(F32)).astype(BF16)
    return (jnp.dot(a_hi, b_hi, preferred_element_type=F32)
            + jnp.dot(a_lo, b_hi, preferred_element_type=F32)
            + jnp.dot(a_hi, b_lo, preferred_element_type=F32))


def _silu(x):
    return x * jax.nn.sigmoid(x)


def _mod_kernel(c_ref, w_ref, b_ref, o_ref):
    o_ref[...] = _dot(_silu(c_ref[...]), w_ref[...]) + b_ref[...]


def _mod_call(c8, w_mod, b_mod):
    d, n = w_mod.shape
    tn = 512
    return pl.pallas_call(
        _mod_kernel,
        grid=(n // tn,),
        in_specs=[pl.BlockSpec((8, d), lambda j: (0, 0)),
                  pl.BlockSpec((d, tn), lambda j: (0, j)),
                  pl.BlockSpec((1, tn), lambda j: (0, j))],
        out_specs=pl.BlockSpec((8, tn), lambda j: (0, j)),
        out_shape=jax.ShapeDtypeStruct((8, n), F32),
        compiler_params=_cparams(("parallel",)),
        name="mod",
    )(c8, w_mod, b_mod.reshape(1, n))


def _inproj_kernel(x_ref, mods_ref, g_ref, w_ref, wab_ref, o_ref, ab_ref, hn_ref, *, mod_row):
    b = pl.program_id(0)
    j = pl.program_id(2)

    @pl.when(j == 0)
    def _():
        x = x_ref[0]
        y = x * lax.rsqrt(jnp.mean(x * x, axis=-1, keepdims=True) + EPS) * g_ref[...]
        row = b if mod_row is None else mod_row
        sh = mods_ref[0, pl.ds(row, 1), :]
        sc = mods_ref[1, pl.ds(row, 1), :]
        hb = (y * (1 + sc) + sh).astype(BF16)
        hn_ref[...] = hb
        ab_ref[0] = jnp.dot(hb, wab_ref[...], preferred_element_type=F32)

    o_ref[0] = jnp.dot(hn_ref[...], w_ref[...], preferred_element_type=F32)


def _inproj_call(x, mods, g, w_in, wab_bf, n_main, mod_row):
    bsz, t, d = x.shape
    tm = min(t, 1024)
    tn = 1024
    kern = functools.partial(_inproj_kernel, mod_row=mod_row)
    return pl.pallas_call(
        kern,
        grid=(bsz, t // tm, n_main // tn),
        in_specs=[pl.BlockSpec((1, tm, d), lambda b, i, j: (b, i, 0)),
                  pl.BlockSpec((2, 8, d), lambda b, i, j: (0, 0, 0)),
                  pl.BlockSpec((1, d), lambda b, i, j: (0, 0)),
                  pl.BlockSpec((d, tn), lambda b, i, j: (0, j)),
                  pl.BlockSpec((d, LANES), lambda b, i, j: (0, 0))],
        out_specs=[pl.BlockSpec((1, tm, tn), lambda b, i, j: (b, i, j)),
                   pl.BlockSpec((1, tm, LANES), lambda b, i, j: (b, i, 0))],
        out_shape=[jax.ShapeDtypeStruct((bsz, t, n_main), F32),
                   jax.ShapeDtypeStruct((bsz, t, LANES), F32)],
        scratch_shapes=[pltpu.VMEM((tm, d), BF16)],
        compiler_params=_cparams(("parallel", "parallel", "arbitrary")),
        name="inproj",
    )(x, mods, g, w_in, wab_bf)


def _gates_kernel(ab_ref, prm_ref, g_ref, gt_ref, *, heads):
    ab = ab_ref[0]
    t = ab.shape[0]
    h2, h3, h4, h6 = 2 * heads, 3 * heads, 4 * heads, 6 * heads
    beta = jax.nn.sigmoid(ab)
    xx = ab + prm_ref[1:2, :]
    softplus = jnp.maximum(xx, 0.0) + jnp.log1p(jnp.exp(-jnp.abs(xx)))
    g = -jnp.exp(prm_ref[0:1, :]) * softplus
    pos = lax.broadcasted_iota(jnp.int32, ab.shape, 0) & (CHUNK - 1)
    cs = g
    ss = g
    s = 1
    while s < CHUNK:
        cs = cs + jnp.where(pos >= s, pltpu.roll(cs, s, 0), 0.0)
        ss = ss + jnp.where(pos < CHUNK - s, pltpu.roll(ss, t - s, 0), 0.0)
        s *= 2
    tot = pltpu.roll(cs + ss - g, h2, 1)
    lane = lax.broadcasted_iota(jnp.int32, ab.shape, 1)
    out = jnp.where(lane < h2, beta,
                    jnp.where(lane < h3, cs,
                              jnp.where(lane < h4, ss,
                                        jnp.where(lane < h6, tot, 0.0))))
    g_ref[0] = out
    gt_ref[0] = out.T


def _gates_call(ab, prm, heads):
    bsz, t, _ = ab.shape
    return pl.pallas_call(
        functools.partial(_gates_kernel, heads=heads),
        grid=(bsz,),
        in_specs=[pl.BlockSpec((1, t, LANES), lambda b: (b, 0, 0)),
                  pl.BlockSpec((8, LANES), lambda b: (0, 0))],
        out_specs=[pl.BlockSpec((1, t, LANES), lambda b: (b, 0, 0)),
                   pl.BlockSpec((1, LANES, t), lambda b: (b, 0, 0))],
        out_shape=[jax.ShapeDtypeStruct((bsz, t, LANES), F32),
                   jax.ShapeDtypeStruct((bsz, LANES, t), F32)],
        compiler_params=_cparams(("parallel",)),
        name="gates",
    )(ab, prm)


def _conv_silu(x, w):
    n = x.shape[0]
    row = lax.broadcasted_iota(jnp.int32, x.shape, 0)
    acc = x * w[CONV_WIDTH // 2:CONV_WIDTH // 2 + 1, :]
    for j in range(CONV_WIDTH):
        d = j - CONV_WIDTH // 2
        if d == 0:
            continue
        xs = pltpu.roll(x, (-d) % n, 0)
        valid = (row + d >= 0) & (row + d < n)
        acc = acc + jnp.where(valid, xs, 0.0) * w[j:j + 1, :]
    return _silu(acc)


def _l2norm(a):
    return a * lax.rsqrt(jnp.sum(a * a, axis=-1, keepdims=True) + EPS)


def _lane_col(g, lane_idx):
    lane = lax.broadcasted_iota(jnp.int32, g.shape, 1)
    return jnp.sum(jnp.where(lane == lane_idx, g, 0.0), axis=-1, keepdims=True)


def _chunk_terms(chains, between=()):
    pending = list(between)

    def stage_done():
        if pending:
            pending.pop(0)()

    c, hd = chains[0][0].shape
    ri = lax.broadcasted_iota(jnp.int32, (c, c), 0)
    ci = lax.broadcasted_iota(jnp.int32, (c, c), 1)
    eye = jnp.where(ri == ci, 1.0, 0.0)
    right = lax.broadcasted_iota(jnp.int32, (c, 2 * c), 1) >= c
    nt = (((1,), (1,)), ((), ()))
    tn = (((0,), (0,)), ((), ()))

    decs, kn_bs, zs = [], [], []
    for kn_c, kb_c, _, _, _, _, gcc, gcr, upper in chains:
        incl = (ri <= ci) if upper else (ri >= ci)
        strict = (ri < ci) if upper else (ri > ci)
        dec = jnp.where(incl, jnp.exp(jnp.where(incl, gcc - gcr, 0.0)), 0.0)
        kn_b = kn_c.astype(BF16)
        kk = lax.dot_general(kb_c.astype(BF16), kn_b, nt, preferred_element_type=F32)
        decs.append(dec)
        kn_bs.append(kn_b)
        zs.append(jnp.concatenate([-jnp.where(strict, kk * dec, 0.0), eye], axis=1))
    stage_done()
    n = 1
    while n < c:
        zs = [_dot(z[:, :c], z) + jnp.where(right, z, 0.0) for z in zs]
        stage_done()
        n *= 2
    uw_bs = [_dot(z[:, c:], ch[3]).astype(BF16) for z, ch in zip(zs, chains)]
    stage_done()
    wns = [lax.dot_general(ch[4].astype(BF16), uw_b, tn, preferred_element_type=F32)
           for uw_b, ch in zip(uw_bs, chains)]
    stage_done()
    qks = [None if ch[2] is None else
           lax.dot_general(ch[2].astype(BF16), kn_b, nt, preferred_element_type=F32) * dec
           for ch, kn_b, dec in zip(chains, kn_bs, decs)]
    stage_done()
    qws = [None if qk is None else jnp.dot(qk.astype(BF16), uw_b, preferred_element_type=F32)
           for qk, uw_b in zip(qks, uw_bs)]
    while pending:
        stage_done()
    out = []
    for ch, wn, qw in zip(chains, wns, qws):
        nc, w2 = wn[:, :hd], wn[:, hd:]
        if qw is None:
            out.append((w2, nc, None, None))
        else:
            out.append((w2, nc, ch[5] - qw[:, hd:], qw[:, :hd]))
    return out


def _delta_kernel(q_ref, k_ref, v_ref, z_ref, kc_ref, vc_ref, cwq_ref, cwk_ref, cwv_ref,
                  g_ref, gc_ref, grf_ref, grb_ref, grcf_ref, grcb_ref, ong_ref,
                  out_ref,
                  qn_s, kn_s, kb_s, rhs_s, kd_s, qd_s, col_s,
                  knc_s, kbc_s, rhsc_s, kdc_s, colc_s,
                  w2_s, nc_s, qp_s, o0_s, w2c_s, ncc_s, o_s, *, heads):
    h = pl.program_id(1)
    t = q_ref.shape[1]
    tc = kc_ref.shape[1]
    hd = HEAD_DIM
    n_lat = t // CHUNK
    n_ctx = tc // CHUNK

    qn = _l2norm(_conv_silu(q_ref[0], cwq_ref[...])) * (hd ** -0.5)
    kn = _l2norm(_conv_silu(k_ref[0], cwk_ref[...]))
    vv = _conv_silu(v_ref[0], cwv_ref[...])
    knc = _l2norm(_conv_silu(kc_ref[0], cwk_ref[...]))
    vvc = _conv_silu(vc_ref[0], cwv_ref[...])
    qn_s[...] = qn
    kn_s[...] = kn
    knc_s[...] = knc
    g_lat = g_ref[0]
    g_ctx = gc_ref[0]

    for d in range(2):
        for (gt_, kn_, vv_, kb_r, rhs_r, kd_r, col_r, qn_, qd_r) in (
                (g_lat, kn, vv, kb_s, rhs_s, kd_s, col_s, qn, qd_s),
                (g_ctx, knc, vvc, kbc_s, rhsc_s, kdc_s, colc_s, None, None)):
            beta = _lane_col(gt_, d * heads + h)
            gcum = _lane_col(gt_, (2 + d) * heads + h)
            gtot = _lane_col(gt_, (4 + d) * heads + h)
            e = jnp.exp(gcum)
            kb = kn_ * beta
            kb_r[d] = kb
            rhs_r[d, :, 0:hd] = vv_ * beta
            rhs_r[d, :, hd:2 * hd] = kb * e
            kd_r[d] = kn_ * jnp.exp(gtot - gcum)
            lane = lax.broadcasted_iota(jnp.int32, (gcum.shape[0], LANES), 1)
            col_r[d] = jnp.where(lane == 0, gcum, jnp.where(lane == 1, jnp.exp(gtot), 0.0))
            if qn_ is not None:
                qd_r[d] = qn_ * e

    gr_lat = (grf_ref, grb_ref)
    gr_ctx = (grcf_ref, grcb_ref)

    un_ctx = min(PRE_UNROLL, n_ctx)
    un_lat = min(PRE_UNROLL, n_lat)

    def pre_ctx(i, carry):
        ids = [(i * un_ctx + u, d) for u in range(un_ctx) for d in range(2)]
        chains = []
        for c, d in ids:
            rows = pl.ds(pl.multiple_of(c * CHUNK, CHUNK), CHUNK)
            chains.append((knc_s[rows, :], kbc_s[d, rows, :], None, rhsc_s[d, rows, :], kdc_s[d, rows, :],
                           None, colc_s[d, rows, 0:1], gr_ctx[d][0, 0, pl.ds(c, 1), :], d == 1))
        for (c, d), (w2, nc, _, _) in zip(ids, _chunk_terms(chains)):
            m0 = pl.multiple_of(c * hd, hd)
            w2c_s[d, pl.ds(m0, hd), :] = w2
            ncc_s[d, pl.ds(m0, hd), :] = nc
        return carry

    lax.fori_loop(0, n_ctx // un_ctx, pre_ctx, 0)

    def lat_chunk(i, u, d):
        k = i * un_lat + u
        return k if d == 0 else n_lat - 1 - k

    def pre_lat(i, between=()):
        ids = [(lat_chunk(i, u, d), d) for u in range(un_lat) for d in range(2)]
        chains = []
        for c, d in ids:
            rows = pl.ds(pl.multiple_of(c * CHUNK, CHUNK), CHUNK)
            chains.append((kn_s[rows, :], kb_s[d, rows, :], qn_s[rows, :], rhs_s[d, rows, :], kd_s[d, rows, :],
                           qd_s[d, rows, :], col_s[d, rows, 0:1], gr_lat[d][0, 0, pl.ds(c, 1), :], d == 1))
        for (c, d), (w2, nc, qp, o0) in zip(ids, _chunk_terms(chains, between)):
            rows = pl.ds(pl.multiple_of(c * CHUNK, CHUNK), CHUNK)
            m0 = pl.multiple_of(c * hd, hd)
            w2_s[d, pl.ds(m0, hd), :] = w2
            nc_s[d, pl.ds(m0, hd), :] = nc
            qp_s[d, rows, :] = qp
            o0_s[d, rows, :] = o0

    def scan_ctx_steps(box):
        def step(i):
            def run():
                for d in range(2):
                    c = i if d == 0 else n_ctx - 1 - i
                    gt = colc_s[d, c * CHUNK:c * CHUNK + 1, 1:2]
                    box[d] = (gt * box[d] + ncc_s[d, c * hd:(c + 1) * hd, :]
                              - _dot(w2c_s[d, c * hd:(c + 1) * hd, :], box[d]))
            return run

        return [step(i) for i in range(n_ctx)]

    def scan_lat_steps(i, box):
        def step(u):
            def run():
                for d in range(2):
                    s = box[d]
                    c = lat_chunk(i, u, d)
                    r0 = pl.multiple_of(c * CHUNK, CHUNK)
                    rows = pl.ds(r0, CHUNK)
                    m0 = pl.multiple_of(c * hd, hd)
                    s_b = s.astype(BF16)
                    o_s[d, rows, :] = jnp.dot(qp_s[d, rows, :].astype(BF16), s_b,
                                              preferred_element_type=F32) + o0_s[d, rows, :]
                    gt = col_s[d, pl.ds(r0, 1), 1:2]
                    box[d] = gt * s + nc_s[d, pl.ds(m0, hd), :] - jnp.dot(
                        w2_s[d, pl.ds(m0, hd), :].astype(BF16), s_b, preferred_element_type=F32)
            return run

        return [step(u) for u in range(un_lat)]

    zero = jnp.zeros((hd, hd), F32)
    box = [zero, zero]
    pre_lat(0, scan_ctx_steps(box))
    states = tuple(box)

    def lat_body(i, states):
        box = list(states)
        pre_lat(i, scan_lat_steps(i - 1, box))
        return tuple(box)

    states = lax.fori_loop(1, n_lat // un_lat, lat_body, states)
    box = list(states)
    for run in scan_lat_steps(n_lat // un_lat - 1, box):
        run()

    o = o_s[0] + o_s[1]
    o = o * lax.rsqrt(jnp.mean(o * o, axis=-1, keepdims=True) + EPS) * ong_ref[...]
    out_ref[0] = (o * _silu(z_ref[0])).astype(out_ref.dtype)


def _delta_call(proj, proj_c, conv_w, g, gc, gr, grc, ong, heads, q_blk0, z_blk0):
    bsz, t, _ = proj.shape
    tc = proj_c.shape[1]
    hd = HEAD_DIM
    n_lat, n_ctx = t // CHUNK, tc // CHUNK

    def col(off):
        return lambda b, h: (b, 0, off + h)

    def cw(off):
        return lambda b, h: (0, off + h)

    def grow(off):
        return lambda b, h: (b, off + h, 0, 0)

    in_specs = [
        pl.BlockSpec((1, t, hd), col(q_blk0)),
        pl.BlockSpec((1, t, hd), col(q_blk0 + heads)),
        pl.BlockSpec((1, t, hd), col(q_blk0 + 2 * heads)),
        pl.BlockSpec((1, t, hd), col(z_blk0)),
        pl.BlockSpec((1, tc, hd), col(q_blk0 + heads)),
        pl.BlockSpec((1, tc, hd), col(q_blk0 + 2 * heads)),
        pl.BlockSpec((CONV_WIDTH, hd), cw(0)),
        pl.BlockSpec((CONV_WIDTH, hd), cw(heads)),
        pl.BlockSpec((CONV_WIDTH, hd), cw(2 * heads)),
        pl.BlockSpec((1, t, LANES), lambda b, h: (b, 0, 0)),
        pl.BlockSpec((1, tc, LANES), lambda b, h: (b, 0, 0)),
        pl.BlockSpec((1, 1, n_lat, CHUNK), grow(2 * heads)),
        pl.BlockSpec((1, 1, n_lat, CHUNK), grow(3 * heads)),
        pl.BlockSpec((1, 1, n_ctx, CHUNK), grow(2 * heads)),
        pl.BlockSpec((1, 1, n_ctx, CHUNK), grow(3 * heads)),
        pl.BlockSpec((1, hd), lambda b, h: (0, 0)),
    ]
    scratch = [
        pltpu.VMEM((t, hd), F32), pltpu.VMEM((t, hd), F32),
        pltpu.VMEM((2, t, hd), F32), pltpu.VMEM((2, t, 2 * hd), F32),
        pltpu.VMEM((2, t, hd), F32), pltpu.VMEM((2, t, hd), F32),
        pltpu.VMEM((2, t, LANES), F32),
        pltpu.VMEM((tc, hd), F32), pltpu.VMEM((2, tc, hd), F32),
        pltpu.VMEM((2, tc, 2 * hd), F32), pltpu.VMEM((2, tc, hd), F32),
        pltpu.VMEM((2, tc, LANES), F32),
        pltpu.VMEM((2, n_lat * hd, hd), F32), pltpu.VMEM((2, n_lat * hd, hd), F32),
        pltpu.VMEM((2, t, hd), F32), pltpu.VMEM((2, t, hd), F32),
        pltpu.VMEM((2, n_ctx * hd, hd), F32), pltpu.VMEM((2, n_ctx * hd, hd), F32),
        pltpu.VMEM((2, t, hd), F32),
    ]
    return pl.pallas_call(
        functools.partial(_delta_kernel, heads=heads),
        grid=(bsz, heads),
        in_specs=in_specs,
        out_specs=pl.BlockSpec((1, t, hd), lambda b, h: (b, 0, h)),
        out_shape=jax.ShapeDtypeStruct((bsz, t, heads * hd), BF16),
        scratch_shapes=scratch,
        compiler_params=_cparams(("parallel", "parallel")),
        name="delta",
    )(proj, proj, proj, proj, proj_c, proj_c, conv_w, conv_w, conv_w, g, gc, gr, gr, grc, grc, ong)


def _shift_rows(x, d, idx, size, stride):
    n = x.shape[0]
    xs = pltpu.roll(x, (-d * stride) % n, 0)
    return jnp.where((idx + d >= 0) & (idx + d < size), xs, 0.0)


def _box_sum_1d(x, win, idx, size, stride):
    m = win // 2
    lead = x
    trail = x
    k = 1
    while k < m:
        lead = lead + _shift_rows(lead, k, idx, size, stride)
        trail = trail + _shift_rows(trail, -k, idx, size, stride)
        k *= 2
    return lead + _shift_rows(trail, -1, idx, size, stride)


def _pool_kernel(u_ref, pw_ref, ps_ref, o_ref, *, rows, cols):
    t = u_ref.shape[1]
    gc = pw_ref.shape[1]
    tok = lax.broadcasted_iota(jnp.int32, (t, gc), 0)
    ci = tok % cols
    ri = tok // cols
    for gi, win in enumerate(POOL_WINDOWS):
        lo = win // 2
        hi = win - lo
        u = u_ref[0, :, gi * gc:(gi + 1) * gc]
        s = _box_sum_1d(u, win, ci, cols, 1)
        s = _box_sum_1d(s, win, ri, rows, cols)
        cnt_c = jnp.minimum(ci + hi, cols) - jnp.maximum(ci - lo, 0)
        cnt_r = jnp.minimum(ri + hi, rows) - jnp.maximum(ri - lo, 0)
        mean = s / (cnt_c * cnt_r).astype(F32)
        y = _dot(mean - u, pw_ref[gi]) * ps_ref[:, gi * gc:(gi + 1) * gc]
        o_ref[0, :, gi * gc:(gi + 1) * gc] = y.astype(o_ref.dtype)


def _pool_call(proj, pool_w, pool_scale, rows, cols):
    bsz, t, _ = proj.shape
    ng, gc, _ = pool_w.shape
    pwid = ng * gc
    return pl.pallas_call(
        functools.partial(_pool_kernel, rows=rows, cols=cols),
        grid=(bsz,),
        in_specs=[pl.BlockSpec((1, t, pwid), lambda b: (b, 0, 0)),
                  pl.BlockSpec((ng, gc, gc), lambda b: (0, 0, 0)),
                  pl.BlockSpec((1, pwid), lambda b: (0, 0))],
        out_specs=pl.BlockSpec((1, t, pwid), lambda b: (b, 0, 0)),
        out_shape=jax.ShapeDtypeStruct((bsz, t, pwid), BF16),
        compiler_params=_cparams(("parallel",)),
        name="pool",
    )(proj, pool_w, pool_scale)


def _outproj_kernel(pool_ref, dn_ref, wa_ref, wb_ref, x_ref, mods_ref, g2_ref, wr_ref, br_ref,
                    x1_ref, h2_ref, lg_ref):
    b = pl.program_id(0)
    mix = (jnp.dot(pool_ref[0], wa_ref[...], preferred_element_type=F32)
           + jnp.dot(dn_ref[0], wb_ref[...], preferred_element_type=F32))
    gt1 = mods_ref[2, pl.ds(b, 1), :]
    sh2 = mods_ref[3, pl.ds(b, 1), :]
    sc2 = mods_ref[4, pl.ds(b, 1), :]
    x1 = x_ref[0] + gt1 * mix
    x1_ref[0] = x1
    y = x1 * lax.rsqrt(jnp.mean(x1 * x1, axis=-1, keepdims=True) + EPS) * g2_ref[...]
    h2 = y * (1 + sc2) + sh2
    h2_ref[0] = h2
    lg_ref[0] = _dot_split(h2, wr_ref[...]) + br_ref[...]


def _outproj_call(pool, dn, w_out_bf, x, mods, g2, wr, br):
    bsz, t, d = x.shape
    half = pool.shape[-1]
    tm = min(t, 512)
    return pl.pallas_call(
        _outproj_kernel,
        grid=(bsz, t // tm),
        in_specs=[pl.BlockSpec((1, tm, half), lambda b, i: (b, i, 0)),
                  pl.BlockSpec((1, tm, half), lambda b, i: (b, i, 0)),
                  pl.BlockSpec((half, d), lambda b, i: (0, 0)),
                  pl.BlockSpec((half, d), lambda b, i: (1, 0)),
                  pl.BlockSpec((1, tm, d), lambda b, i: (b, i, 0)),
                  pl.BlockSpec((6, 8, d), lambda b, i: (0, 0, 0)),
                  pl.BlockSpec((1, d), lambda b, i: (0, 0)),
                  pl.BlockSpec((d, LANES), lambda b, i: (0, 0)),
                  pl.BlockSpec((1, LANES), lambda b, i: (0, 0))],
        out_specs=[pl.BlockSpec((1, tm, d), lambda b, i: (b, i, 0)),
                   pl.BlockSpec((1, tm, d), lambda b, i: (b, i, 0)),
                   pl.BlockSpec((1, tm, LANES), lambda b, i: (b, i, 0))],
        out_shape=[jax.ShapeDtypeStruct((bsz, t, d), F32),
                   jax.ShapeDtypeStruct((bsz, t, d), F32),
                   jax.ShapeDtypeStruct((bsz, t, LANES), F32)],
        compiler_params=_cparams(("parallel", "parallel")),
        name="outproj",
    )(pool, dn, w_out_bf, w_out_bf, x, mods, g2, wr, br)


def _router_kernel(lg_ref, dest_ref, gate_ref, be_ref, cnt_s, run_s, off_s):
    p = pl.program_id(0)
    i = pl.program_id(1)
    lg = lg_ref[...]
    tm = lg.shape[0]
    lane = lax.broadcasted_iota(jnp.int32, lg.shape, 1)
    lane_f = lane.astype(F32)
    neg = -jnp.inf
    big = float(LANES)

    grp = jnp.where(lane < N_GROUPS, lg, neg)
    gmax = jnp.max(grp, axis=-1, keepdims=True)
    gidx = jnp.min(jnp.where(grp == gmax, lane_f, big), axis=-1, keepdims=True)
    p_grp = 1.0 / jnp.sum(jnp.where(lane < N_GROUPS, jnp.exp(lg - gmax), 0.0), axis=-1, keepdims=True)
    lo = EXP_LANE0 + EXPERTS_PER_GROUP * gidx
    ev = jnp.where((lane_f >= lo) & (lane_f < lo + EXPERTS_PER_GROUP), lg, neg)
    t1 = jnp.max(ev, axis=-1, keepdims=True)
    i1 = jnp.min(jnp.where(ev == t1, lane_f, big), axis=-1, keepdims=True)
    ev2 = jnp.where(lane_f == i1, neg, ev)
    t2 = jnp.max(ev2, axis=-1, keepdims=True)
    i2 = jnp.min(jnp.where(ev2 == t2, lane_f, big), axis=-1, keepdims=True)
    oh1 = lane_f == i1
    oh2 = lane_f == i2
    cnt = oh1.astype(F32) + oh2.astype(F32)
    colsum = jnp.sum(cnt, axis=0, keepdims=True)

    @pl.when(p == 0)
    def _():
        @pl.when(i == 0)
        def _():
            cnt_s[...] = jnp.zeros_like(cnt_s)

        cnt_s[...] += colsum

    @pl.when(p == 1)
    def _():
        @pl.when(i == 0)
        def _():
            nblk = jnp.floor((cnt_s[...] + (MOE_BLOCK - 1)) * (1.0 / MOE_BLOCK))
            r = lax.broadcasted_iota(jnp.int32, (LANES, LANES), 0)
            c = lax.broadcasted_iota(jnp.int32, (LANES, LANES), 1)
            tri = (r < c).astype(BF16)
            nb8 = jnp.broadcast_to(nblk, (8, LANES))
            start_blk = jnp.dot(nb8.astype(BF16), tri, preferred_element_type=F32)[0:1, :]
            off_s[...] = start_blk * MOE_BLOCK
            run_s[...] = jnp.zeros_like(run_s)
            end_blk = start_blk + nblk
            nb = be_ref.shape[0]
            blk = lax.broadcasted_iota(jnp.int32, (nb, LANES), 0).astype(F32)
            ln = lax.broadcasted_iota(jnp.int32, (nb, LANES), 1)
            is_exp = (ln >= EXP_LANE0) & (ln < EXP_LANE0 + N_EXPERTS)
            done = jnp.sum(jnp.where(is_exp & (end_blk <= blk), 1.0, 0.0), axis=-1, keepdims=True)
            bexp = jnp.minimum(done, N_EXPERTS - 1.0)
            used = jnp.max(jnp.where(is_exp, end_blk, 0.0), axis=-1, keepdims=True)
            be_ref[...] = jnp.where(ln == 0, bexp, jnp.where(ln == 1, used, 0.0)).astype(jnp.int32)

        rr = lax.broadcasted_iota(jnp.int32, (tm, tm), 0)
        cc = lax.broadcasted_iota(jnp.int32, (tm, tm), 1)
        before = (cc < rr).astype(BF16)
        prefix = jnp.dot(before, cnt.astype(BF16), preferred_element_type=F32)
        base = off_s[...] + run_s[...] + prefix
        d1 = jnp.sum(jnp.where(oh1, base, 0.0), axis=-1, keepdims=True)
        d2 = jnp.sum(jnp.where(oh2, base, 0.0), axis=-1, keepdims=True)
        run_s[...] += colsum
        dd = jnp.exp(t2 - t1)
        g1 = p_grp / (1.0 + dd)
        g2 = p_grp * dd / (1.0 + dd)
        dest_ref[...] = jnp.where(lane == 0, d1, jnp.where(lane == 1, d2, 0.0)).astype(jnp.int32)
        gate_ref[...] = jnp.where(lane == 0, g1, jnp.where(lane == 1, g2, 0.0))


def _router_call(logits, n_blocks):
    n = logits.shape[0]
    tm = min(n, 512)
    nb_pad = ((n_blocks + 7) // 8) * 8
    return pl.pallas_call(
        _router_kernel,
        grid=(2, n // tm),
        in_specs=[pl.BlockSpec((tm, LANES), lambda p, i: (i, 0))],
        out_specs=[pl.BlockSpec((tm, LANES), lambda p, i: (i * p, 0)),
                   pl.BlockSpec((tm, LANES), lambda p, i: (i * p, 0)),
                   pl.BlockSpec((nb_pad, LANES), lambda p, i: (0, 0))],
        out_shape=[jax.ShapeDtypeStruct((n, LANES), jnp.int32),
                   jax.ShapeDtypeStruct((n, LANES), F32),
                   jax.ShapeDtypeStruct((nb_pad, LANES), jnp.int32)],
        scratch_shapes=[pltpu.VMEM((1, LANES), F32), pltpu.VMEM((1, LANES), F32),
                        pltpu.VMEM((1, LANES), F32)],
        compiler_params=_cparams(("arbitrary", "arbitrary")),
        name="router",
    )(logits)


def _slots_kernel(dest_ref, zero_ref, inv_ref, sem):
    fill = pltpu.make_async_copy(zero_ref, inv_ref, sem)
    fill.start()
    fill.wait()

    def put(tok, carry):
        inv_ref[dest_ref[2 * tok]] = tok
        inv_ref[dest_ref[2 * tok + 1]] = tok
        return carry

    lax.fori_loop(0, dest_ref.shape[0] // 2, put, 0, unroll=8)


def _slots_call(dest_flat, n_rows):
    return pl.pallas_call(
        _slots_kernel,
        in_specs=[pl.BlockSpec(memory_space=pltpu.SMEM), pl.BlockSpec(memory_space=pl.ANY)],
        out_specs=pl.BlockSpec(memory_space=pltpu.SMEM),
        out_shape=jax.ShapeDtypeStruct((n_rows,), jnp.int32),
        scratch_shapes=[pltpu.SemaphoreType.DMA(())],
        name="slots",
    )(dest_flat, jnp.zeros((n_rows,), jnp.int32))


GMM_GROUP = 5
GMM_TILES = 4
WEIGHT_DMA_SPLIT = 4
WEIGHT_BUFFERS = 4


def _unit_tables(block_expert, used, n_blocks):
    n_units_max = N_EXPERTS + n_blocks // GMM_GROUP
    valid = jnp.arange(n_blocks) < used
    nblk_e = jnp.sum((block_expert[None, :] == jnp.arange(N_EXPERTS)[:, None]) & valid[None, :],
                     axis=1).astype(jnp.int32)
    first_e = jnp.cumsum(nblk_e) - nblk_e
    units_e = (nblk_e + GMM_GROUP - 1) // GMM_GROUP
    uend = jnp.cumsum(units_e)
    n_units = uend[-1]
    u = jnp.arange(n_units_max, dtype=jnp.int32)
    ue = jnp.minimum(jnp.sum(uend[None, :] <= u[:, None], axis=1), N_EXPERTS - 1).astype(jnp.int32)
    k = u - (uend - units_e)[ue]
    live = u < n_units
    ub = jnp.where(live, first_e[ue] + k * GMM_GROUP, 0).astype(jnp.int32)
    un = jnp.where(live, jnp.clip(nblk_e[ue] - k * GMM_GROUP, 0, GMM_GROUP), 0).astype(jnp.int32)
    return ue, ub, un, n_units.reshape(1).astype(jnp.int32)


def _gmm_kernel(ue_ref, ub_ref, un_ref, nu_ref, used_ref, inv_ref, h_ref, w1_ref, w3_ref, w2_ref, y_ref,
                xbuf, xs, acc, w1f, w3f, w2f, w1b_s, w3b_s, w2b_s, gsem, osem, wsem, *, n_blocks):
    u = pl.program_id(0)
    n_units = nu_ref[0]
    slot = u % 2
    d, tj = w1f.shape[1], w1f.shape[2]

    def gather(unit, s):
        base = ub_ref[unit] * MOE_BLOCK

        def body(g, carry):
            for k in range(8):
                r = g * 8 + k
                pltpu.make_async_copy(h_ref.at[pl.ds(inv_ref[base + r], 1)], xbuf.at[s, pl.ds(r, 1)],
                                      gsem.at[s]).start()
            return carry

        lax.fori_loop(0, un_ref[unit] * (MOE_BLOCK // 8), body, 0)

    def wait_gather(unit, s):
        rows = pl.ds(0, un_ref[unit] * MOE_BLOCK)
        pltpu.make_async_copy(xbuf.at[s, rows], xbuf.at[s, rows], gsem.at[s]).wait()

    def blk_rows(i):
        return pl.ds(pl.multiple_of(i * MOE_BLOCK, MOE_BLOCK), MOE_BLOCK)

    def out_copy(unit, s, i):
        dst = pl.ds(pl.multiple_of((ub_ref[unit] + i) * MOE_BLOCK, MOE_BLOCK), MOE_BLOCK)
        return pltpu.make_async_copy(acc.at[s, blk_rows(i)], y_ref.at[dst], osem.at[s])

    def wait_out(unit, s):
        def body(i, carry):
            out_copy(unit, s, i).wait()
            return carry

        lax.fori_loop(0, un_ref[unit], body, 0)

    def weight_copies(unit, j, ws):
        e = ue_ref[unit]
        cols = pl.ds(j * tj, tj)
        copies = []
        for p in range(WEIGHT_DMA_SPLIT):
            rk = pl.ds(p * (d // WEIGHT_DMA_SPLIT), d // WEIGHT_DMA_SPLIT)
            rj = pl.ds(p * (tj // WEIGHT_DMA_SPLIT), tj // WEIGHT_DMA_SPLIT)
            rj_src = pl.ds(j * tj + p * (tj // WEIGHT_DMA_SPLIT), tj // WEIGHT_DMA_SPLIT)
            copies += [pltpu.make_async_copy(w1_ref.at[e, rk, cols], w1f.at[ws, rk], wsem.at[ws]),
                       pltpu.make_async_copy(w3_ref.at[e, rk, cols], w3f.at[ws, rk], wsem.at[ws]),
                       pltpu.make_async_copy(w2_ref.at[e, rj_src, :], w2f.at[ws, rj], wsem.at[ws])]
        return copies

    n_wbuf = w1f.shape[0]
    ahead = n_wbuf - 1

    def tile_buffer(unit, j):
        return (unit * GMM_TILES + j) % n_wbuf

    def tile_after(unit, j, k):
        jj = j + k
        if jj < GMM_TILES:
            return unit, jj
        return jnp.minimum(unit + 1, n_units - 1), jj - GMM_TILES

    @pl.when(u == 0)
    def _():
        gather(0, 0)
        for k in range(ahead):
            for c in weight_copies(0, k, tile_buffer(0, k)):
                c.start()

    def tile_pass(j, ws):
        first = j == 0
        w1b = w1f[ws].astype(BF16)
        w3b = w3f[ws].astype(BF16)
        w2b = w2f[ws].astype(BF16)
        w1b_s[...] = w1b
        w3b_s[...] = w3b
        w2b_s[...] = w2b

        def rows_pass(r0, m, a1, a3, a2):
            rows = pl.ds(r0, m)
            xb = xs[rows, :]
            h1 = jnp.dot(xb, a1, preferred_element_type=F32)
            h3 = jnp.dot(xb, a3, preferred_element_type=F32)
            part = jnp.dot((_silu(h1) * h3).astype(BF16), a2, preferred_element_type=F32)
            if first:
                acc[slot, rows, :] = part
            else:
                acc[slot, rows, :] += part

        rows_pass(0, MOE_BLOCK, w1b, w3b, w2b)
        rest = un_ref[u] - 1

        def pair(i, carry):
            r0 = pl.multiple_of(MOE_BLOCK + i * 2 * MOE_BLOCK, MOE_BLOCK)
            rows_pass(r0, 2 * MOE_BLOCK, w1b_s[...], w3b_s[...], w2b_s[...])
            return carry

        lax.fori_loop(0, rest // 2, pair, 0)

        @pl.when(rest % 2 == 1)
        def _():
            rows_pass(pl.multiple_of(rest * MOE_BLOCK, MOE_BLOCK), MOE_BLOCK,
                      w1b_s[...], w3b_s[...], w2b_s[...])

    @pl.when(u < n_units)
    def _():
        @pl.when(u >= 2)
        def _():
            wait_out(u - 2, slot)

        @pl.when(u + 1 < n_units)
        def _():
            gather(u + 1, 1 - slot)

        wait_gather(u, slot)

        def to_bf16(i, carry):
            xs[blk_rows(i), :] = xbuf[slot, blk_rows(i), :].astype(BF16)
            return carry

        lax.fori_loop(0, un_ref[u], to_bf16, 0)
        for j in range(GMM_TILES):
            nu_, nj = tile_after(u, j, ahead)
            for c in weight_copies(nu_, nj, (tile_buffer(u, j) + ahead) % n_wbuf):
                c.start()
            ws = tile_buffer(u, j)
            for c in weight_copies(u, j, ws):
                c.wait()
            tile_pass(j, ws)

        def start(i, carry):
            out_copy(u, slot, i).start()
            return carry

        lax.fori_loop(0, un_ref[u], start, 0)

    @pl.when(u == pl.num_programs(0) - 1)
    def _():
        last = n_units - 1
        for k in range(ahead):
            for c in weight_copies(last, k, (tile_buffer(last, GMM_TILES - 1) + 1 + k) % n_wbuf):
                c.wait()
        wait_out(last, last % 2)

        @pl.when(n_units >= 2)
        def _():
            wait_out(last - 1, (last - 1) % 2)

        acc[0, blk_rows(0), :] = jnp.zeros((MOE_BLOCK, acc.shape[2]), F32)

        def zero_copy(b):
            dst = pl.ds(pl.multiple_of(b * MOE_BLOCK, MOE_BLOCK), MOE_BLOCK)
            return pltpu.make_async_copy(acc.at[0, blk_rows(0)], y_ref.at[dst], osem.at[0])

        def start(b, carry):
            zero_copy(b).start()
            return carry

        def wait(b, carry):
            zero_copy(b).wait()
            return carry

        lax.fori_loop(used_ref[0], n_blocks, start, 0)
        lax.fori_loop(used_ref[0], n_blocks, wait, 0)


def _gmm_call(block_expert, used, inv, h2, w1, w3, w2):
    n_rows = inv.shape[0]
    d = h2.shape[1]
    de = w1.shape[-1]
    n_blocks = n_rows // MOE_BLOCK
    tj = de // GMM_TILES
    ue, ub, un, nu = _unit_tables(block_expert, used[0], n_blocks)

    rows = GMM_GROUP * MOE_BLOCK
    grid_spec = pltpu.PrefetchScalarGridSpec(
        num_scalar_prefetch=6,
        grid=(ue.shape[0],),
        in_specs=[pl.BlockSpec(memory_space=pl.ANY)] * 4,
        out_specs=pl.BlockSpec(memory_space=pl.ANY),
        scratch_shapes=[pltpu.VMEM((2, rows, d), F32), pltpu.VMEM((rows, d), BF16),
                        pltpu.VMEM((2, rows, d), F32),
                        pltpu.VMEM((WEIGHT_BUFFERS, d, tj), F32), pltpu.VMEM((WEIGHT_BUFFERS, d, tj), F32),
                        pltpu.VMEM((WEIGHT_BUFFERS, tj, d), F32),
                        pltpu.VMEM((d, tj), BF16), pltpu.VMEM((d, tj), BF16), pltpu.VMEM((tj, d), BF16),
                        pltpu.SemaphoreType.DMA((2,)), pltpu.SemaphoreType.DMA((2,)),
                        pltpu.SemaphoreType.DMA((WEIGHT_BUFFERS,))],
    )
    return pl.pallas_call(
        functools.partial(_gmm_kernel, n_blocks=n_blocks),
        grid_spec=grid_spec,
        out_shape=jax.ShapeDtypeStruct((n_rows, d), F32),
        compiler_params=_cparams(("arbitrary",)),
        name="gmm",
    )(ue, ub, un, nu, used, inv, h2, w1, w3, w2)


def _combine_kernel(dest_ref, y_ref, x1_ref, gate_ref, mods_ref, fg_ref, o_ref, ybuf, sem, *, seq):
    i = pl.program_id(0)
    tm = x1_ref.shape[0]
    slot = i % 2

    last = pl.num_programs(0) - 1

    def row_copy(step, r, k, s):
        return pltpu.make_async_copy(y_ref.at[pl.ds(dest_ref[2 * (step * tm + r) + k], 1)],
                                     ybuf.at[s, k, pl.ds(r, 1)], sem.at[s])

    def wait_gather(s):
        pltpu.make_async_copy(ybuf.at[s], ybuf.at[s], sem.at[s]).wait()

    @pl.when(i == 0)
    def _():
        def body(r, carry):
            row_copy(0, r, 0, 0).start()
            row_copy(0, r, 1, 0).start()
            return carry

        lax.fori_loop(0, tm, body, 0, unroll=4)

    wait_gather(slot)

    nxt = jnp.minimum(i + 1, last)
    for r in range(tm):
        row_copy(nxt, r, 0, 1 - slot).start()
        row_copy(nxt, r, 1, 1 - slot).start()

    b = (i * tm) // seq
    gt2 = mods_ref[5, pl.ds(b, 1), :]
    gate = gate_ref[...]
    moe = gate[:, 0:1] * ybuf[slot, 0] + gate[:, 1:2] * ybuf[slot, 1]
    x = x1_ref[...] + gt2 * moe
    o_ref[...] = x * lax.rsqrt(jnp.mean(x * x, axis=-1, keepdims=True) + EPS) * fg_ref[...]

    @pl.when(i == last)
    def _():
        wait_gather(1 - slot)


def _combine_call(dest_flat, y_pad, x1, gate, mods, final_g, seq):
    n, d = x1.shape
    tm = min(seq, 256)
    grid_spec = pltpu.PrefetchScalarGridSpec(
        num_scalar_prefetch=1,
        grid=(n // tm,),
        in_specs=[pl.BlockSpec(memory_space=pl.ANY),
                  pl.BlockSpec((tm, d), lambda i, dest: (i, 0)),
                  pl.BlockSpec((tm, LANES), lambda i, dest: (i, 0)),
                  pl.BlockSpec((6, 8, d), lambda i, dest: (0, 0, 0)),
                  pl.BlockSpec((1, d), lambda i, dest: (0, 0))],
        out_specs=pl.BlockSpec((tm, d), lambda i, dest: (i, 0)),
        scratch_shapes=[pltpu.VMEM((2, 2, tm, d), F32), pltpu.SemaphoreType.DMA((2,))],
    )
    return pl.pallas_call(
        functools.partial(_combine_kernel, seq=seq),
        grid_spec=grid_spec,
        out_shape=jax.ShapeDtypeStruct((n, d), F32),
        compiler_params=_cparams(("arbitrary",)),
        name="combine",
    )(dest_flat, y_pad, x1, gate, mods, final_g)


def _pad_lanes(a, lane0):
    return jnp.zeros((LANES,), F32).at[lane0:lane0 + a.shape[0]].set(a.astype(F32))


def kernel(x, c, ctx, c_ctx, w_mod, b_mod, norm1_g, w_in, pool_w, pool_scale, conv_w,
           a_log_f, dt_bias_f, a_log_b, dt_bias_b, out_norm_g, w_out, norm2_g,
           w_grp, b_grp, w_rt, b_rt, w1, w3, w2, final_g):
    bsz, t, d = x.shape
    depth = w_mod.shape[0]
    assert depth == 1, "single-layer problem: the context stream is read but never updated"
    heads = a_log_f.shape[1]
    pool_width = pool_w.shape[1] * pool_w.shape[2]
    dn_width = heads * HEAD_DIM
    q0 = pool_width
    z0 = q0 + 3 * dn_width
    ab0 = z0 + dn_width
    n_tok = bsz * t
    l = 0

    c8 = jnp.zeros((8, d), F32).at[:bsz].set(c).at[bsz].set(c_ctx)
    mod = _mod_call(c8, w_mod[l], b_mod[l])
    mods = mod.reshape(8, 6, d).transpose(1, 0, 2)

    w_in_bf = w_in[l].astype(BF16)
    wab_bf = jnp.zeros((d, LANES), BF16).at[:, :4 * heads].set(w_in_bf[:, ab0:])
    g1 = norm1_g[l].reshape(1, d)
    proj, ab = _inproj_call(x, mods, g1, w_in_bf, wab_bf, ab0, None)
    tc = ctx.shape[1]
    proj_c, ab_c = _inproj_call(ctx.reshape(1, bsz * tc, d), mods, g1, w_in_bf, wab_bf, ab0, bsz)
    proj_c = proj_c.reshape(bsz, tc, ab0)
    ab_c = ab_c.reshape(bsz, tc, LANES)

    prm = jnp.zeros((8, LANES), F32)
    prm = prm.at[0].set(_pad_lanes(jnp.concatenate([a_log_f[l], a_log_b[l]]), 2 * heads))
    prm = prm.at[1].set(_pad_lanes(jnp.concatenate([dt_bias_f[l], dt_bias_b[l]]), 2 * heads))
    g, g_t = _gates_call(ab, prm, heads)
    gc, gc_t = _gates_call(ab_c, prm, heads)
    gr = g_t.reshape(bsz, LANES, t // CHUNK, CHUNK)
    grc = gc_t.reshape(bsz, LANES, ctx.shape[1] // CHUNK, CHUNK)

    dn = _delta_call(proj, proj_c, conv_w[l], g, gc, gr, grc, out_norm_g[l].reshape(1, HEAD_DIM),
                     heads, q0 // HEAD_DIM, z0 // HEAD_DIM)
    pool = _pool_call(proj, pool_w[l], pool_scale[l].reshape(1, pool_width), t // GRID_W, GRID_W)

    wr = jnp.zeros((d, LANES), F32).at[:, :N_GROUPS].set(w_grp[l]).at[:, EXP_LANE0:EXP_LANE0 + N_EXPERTS].set(w_rt[l])
    br = jnp.zeros((LANES,), F32).at[:N_GROUPS].set(b_grp[l]).at[EXP_LANE0:EXP_LANE0 + N_EXPERTS].set(b_rt[l])
    x1, h2, logits = _outproj_call(pool, dn, w_out[l].astype(BF16), x, mods, norm2_g[l].reshape(1, d),
                                   wr, br.reshape(1, LANES))

    n_blocks = (n_tok * 2 + N_EXPERTS * (MOE_BLOCK - 1) + MOE_BLOCK - 1) // MOE_BLOCK
    dest, gate, be = _router_call(logits.reshape(n_tok, LANES), n_blocks)
    dest_flat = dest[:, :2].reshape(-1)
    block_expert = be[:n_blocks, 0]
    used = be[0:1, 1]
    inv = _slots_call(dest_flat, n_blocks * MOE_BLOCK)
    y_pad = _gmm_call(block_expert, used, inv, h2.reshape(n_tok, d), w1[l], w3[l], w2[l])
    out = _combine_call(dest_flat, y_pad, x1.reshape(n_tok, d), gate, mods, final_g.reshape(1, d), t)
    return out.reshape(bsz, t, d)
```

```python
import functools

import jax
import jax.numpy as jnp
from jax import lax
from jax.experimental import pallas as pl
from jax.experimental.pallas import tpu as pltpu

F32 = jnp.float32
BF16 = jnp.bfloat16

GRID_W = 64
POOL_WINDOWS = (2, 4, 8, 16)
HEAD_DIM = 128
CONV_WIDTH = 5
CHUNK = 64
N_GROUPS = 4
EXPERTS_PER_GROUP = 8
N_EXPERTS = N_GROUPS * EXPERTS_PER_GROUP
MOE_BLOCK = 128
EPS = 1e-6
LANES = 128
PRE_UNROLL = 8
EXP_LANE0 = N_GROUPS

VMEM_LIMIT = 56 * 1024 * 1024


def _cparams(sem):
    return pltpu.CompilerParams(dimension_semantics=sem, vmem_limit_bytes=VMEM_LIMIT)


def _dot(a, b):
    return jnp.dot(a.astype(BF16), b.astype(BF16), preferred_element_type=F32)


def _dot_split(a, b):
    a_hi = a.astype(BF16)
    b_hi = b.astype(BF16)
    a_lo = (a - a_hi.astype(F32)).astype(BF16)
    b_lo = (b - b_hi.astype(F32)).astype(BF16)
    return (jnp.dot(a_hi, b_hi, preferred_element_type=F32)
            + jnp.dot(a_lo, b_hi, preferred_element_type=F32)
            + jnp.dot(a_hi, b_lo, preferred_element_type=F32))


def _silu(x):
    return x * jax.nn.sigmoid(x)


def _mod_kernel(c_ref, w_ref, b_ref, o_ref):
    o_ref[...] = _dot(_silu(c_ref[...]), w_ref[...]) + b_ref[...]


def _mod_call(c8, w_mod, b_mod):
    d, n = w_mod.shape
    tn = 512
    return pl.pallas_call(
        _mod_kernel,
        grid=(n // tn,),
        in_specs=[pl.BlockSpec((8, d), lambda j: (0, 0)),
                  pl.BlockSpec((d, tn), lambda j: (0, j)),
                  pl.BlockSpec((1, tn), lambda j: (0, j))],
        out_specs=pl.BlockSpec((8, tn), lambda j: (0, j)),
        out_shape=jax.ShapeDtypeStruct((8, n), F32),
        compiler_params=_cparams(("parallel",)),
        name="mod",
    )(c8, w_mod, b_mod.reshape(1, n))


def _inproj_kernel(x_ref, mods_ref, g_ref, w_ref, wab_ref, o_ref, ab_ref, hn_ref, *, mod_row):
    b = pl.program_id(0)
    j = pl.program_id(2)

    @pl.when(j == 0)
    def _():
        x = x_ref[0]
        y = x * lax.rsqrt(jnp.mean(x * x, axis=-1, keepdims=True) + EPS) * g_ref[...]
        row = b if mod_row is None else mod_row
        sh = mods_ref[0, pl.ds(row, 1), :]
        sc = mods_ref[1, pl.ds(row, 1), :]
        hb = (y * (1 + sc) + sh).astype(BF16)
        hn_ref[...] = hb
        ab_ref[0] = jnp.dot(hb, wab_ref[...], preferred_element_type=F32)

    o_ref[0] = jnp.dot(hn_ref[...], w_ref[...], preferred_element_type=F32)


def _inproj_call(x, mods, g, w_in, wab_bf, n_main, mod_row):
    bsz, t, d = x.shape
    tm = min(t, 1024)
    tn = 1024
    kern = functools.partial(_inproj_kernel, mod_row=mod_row)
    return pl.pallas_call(
        kern,
        grid=(bsz, t // tm, n_main // tn),
        in_specs=[pl.BlockSpec((1, tm, d), lambda b, i, j: (b, i, 0)),
                  pl.BlockSpec((2, 8, d), lambda b, i, j: (0, 0, 0)),
                  pl.BlockSpec((1, d), lambda b, i, j: (0, 0)),
                  pl.BlockSpec((d, tn), lambda b, i, j: (0, j)),
                  pl.BlockSpec((d, LANES), lambda b, i, j: (0, 0))],
        out_specs=[pl.BlockSpec((1, tm, tn), lambda b, i, j: (b, i, j)),
                   pl.BlockSpec((1, tm, LANES), lambda b, i, j: (b, i, 0))],
        out_shape=[jax.ShapeDtypeStruct((bsz, t, n_main), F32),
                   jax.ShapeDtypeStruct((bsz, t, LANES), F32)],
        scratch_shapes=[pltpu.VMEM((tm, d), BF16)],
        compiler_params=_cparams(("parallel", "parallel", "arbitrary")),
        name="inproj",
    )(x, mods, g, w_in, wab_bf)


def _gates_kernel(ab_ref, prm_ref, g_ref, gt_ref, *, heads):
    ab = ab_ref[0]
    t = ab.shape[0]
    h2, h3, h4, h6 = 2 * heads, 3 * heads, 4 * heads, 6 * heads
    beta = jax.nn.sigmoid(ab)
    xx = ab + prm_ref[1:2, :]
    softplus = jnp.maximum(xx, 0.0) + jnp.log1p(jnp.exp(-jnp.abs(xx)))
    g = -jnp.exp(prm_ref[0:1, :]) * softplus
    pos = lax.broadcasted_iota(jnp.int32, ab.shape, 0) & (CHUNK - 1)
    cs = g
    ss = g
    s = 1
    while s < CHUNK:
        cs = cs + jnp.where(pos >= s, pltpu.roll(cs, s, 0), 0.0)
        ss = ss + jnp.where(pos < CHUNK - s, pltpu.roll(ss, t - s, 0), 0.0)
        s *= 2
    tot = pltpu.roll(cs + ss - g, h2, 1)
    lane = lax.broadcasted_iota(jnp.int32, ab.shape, 1)
    out = jnp.where(lane < h2, beta,
                    jnp.where(lane < h3, cs,
                              jnp.where(lane < h4, ss,
                                        jnp.where(lane < h6, tot, 0.0))))
    g_ref[0] = out
    gt_ref[0] = out.T


def _gates_call(ab, prm, heads):
    bsz, t, _ = ab.shape
    return pl.pallas_call(
        functools.partial(_gates_kernel, heads=heads),
        grid=(bsz,),
        in_specs=[pl.BlockSpec((1, t, LANES), lambda b: (b, 0, 0)),
                  pl.BlockSpec((8, LANES), lambda b: (0, 0))],
        out_specs=[pl.BlockSpec((1, t, LANES), lambda b: (b, 0, 0)),
                   pl.BlockSpec((1, LANES, t), lambda b: (b, 0, 0))],
        out_shape=[jax.ShapeDtypeStruct((bsz, t, LANES), F32),
                   jax.ShapeDtypeStruct((bsz, LANES, t), F32)],
        compiler_params=_cparams(("parallel",)),
        name="gates",
    )(ab, prm)


def _conv_silu(x, w):
    n = x.shape[0]
    row = lax.broadcasted_iota(jnp.int32, x.shape, 0)
    acc = x * w[CONV_WIDTH // 2:CONV_WIDTH // 2 + 1, :]
    for j in range(CONV_WIDTH):
        d = j - CONV_WIDTH // 2
        if d == 0:
            continue
        xs = pltpu.roll(x, (-d) % n, 0)
        valid = (row + d >= 0) & (row + d < n)
        acc = acc + jnp.where(valid, xs, 0.0) * w[j:j + 1, :]
    return _silu(acc)


def _l2norm(a):
    return a * lax.rsqrt(jnp.sum(a * a, axis=-1, keepdims=True) + EPS)


def _lane_col(g, lane_idx):
    lane = lax.broadcasted_iota(jnp.int32, g.shape, 1)
    return jnp.sum(jnp.where(lane == lane_idx, g, 0.0), axis=-1, keepdims=True)


def _chunk_terms(chains, between=()):
    pending = list(between)

    def stage_done():
        if pending:
            pending.pop(0)()

    c, hd = chains[0][0].shape
    ri = lax.broadcasted_iota(jnp.int32, (c, c), 0)
    ci = lax.broadcasted_iota(jnp.int32, (c, c), 1)
    eye = jnp.where(ri == ci, 1.0, 0.0)
    right = lax.broadcasted_iota(jnp.int32, (c, 2 * c), 1) >= c
    nt = (((1,), (1,)), ((), ()))
    tn = (((0,), (0,)), ((), ()))

    decs, kn_bs, zs = [], [], []
    for kn_c, kb_c, _, _, _, _, gcc, gcr, upper in chains:
        incl = (ri <= ci) if upper else (ri >= ci)
        strict = (ri < ci) if upper else (ri > ci)
        dec = jnp.where(incl, jnp.exp(jnp.where(incl, gcc - gcr, 0.0)), 0.0)
        kn_b = kn_c.astype(BF16)
        kk = lax.dot_general(kb_c.astype(BF16), kn_b, nt, preferred_element_type=F32)
        decs.append(dec)
        kn_bs.append(kn_b)
        zs.append(jnp.concatenate([-jnp.where(strict, kk * dec, 0.0), eye], axis=1))
    stage_done()
    n = 1
    while n < c:
        zs = [_dot(z[:, :c], z) + jnp.where(right, z, 0.0) for z in zs]
        stage_done()
        n *= 2
    uw_bs = [_dot(z[:, c:], ch[3]).astype(BF16) for z, ch in zip(zs, chains)]
    stage_done()
    wns = [lax.dot_general(ch[4].astype(BF16), uw_b, tn, preferred_element_type=F32)
           for uw_b, ch in zip(uw_bs, chains)]
    stage_done()
    qks = [None if ch[2] is None else
           lax.dot_general(ch[2].astype(BF16), kn_b, nt, preferred_element_type=F32) * dec
           for ch, kn_b, dec in zip(chains, kn_bs, decs)]
    stage_done()
    qws = [None if qk is None else jnp.dot(qk.astype(BF16), uw_b, preferred_element_type=F32)
           for qk, uw_b in zip(qks, uw_bs)]
    while pending:
        stage_done()
    out = []
    for ch, wn, qw in zip(chains, wns, qws):
        nc, w2 = wn[:, :hd], wn[:, hd:]
        if qw is None:
            out.append((w2, nc, None, None))
        else:
            out.append((w2, nc, ch[5] - qw[:, hd:], qw[:, :hd]))
    return out


def _delta_kernel(q_ref, k_ref, v_ref, z_ref, kc_ref, vc_ref, cwq_ref, cwk_ref, cwv_ref,
                  g_ref, gc_ref, grf_ref, grb_ref, grcf_ref, grcb_ref, ong_ref,
                  out_ref,
                  qn_s, kn_s, kb_s, rhs_s, kd_s, qd_s, col_s,
                  knc_s, kbc_s, rhsc_s, kdc_s, colc_s,
                  w2_s, nc_s, qp_s, o0_s, w2c_s, ncc_s, o_s, *, heads):
    h = pl.program_id(1)
    t = q_ref.shape[1]
    tc = kc_ref.shape[1]
    hd = HEAD_DIM
    n_lat = t // CHUNK
    n_ctx = tc // CHUNK

    qn = _l2norm(_conv_silu(q_ref[0], cwq_ref[...])) * (hd ** -0.5)
    kn = _l2norm(_conv_silu(k_ref[0], cwk_ref[...]))
    vv = _conv_silu(v_ref[0], cwv_ref[...])
    knc = _l2norm(_conv_silu(kc_ref[0], cwk_ref[...]))
    vvc = _conv_silu(vc_ref[0], cwv_ref[...])
    qn_s[...] = qn
    kn_s[...] = kn
    knc_s[...] = knc
    g_lat = g_ref[0]
    g_ctx = gc_ref[0]

    for d in range(2):
        for (gt_, kn_, vv_, kb_r, rhs_r, kd_r, col_r, qn_, qd_r) in (
                (g_lat, kn, vv, kb_s, rhs_s, kd_s, col_s, qn, qd_s),
                (g_ctx, knc, vvc, kbc_s, rhsc_s, kdc_s, colc_s, None, None)):
            beta = _lane_col(gt_, d * heads + h)
            gcum = _lane_col(gt_, (2 + d) * heads + h)
            gtot = _lane_col(gt_, (4 + d) * heads + h)
            e = jnp.exp(gcum)
            kb = kn_ * beta
            kb_r[d] = kb
            rhs_r[d, :, 0:hd] = vv_ * beta
            rhs_r[d, :, hd:2 * hd] = kb * e
            kd_r[d] = kn_ * jnp.exp(gtot - gcum)
            lane = lax.broadcasted_iota(jnp.int32, (gcum.shape[0], LANES), 1)
            col_r[d] = jnp.where(lane == 0, gcum, jnp.where(lane == 1, jnp.exp(gtot), 0.0))
            if qn_ is not None:
                qd_r[d] = qn_ * e

    gr_lat = (grf_ref, grb_ref)
    gr_ctx = (grcf_ref, grcb_ref)

    un_ctx = min(PRE_UNROLL, n_ctx)
    un_lat = min(PRE_UNROLL, n_lat)

    def pre_ctx(i, carry):
        ids = [(i * un_ctx + u, d) for u in range(un_ctx) for d in range(2)]
        chains = []
        for c, d in ids:
            rows = pl.ds(pl.multiple_of(c * CHUNK, CHUNK), CHUNK)
            chains.append((knc_s[rows, :], kbc_s[d, rows, :], None, rhsc_s[d, rows, :], kdc_s[d, rows, :],
                           None, colc_s[d, rows, 0:1], gr_ctx[d][0, 0, pl.ds(c, 1), :], d == 1))
        for (c, d), (w2, nc, _, _) in zip(ids, _chunk_terms(chains)):
            m0 = pl.multiple_of(c * hd, hd)
            w2c_s[d, pl.ds(m0, hd), :] = w2
            ncc_s[d, pl.ds(m0, hd), :] = nc
        return carry

    lax.fori_loop(0, n_ctx // un_ctx, pre_ctx, 0)

    def lat_chunk(i, u, d):
        k = i * un_lat + u
        return k if d == 0 else n_lat - 1 - k

    def pre_lat(i, between=()):
        ids = [(lat_chunk(i, u, d), d) for u in range(un_lat) for d in range(2)]
        chains = []
        for c, d in ids:
            rows = pl.ds(pl.multiple_of(c * CHUNK, CHUNK), CHUNK)
            chains.append((kn_s[rows, :], kb_s[d, rows, :], qn_s[rows, :], rhs_s[d, rows, :], kd_s[d, rows, :],
                           qd_s[d, rows, :], col_s[d, rows, 0:1], gr_lat[d][0, 0, pl.ds(c, 1), :], d == 1))
        for (c, d), (w2, nc, qp, o0) in zip(ids, _chunk_terms(chains, between)):
            rows = pl.ds(pl.multiple_of(c * CHUNK, CHUNK), CHUNK)
            m0 = pl.multiple_of(c * hd, hd)
            w2_s[d, pl.ds(m0, hd), :] = w2
            nc_s[d, pl.ds(m0, hd), :] = nc
            qp_s[d, rows, :] = qp
            o0_s[d, rows, :] = o0

    def scan_ctx_steps(box):
        def step(i):
            def run():
                for d in range(2):
                    c = i if d == 0 else n_ctx - 1 - i
                    gt = colc_s[d, c * CHUNK:c * CHUNK + 1, 1:2]
                    box[d] = (gt * box[d] + ncc_s[d, c * hd:(c + 1) * hd, :]
                              - _dot(w2c_s[d, c * hd:(c + 1) * hd, :], box[d]))
            return run

        return [step(i) for i in range(n_ctx)]

    def scan_lat_steps(i, box):
        def step(u):
            def run():
                for d in range(2):
                    s = box[d]
                    c = lat_chunk(i, u, d)
                    r0 = pl.multiple_of(c * CHUNK, CHUNK)
                    rows = pl.ds(r0, CHUNK)
                    m0 = pl.multiple_of(c * hd, hd)
                    s_b = s.astype(BF16)
                    o_s[d, rows, :] = jnp.dot(qp_s[d, rows, :].astype(BF16), s_b,
                                              preferred_element_type=F32) + o0_s[d, rows, :]
                    gt = col_s[d, pl.ds(r0, 1), 1:2]
                    box[d] = gt * s + nc_s[d, pl.ds(m0, hd), :] - jnp.dot(
                        w2_s[d, pl.ds(m0, hd), :].astype(BF16), s_b, preferred_element_type=F32)
            return run

        return [step(u) for u in range(un_lat)]

    zero = jnp.zeros((hd, hd), F32)
    box = [zero, zero]
    pre_lat(0, scan_ctx_steps(box))
    states = tuple(box)

    def lat_body(i, states):
        box = list(states)
        pre_lat(i, scan_lat_steps(i - 1, box))
        return tuple(box)

    states = lax.fori_loop(1, n_lat // un_lat, lat_body, states)
    box = list(states)
    for run in scan_lat_steps(n_lat // un_lat - 1, box):
        run()

    o = o_s[0] + o_s[1]
    o = o * lax.rsqrt(jnp.mean(o * o, axis=-1, keepdims=True) + EPS) * ong_ref[...]
    out_ref[0] = (o * _silu(z_ref[0])).astype(out_ref.dtype)


def _delta_call(proj, proj_c, conv_w, g, gc, gr, grc, ong, heads, q_blk0, z_blk0):
    bsz, t, _ = proj.shape
    tc = proj_c.shape[1]
    hd = HEAD_DIM
    n_lat, n_ctx = t // CHUNK, tc // CHUNK

    def col(off):
        return lambda b, h: (b, 0, off + h)

    def cw(off):
        return lambda b, h: (0, off + h)

    def grow(off):
        return lambda b, h: (b, off + h, 0, 0)

    in_specs = [
        pl.BlockSpec((1, t, hd), col(q_blk0)),
        pl.BlockSpec((1, t, hd), col(q_blk0 + heads)),
        pl.BlockSpec((1, t, hd), col(q_blk0 + 2 * heads)),
        pl.BlockSpec((1, t, hd), col(z_blk0)),
        pl.BlockSpec((1, tc, hd), col(q_blk0 + heads)),
        pl.BlockSpec((1, tc, hd), col(q_blk0 + 2 * heads)),
        pl.BlockSpec((CONV_WIDTH, hd), cw(0)),
        pl.BlockSpec((CONV_WIDTH, hd), cw(heads)),
        pl.BlockSpec((CONV_WIDTH, hd), cw(2 * heads)),
        pl.BlockSpec((1, t, LANES), lambda b, h: (b, 0, 0)),
        pl.BlockSpec((1, tc, LANES), lambda b, h: (b, 0, 0)),
        pl.BlockSpec((1, 1, n_lat, CHUNK), grow(2 * heads)),
        pl.BlockSpec((1, 1, n_lat, CHUNK), grow(3 * heads)),
        pl.BlockSpec((1, 1, n_ctx, CHUNK), grow(2 * heads)),
        pl.BlockSpec((1, 1, n_ctx, CHUNK), grow(3 * heads)),
        pl.BlockSpec((1, hd), lambda b, h: (0, 0)),
    ]
    scratch = [
        pltpu.VMEM((t, hd), F32), pltpu.VMEM((t, hd), F32),
        pltpu.VMEM((2, t, hd), F32), pltpu.VMEM((2, t, 2 * hd), F32),
        pltpu.VMEM((2, t, hd), F32), pltpu.VMEM((2, t, hd), F32),
        pltpu.VMEM((2, t, LANES), F32),
        pltpu.VMEM((tc, hd), F32), pltpu.VMEM((2, tc, hd), F32),
        pltpu.VMEM((2, tc, 2 * hd), F32), pltpu.VMEM((2, tc, hd), F32),
        pltpu.VMEM((2, tc, LANES), F32),
        pltpu.VMEM((2, n_lat * hd, hd), F32), pltpu.VMEM((2, n_lat * hd, hd), F32),
        pltpu.VMEM((2, t, hd), F32), pltpu.VMEM((2, t, hd), F32),
        pltpu.VMEM((2, n_ctx * hd, hd), F32), pltpu.VMEM((2, n_ctx * hd, hd), F32),
        pltpu.VMEM((2, t, hd), F32),
    ]
    return pl.pallas_call(
        functools.partial(_delta_kernel, heads=heads),
        grid=(bsz, heads),
        in_specs=in_specs,
        out_specs=pl.BlockSpec((1, t, hd), lambda b, h: (b, 0, h)),
        out_shape=jax.ShapeDtypeStruct((bsz, t, heads * hd), BF16),
        scratch_shapes=scratch,
        compiler_params=_cparams(("parallel", "parallel")),
        name="delta",
    )(proj, proj, proj, proj, proj_c, proj_c, conv_w, conv_w, conv_w, g, gc, gr, gr, grc, grc, ong)


def _shift_rows(x, d, idx, size, stride):
    n = x.shape[0]
    xs = pltpu.roll(x, (-d * stride) % n, 0)
    return jnp.where((idx + d >= 0) & (idx + d < size), xs, 0.0)


def _box_sum_1d(x, win, idx, size, stride):
    m = win // 2
    lead = x
    trail = x
    k = 1
    while k < m:
        lead = lead + _shift_rows(lead, k, idx, size, stride)
        trail = trail + _shift_rows(trail, -k, idx, size, stride)
        k *= 2
    return lead + _shift_rows(trail, -1, idx, size, stride)


def _pool_kernel(u_ref, pw_ref, ps_ref, o_ref, *, rows, cols):
    t = u_ref.shape[1]
    gc = pw_ref.shape[1]
    tok = lax.broadcasted_iota(jnp.int32, (t, gc), 0)
    ci = tok % cols
    ri = tok // cols
    for gi, win in enumerate(POOL_WINDOWS):
        lo = win // 2
        hi = win - lo
        u = u_ref[0, :, gi * gc:(gi + 1) * gc]
        s = _box_sum_1d(u, win, ci, cols, 1)
        s = _box_sum_1d(s, win, ri, rows, cols)
        cnt_c = jnp.minimum(ci + hi, cols) - jnp.maximum(ci - lo, 0)
        cnt_r = jnp.minimum(ri + hi, rows) - jnp.maximum(ri - lo, 0)
        mean = s / (cnt_c * cnt_r).astype(F32)
        y = _dot(mean - u, pw_ref[gi]) * ps_ref[:, gi * gc:(gi + 1) * gc]
        o_ref[0, :, gi * gc:(gi + 1) * gc] = y.astype(o_ref.dtype)


def _pool_call(proj, pool_w, pool_scale, rows, cols):
    bsz, t, _ = proj.shape
    ng, gc, _ = pool_w.shape
    pwid = ng * gc
    return pl.pallas_call(
        functools.partial(_pool_kernel, rows=rows, cols=cols),
        grid=(bsz,),
        in_specs=[pl.BlockSpec((1, t, pwid), lambda b: (b, 0, 0)),
                  pl.BlockSpec((ng, gc, gc), lambda b: (0, 0, 0)),
                  pl.BlockSpec((1, pwid), lambda b: (0, 0))],
        out_specs=pl.BlockSpec((1, t, pwid), lambda b: (b, 0, 0)),
        out_shape=jax.ShapeDtypeStruct((bsz, t, pwid), BF16),
        compiler_params=_cparams(("parallel",)),
        name="pool",
    )(proj, pool_w, pool_scale)


def _outproj_kernel(pool_ref, dn_ref, wa_ref, wb_ref, x_ref, mods_ref, g2_ref, wr_ref, br_ref,
                    x1_ref, h2_ref, lg_ref):
    b = pl.program_id(0)
    mix = (jnp.dot(pool_ref[0], wa_ref[...], preferred_element_type=F32)
           + jnp.dot(dn_ref[0], wb_ref[...], preferred_element_type=F32))
    gt1 = mods_ref[2, pl.ds(b, 1), :]
    sh2 = mods_ref[3, pl.ds(b, 1), :]
    sc2 = mods_ref[4, pl.ds(b, 1), :]
    x1 = x_ref[0] + gt1 * mix
    x1_ref[0] = x1
    y = x1 * lax.rsqrt(jnp.mean(x1 * x1, axis=-1, keepdims=True) + EPS) * g2_ref[...]
    h2 = y * (1 + sc2) + sh2
    h2_ref[0] = h2
    lg_ref[0] = _dot_split(h2, wr_ref[...]) + br_ref[...]


def _outproj_call(pool, dn, w_out_bf, x, mods, g2, wr, br):
    bsz, t, d = x.shape
    half = pool.shape[-1]
    tm = min(t, 512)
    return pl.pallas_call(
        _outproj_kernel,
        grid=(bsz, t // tm),
        in_specs=[pl.BlockSpec((1, tm, half), lambda b, i: (b, i, 0)),
                  pl.BlockSpec((1, tm, half), lambda b, i: (b, i, 0)),
                  pl.BlockSpec((half, d), lambda b, i: (0, 0)),
                  pl.BlockSpec((half, d), lambda b, i: (1, 0)),
                  pl.BlockSpec((1, tm, d), lambda b, i: (b, i, 0)),
                  pl.BlockSpec((6, 8, d), lambda b, i: (0, 0, 0)),
                  pl.BlockSpec((1, d), lambda b, i: (0, 0)),
                  pl.BlockSpec((d, LANES), lambda b, i: (0, 0)),
                  pl.BlockSpec((1, LANES), lambda b, i: (0, 0))],
        out_specs=[pl.BlockSpec((1, tm, d), lambda b, i: (b, i, 0)),
                   pl.BlockSpec((1, tm, d), lambda b, i: (b, i, 0)),
                   pl.BlockSpec((1, tm, LANES), lambda b, i: (b, i, 0))],
        out_shape=[jax.ShapeDtypeStruct((bsz, t, d), F32),
                   jax.ShapeDtypeStruct((bsz, t, d), F32),
                   jax.ShapeDtypeStruct((bsz, t, LANES), F32)],
        compiler_params=_cparams(("parallel", "parallel")),
        name="outproj",
    )(pool, dn, w_out_bf, w_out_bf, x, mods, g2, wr, br)


def _router_kernel(lg_ref, dest_ref, gate_ref, be_ref, cnt_s, run_s, off_s):
    p = pl.program_id(0)
    i = pl.program_id(1)
    lg = lg_ref[...]
    tm = lg.shape[0]
    lane = lax.broadcasted_iota(jnp.int32, lg.shape, 1)
    lane_f = lane.astype(F32)
    neg = -jnp.inf
    big = float(LANES)

    grp = jnp.where(lane < N_GROUPS, lg, neg)
    gmax = jnp.max(grp, axis=-1, keepdims=True)
    gidx = jnp.min(jnp.where(grp == gmax, lane_f, big), axis=-1, keepdims=True)
    p_grp = 1.0 / jnp.sum(jnp.where(lane < N_GROUPS, jnp.exp(lg - gmax), 0.0), axis=-1, keepdims=True)
    lo = EXP_LANE0 + EXPERTS_PER_GROUP * gidx
    ev = jnp.where((lane_f >= lo) & (lane_f < lo + EXPERTS_PER_GROUP), lg, neg)
    t1 = jnp.max(ev, axis=-1, keepdims=True)
    i1 = jnp.min(jnp.where(ev == t1, lane_f, big), axis=-1, keepdims=True)
    ev2 = jnp.where(lane_f == i1, neg, ev)
    t2 = jnp.max(ev2, axis=-1, keepdims=True)
    i2 = jnp.min(jnp.where(ev2 == t2, lane_f, big), axis=-1, keepdims=True)
    oh1 = lane_f == i1
    oh2 = lane_f == i2
    cnt = oh1.astype(F32) + oh2.astype(F32)
    colsum = jnp.sum(cnt, axis=0, keepdims=True)

    @pl.when(p == 0)
    def _():
        @pl.when(i == 0)
        def _():
            cnt_s[...] = jnp.zeros_like(cnt_s)

        cnt_s[...] += colsum

    @pl.when(p == 1)
    def _():
        @pl.when(i == 0)
        def _():
            nblk = jnp.floor((cnt_s[...] + (MOE_BLOCK - 1)) * (1.0 / MOE_BLOCK))
            r = lax.broadcasted_iota(jnp.int32, (LANES, LANES), 0)
            c = lax.broadcasted_iota(jnp.int32, (LANES, LANES), 1)
            tri = (r < c).astype(BF16)
            nb8 = jnp.broadcast_to(nblk, (8, LANES))
            start_blk = jnp.dot(nb8.astype(BF16), tri, preferred_element_type=F32)[0:1, :]
            off_s[...] = start_blk * MOE_BLOCK
            run_s[...] = jnp.zeros_like(run_s)
            end_blk = start_blk + nblk
            nb = be_ref.shape[0]
            blk = lax.broadcasted_iota(jnp.int32, (nb, LANES), 0).astype(F32)
            ln = lax.broadcasted_iota(jnp.int32, (nb, LANES), 1)
            is_exp = (ln >= EXP_LANE0) & (ln < EXP_LANE0 + N_EXPERTS)
            done = jnp.sum(jnp.where(is_exp & (end_blk <= blk), 1.0, 0.0), axis=-1, keepdims=True)
            bexp = jnp.minimum(done, N_EXPERTS - 1.0)
            used = jnp.max(jnp.where(is_exp, end_blk, 0.0), axis=-1, keepdims=True)
            be_ref[...] = jnp.where(ln == 0, bexp, jnp.where(ln == 1, used, 0.0)).astype(jnp.int32)

        rr = lax.broadcasted_iota(jnp.int32, (tm, tm), 0)
        cc = lax.broadcasted_iota(jnp.int32, (tm, tm), 1)
        before = (cc < rr).astype(BF16)
        prefix = jnp.dot(before, cnt.astype(BF16), preferred_element_type=F32)
        base = off_s[...] + run_s[...] + prefix
        d1 = jnp.sum(jnp.where(oh1, base, 0.0), axis=-1, keepdims=True)
        d2 = jnp.sum(jnp.where(oh2, base, 0.0), axis=-1, keepdims=True)
        run_s[...] += colsum
        dd = jnp.exp(t2 - t1)
        g1 = p_grp / (1.0 + dd)
        g2 = p_grp * dd / (1.0 + dd)
        dest_ref[...] = jnp.where(lane == 0, d1, jnp.where(lane == 1, d2, 0.0)).astype(jnp.int32)
        gate_ref[...] = jnp.where(lane == 0, g1, jnp.where(lane == 1, g2, 0.0))


def _router_call(logits, n_blocks):
    n = logits.shape[0]
    tm = min(n, 512)
    nb_pad = ((n_blocks + 7) // 8) * 8
    return pl.pallas_call(
        _router_kernel,
        grid=(2, n // tm),
        in_specs=[pl.BlockSpec((tm, LANES), lambda p, i: (i, 0))],
        out_specs=[pl.BlockSpec((tm, LANES), lambda p, i: (i * p, 0)),
                   pl.BlockSpec((tm, LANES), lambda p, i: (i * p, 0)),
                   pl.BlockSpec((nb_pad, LANES), lambda p, i: (0, 0))],
        out_shape=[jax.ShapeDtypeStruct((n, LANES), jnp.int32),
                   jax.ShapeDtypeStruct((n, LANES), F32),
                   jax.ShapeDtypeStruct((nb_pad, LANES), jnp.int32)],
        scratch_shapes=[pltpu.VMEM((1, LANES), F32), pltpu.VMEM((1, LANES), F32),
                        pltpu.VMEM((1, LANES), F32)],
        compiler_params=_cparams(("arbitrary", "arbitrary")),
        name="router",
    )(logits)


def _slots_kernel(dest_ref, zero_ref, inv_ref, sem):
    fill = pltpu.make_async_copy(zero_ref, inv_ref, sem)
    fill.start()
    fill.wait()

    def put(tok, carry):
        inv_ref[dest_ref[2 * tok]] = tok
        inv_ref[dest_ref[2 * tok + 1]] = tok
        return carry

    lax.fori_loop(0, dest_ref.shape[0] // 2, put, 0, unroll=8)


def _slots_call(dest_flat, n_rows):
    return pl.pallas_call(
        _slots_kernel,
        in_specs=[pl.BlockSpec(memory_space=pltpu.SMEM), pl.BlockSpec(memory_space=pl.ANY)],
        out_specs=pl.BlockSpec(memory_space=pltpu.SMEM),
        out_shape=jax.ShapeDtypeStruct((n_rows,), jnp.int32),
        scratch_shapes=[pltpu.SemaphoreType.DMA(())],
        name="slots",
    )(dest_flat, jnp.zeros((n_rows,), jnp.int32))


GMM_GROUP = 5
GMM_TILES = 4
WEIGHT_DMA_SPLIT = 4
WEIGHT_DMA_PRIORITY = 1
WEIGHT_BUFFERS = 4


def _unit_tables(block_expert, used, n_blocks):
    n_units_max = N_EXPERTS + n_blocks // GMM_GROUP
    valid = jnp.arange(n_blocks) < used
    nblk_e = jnp.sum((block_expert[None, :] == jnp.arange(N_EXPERTS)[:, None]) & valid[None, :],
                     axis=1).astype(jnp.int32)
    first_e = jnp.cumsum(nblk_e) - nblk_e
    units_e = (nblk_e + GMM_GROUP - 1) // GMM_GROUP
    uend = jnp.cumsum(units_e)
    n_units = uend[-1]
    u = jnp.arange(n_units_max, dtype=jnp.int32)
    ue = jnp.minimum(jnp.sum(uend[None, :] <= u[:, None], axis=1), N_EXPERTS - 1).astype(jnp.int32)
    k = u - (uend - units_e)[ue]
    live = u < n_units
    ub = jnp.where(live, first_e[ue] + k * GMM_GROUP, 0).astype(jnp.int32)
    un = jnp.where(live, jnp.clip(nblk_e[ue] - k * GMM_GROUP, 0, GMM_GROUP), 0).astype(jnp.int32)
    return ue, ub, un, n_units.reshape(1).astype(jnp.int32)


def _gmm_kernel(ue_ref, ub_ref, un_ref, nu_ref, used_ref, inv_ref, h_ref, w1_ref, w3_ref, w2_ref, y_ref,
                xbuf, xs, acc, w1f, w3f, w2f, w1b_s, w3b_s, w2b_s, gsem, osem, wsem, *, n_blocks):
    u = pl.program_id(0)
    n_units = nu_ref[0]
    slot = u % 2
    d, tj = w1f.shape[1], w1f.shape[2]

    def gather(unit, s):
        base = ub_ref[unit] * MOE_BLOCK

        def body(g, carry):
            for k in range(8):
                r = g * 8 + k
                pltpu.make_async_copy(h_ref.at[pl.ds(inv_ref[base + r], 1)], xbuf.at[s, pl.ds(r, 1)],
                                      gsem.at[s]).start()
            return carry

        lax.fori_loop(0, un_ref[unit] * (MOE_BLOCK // 8), body, 0)

    def wait_gather(unit, s):
        rows = pl.ds(0, un_ref[unit] * MOE_BLOCK)
        pltpu.make_async_copy(xbuf.at[s, rows], xbuf.at[s, rows], gsem.at[s]).wait()

    def blk_rows(i):
        return pl.ds(pl.multiple_of(i * MOE_BLOCK, MOE_BLOCK), MOE_BLOCK)

    def out_copy(unit, s, i):
        dst = pl.ds(pl.multiple_of((ub_ref[unit] + i) * MOE_BLOCK, MOE_BLOCK), MOE_BLOCK)
        return pltpu.make_async_copy(acc.at[s, blk_rows(i)], y_ref.at[dst], osem.at[s])

    def wait_out(unit, s):
        def body(i, carry):
            out_copy(unit, s, i).wait()
            return carry

        lax.fori_loop(0, un_ref[unit], body, 0)

    def weight_copies(unit, j, ws):
        e = ue_ref[unit]
        cols = pl.ds(j * tj, tj)
        copies = []
        for p in range(WEIGHT_DMA_SPLIT):
            rk = pl.ds(p * (d // WEIGHT_DMA_SPLIT), d // WEIGHT_DMA_SPLIT)
            rj = pl.ds(p * (tj // WEIGHT_DMA_SPLIT), tj // WEIGHT_DMA_SPLIT)
            rj_src = pl.ds(j * tj + p * (tj // WEIGHT_DMA_SPLIT), tj // WEIGHT_DMA_SPLIT)
            copies += [pltpu.make_async_copy(w1_ref.at[e, rk, cols], w1f.at[ws, rk], wsem.at[ws]),
                       pltpu.make_async_copy(w3_ref.at[e, rk, cols], w3f.at[ws, rk], wsem.at[ws]),
                       pltpu.make_async_copy(w2_ref.at[e, rj_src, :], w2f.at[ws, rj], wsem.at[ws])]
        return copies

    n_wbuf = w1f.shape[0]
    ahead = n_wbuf - 1

    def tile_buffer(unit, j):
        return (unit * GMM_TILES + j) % n_wbuf

    def tile_after(unit, j, k):
        jj = j + k
        if jj < GMM_TILES:
            return unit, jj
        return jnp.minimum(unit + 1, n_units - 1), jj - GMM_TILES

    @pl.when(u == 0)
    def _():
        gather(0, 0)
        for k in range(ahead):
            for c in weight_copies(0, k, tile_buffer(0, k)):
                c.start(priority=WEIGHT_DMA_PRIORITY)

    def tile_pass(j, ws):
        first = j == 0
        w1b = w1f[ws].astype(BF16)
        w3b = w3f[ws].astype(BF16)
        w2b = w2f[ws].astype(BF16)
        w1b_s[...] = w1b
        w3b_s[...] = w3b
        w2b_s[...] = w2b

        def rows_pass(r0, m, a1, a3, a2):
            rows = pl.ds(r0, m)
            xb = xs[rows, :]
            h1 = jnp.dot(xb, a1, preferred_element_type=F32)
            h3 = jnp.dot(xb, a3, preferred_element_type=F32)
            part = jnp.dot((_silu(h1) * h3).astype(BF16), a2, preferred_element_type=F32)
            if first:
                acc[slot, rows, :] = part
            else:
                acc[slot, rows, :] += part

        rows_pass(0, MOE_BLOCK, w1b, w3b, w2b)
        rest = un_ref[u] - 1

        def pair(i, carry):
            r0 = pl.multiple_of(MOE_BLOCK + i * 2 * MOE_BLOCK, MOE_BLOCK)
            rows_pass(r0, 2 * MOE_BLOCK, w1b_s[...], w3b_s[...], w2b_s[...])
            return carry

        lax.fori_loop(0, rest // 2, pair, 0)

        @pl.when(rest % 2 == 1)
        def _():
            rows_pass(pl.multiple_of(rest * MOE_BLOCK, MOE_BLOCK), MOE_BLOCK,
                      w1b_s[...], w3b_s[...], w2b_s[...])

    @pl.when(u < n_units)
    def _():
        @pl.when(u >= 2)
        def _():
            wait_out(u - 2, slot)

        @pl.when(u + 1 < n_units)
        def _():
            gather(u + 1, 1 - slot)

        wait_gather(u, slot)

        def to_bf16(i, carry):
            xs[blk_rows(i), :] = xbuf[slot, blk_rows(i), :].astype(BF16)
            return carry

        lax.fori_loop(0, un_ref[u], to_bf16, 0)
        for j in range(GMM_TILES):
            nu_, nj = tile_after(u, j, ahead)
            for c in weight_copies(nu_, nj, (tile_buffer(u, j) + ahead) % n_wbuf):
                c.start(priority=WEIGHT_DMA_PRIORITY)
            ws = tile_buffer(u, j)
            for c in weight_copies(u, j, ws):
                c.wait()
            tile_pass(j, ws)

        def start(i, carry):
            out_copy(u, slot, i).start()
            return carry

        lax.fori_loop(0, un_ref[u], start, 0)

    @pl.when(u == pl.num_programs(0) - 1)
    def _():
        last = n_units - 1
        for k in range(ahead):
            for c in weight_copies(last, k, (tile_buffer(last, GMM_TILES - 1) + 1 + k) % n_wbuf):
                c.wait()
        wait_out(last, last % 2)

        @pl.when(n_units >= 2)
        def _():
            wait_out(last - 1, (last - 1) % 2)

        acc[0, blk_rows(0), :] = jnp.zeros((MOE_BLOCK, acc.shape[2]), F32)

        def zero_copy(b):
            dst = pl.ds(pl.multiple_of(b * MOE_BLOCK, MOE_BLOCK), MOE_BLOCK)
            return pltpu.make_async_copy(acc.at[0, blk_rows(0)], y_ref.at[dst], osem.at[0])

        def start(b, carry):
            zero_copy(b).start()
            return carry

        def wait(b, carry):
            zero_copy(b).wait()
            return carry

        lax.fori_loop(used_ref[0], n_blocks, start, 0)
        lax.fori_loop(used_ref[0], n_blocks, wait, 0)


def _gmm_call(block_expert, used, inv, h2, w1, w3, w2):
    n_rows = inv.shape[0]
    d = h2.shape[1]
    de = w1.shape[-1]
    n_blocks = n_rows // MOE_BLOCK
    tj = de // GMM_TILES
    ue, ub, un, nu = _unit_tables(block_expert, used[0], n_blocks)

    rows = GMM_GROUP * MOE_BLOCK
    grid_spec = pltpu.PrefetchScalarGridSpec(
        num_scalar_prefetch=6,
        grid=(ue.shape[0],),
        in_specs=[pl.BlockSpec(memory_space=pl.ANY)] * 4,
        out_specs=pl.BlockSpec(memory_space=pl.ANY),
        scratch_shapes=[pltpu.VMEM((2, rows, d), F32), pltpu.VMEM((rows, d), BF16),
                        pltpu.VMEM((2, rows, d), F32),
                        pltpu.VMEM((WEIGHT_BUFFERS, d, tj), F32), pltpu.VMEM((WEIGHT_BUFFERS, d, tj), F32),
                        pltpu.VMEM((WEIGHT_BUFFERS, tj, d), F32),
                        pltpu.VMEM((d, tj), BF16), pltpu.VMEM((d, tj), BF16), pltpu.VMEM((tj, d), BF16),
                        pltpu.SemaphoreType.DMA((2,)), pltpu.SemaphoreType.DMA((2,)),
                        pltpu.SemaphoreType.DMA((WEIGHT_BUFFERS,))],
    )
    return pl.pallas_call(
        functools.partial(_gmm_kernel, n_blocks=n_blocks),
        grid_spec=grid_spec,
        out_shape=jax.ShapeDtypeStruct((n_rows, d), F32),
        compiler_params=_cparams(("arbitrary",)),
        name="gmm",
    )(ue, ub, un, nu, used, inv, h2, w1, w3, w2)


def _combine_kernel(dest_ref, y_ref, x1_ref, gate_ref, mods_ref, fg_ref, o_ref, ybuf, sem, *, seq):
    i = pl.program_id(0)
    tm = x1_ref.shape[0]
    slot = i % 2

    last = pl.num_programs(0) - 1

    def row_copy(step, r, k, s):
        return pltpu.make_async_copy(y_ref.at[pl.ds(dest_ref[2 * (step * tm + r) + k], 1)],
                                     ybuf.at[s, k, pl.ds(r, 1)], sem.at[s])

    def wait_gather(s):
        pltpu.make_async_copy(ybuf.at[s], ybuf.at[s], sem.at[s]).wait()

    @pl.when(i == 0)
    def _():
        def body(r, carry):
            row_copy(0, r, 0, 0).start(priority=0)
            row_copy(0, r, 1, 0).start(priority=1)
            return carry

        lax.fori_loop(0, tm, body, 0, unroll=4)

    wait_gather(slot)

    nxt = jnp.minimum(i + 1, last)
    for r in range(tm):
        row_copy(nxt, r, 0, 1 - slot).start(priority=0)
        row_copy(nxt, r, 1, 1 - slot).start(priority=1)

    b = (i * tm) // seq
    gt2 = mods_ref[5, pl.ds(b, 1), :]
    gate = gate_ref[...]
    moe = gate[:, 0:1] * ybuf[slot, 0] + gate[:, 1:2] * ybuf[slot, 1]
    x = x1_ref[...] + gt2 * moe
    o_ref[...] = x * lax.rsqrt(jnp.mean(x * x, axis=-1, keepdims=True) + EPS) * fg_ref[...]

    @pl.when(i == last)
    def _():
        wait_gather(1 - slot)


def _combine_call(dest_flat, y_pad, x1, gate, mods, final_g, seq):
    n, d = x1.shape
    tm = min(seq, 256)
    grid_spec = pltpu.PrefetchScalarGridSpec(
        num_scalar_prefetch=1,
        grid=(n // tm,),
        in_specs=[pl.BlockSpec(memory_space=pl.ANY),
                  pl.BlockSpec((tm, d), lambda i, dest: (i, 0)),
                  pl.BlockSpec((tm, LANES), lambda i, dest: (i, 0)),
                  pl.BlockSpec((6, 8, d), lambda i, dest: (0, 0, 0)),
                  pl.BlockSpec((1, d), lambda i, dest: (0, 0))],
        out_specs=pl.BlockSpec((tm, d), lambda i, dest: (i, 0)),
        scratch_shapes=[pltpu.VMEM((2, 2, tm, d), F32), pltpu.SemaphoreType.DMA((2,))],
    )
    return pl.pallas_call(
        functools.partial(_combine_kernel, seq=seq),
        grid_spec=grid_spec,
        out_shape=jax.ShapeDtypeStruct((n, d), F32),
        compiler_params=_cparams(("arbitrary",)),
        name="combine",
    )(dest_flat, y_pad, x1, gate, mods, final_g)


def _pad_lanes(a, lane0):
    return jnp.zeros((LANES,), F32).at[lane0:lane0 + a.shape[0]].set(a.astype(F32))


def kernel(x, c, ctx, c_ctx, w_mod, b_mod, norm1_g, w_in, pool_w, pool_scale, conv_w,
           a_log_f, dt_bias_f, a_log_b, dt_bias_b, out_norm_g, w_out, norm2_g,
           w_grp, b_grp, w_rt, b_rt, w1, w3, w2, final_g):
    bsz, t, d = x.shape
    depth = w_mod.shape[0]
    assert depth == 1, "single-layer problem: the context stream is read but never updated"
    heads = a_log_f.shape[1]
    pool_width = pool_w.shape[1] * pool_w.shape[2]
    dn_width = heads * HEAD_DIM
    q0 = pool_width
    z0 = q0 + 3 * dn_width
    ab0 = z0 + dn_width
    n_tok = bsz * t
    l = 0

    c8 = jnp.zeros((8, d), F32).at[:bsz].set(c).at[bsz].set(c_ctx)
    mod = _mod_call(c8, w_mod[l], b_mod[l])
    mods = mod.reshape(8, 6, d).transpose(1, 0, 2)

    w_in_bf = w_in[l].astype(BF16)
    wab_bf = jnp.zeros((d, LANES), BF16).at[:, :4 * heads].set(w_in_bf[:, ab0:])
    g1 = norm1_g[l].reshape(1, d)
    proj, ab = _inproj_call(x, mods, g1, w_in_bf, wab_bf, ab0, None)
    tc = ctx.shape[1]
    proj_c, ab_c = _inproj_call(ctx.reshape(1, bsz * tc, d), mods, g1, w_in_bf, wab_bf, ab0, bsz)
    proj_c = proj_c.reshape(bsz, tc, ab0)
    ab_c = ab_c.reshape(bsz, tc, LANES)

    prm = jnp.zeros((8, LANES), F32)
    prm = prm.at[0].set(_pad_lanes(jnp.concatenate([a_log_f[l], a_log_b[l]]), 2 * heads))
    prm = prm.at[1].set(_pad_lanes(jnp.concatenate([dt_bias_f[l], dt_bias_b[l]]), 2 * heads))
    g, g_t = _gates_call(ab, prm, heads)
    gc, gc_t = _gates_call(ab_c, prm, heads)
    gr = g_t.reshape(bsz, LANES, t // CHUNK, CHUNK)
    grc = gc_t.reshape(bsz, LANES, ctx.shape[1] // CHUNK, CHUNK)

    dn = _delta_call(proj, proj_c, conv_w[l], g, gc, gr, grc, out_norm_g[l].reshape(1, HEAD_DIM),
                     heads, q0 // HEAD_DIM, z0 // HEAD_DIM)
    pool = _pool_call(proj, pool_w[l], pool_scale[l].reshape(1, pool_width), t // GRID_W, GRID_W)

    wr = jnp.zeros((d, LANES), F32).at[:, :N_GROUPS].set(w_grp[l]).at[:, EXP_LANE0:EXP_LANE0 + N_EXPERTS].set(w_rt[l])
    br = jnp.zeros((LANES,), F32).at[:N_GROUPS].set(b_grp[l]).at[EXP_LANE0:EXP_LANE0 + N_EXPERTS].set(b_rt[l])
    x1, h2, logits = _outproj_call(pool, dn, w_out[l].astype(BF16), x, mods, norm2_g[l].reshape(1, d),
                                   wr, br.reshape(1, LANES))

    n_blocks = (n_tok * 2 + N_EXPERTS * (MOE_BLOCK - 1) + MOE_BLOCK - 1) // MOE_BLOCK
    dest, gate, be = _router_call(logits.reshape(n_tok, LANES), n_blocks)
    dest_flat = dest[:, :2].reshape(-1)
    block_expert = be[:n_blocks, 0]
    used = be[0:1, 1]
    inv = _slots_call(dest_flat, n_blocks * MOE_BLOCK)
    y_pad = _gmm_call(block_expert, used, inv, h2.reshape(n_tok, d), w1[l], w3[l], w2[l])
    out = _combine_call(dest_flat, y_pad, x1.reshape(n_tok, d), gate, mods, final_g.reshape(1, d), t)
    return out.reshape(bsz, t, d)
```
